```python
import jax, jax.numpy as jnp
from jax import lax
import numpy as np

D_MODEL = 1024
BATCH = 8
SEQ = 16384
DEPTH = 4

HEAD_DIM = 128
DILATED_GROUPS = ((128, 1), (512, 4), (2048, 16))
HEADS_PER_GROUP = 4
N_ATTN_HEADS = HEADS_PER_GROUP * len(DILATED_GROUPS)
ATTN_WIDTH = N_ATTN_HEADS * HEAD_DIM
ATTN_OUT = HEADS_PER_GROUP * HEAD_DIM
BLOCK = 128
ROPE_THETA = 500000.0
ROT_DIM = HEAD_DIM // 4
D_CONV = D_MODEL
CONV_K = 31
D_FF = 256 * ((8 * D_MODEL // 3 + 255) // 256)
FFN_K = 3
IN_WIDTH = 3 * ATTN_WIDTH + 2 * D_CONV + 2 * D_MODEL
EPS = 1e-6

kernel_name = "hybrid_dilated_attn_conformer_convffn_adaln"


def rms_norm(x, g):
    x32 = x.astype(jnp.float32)
    y = x32 * lax.rsqrt(jnp.mean(x32 * x32, axis=-1, keepdims=True) + EPS)
    return (y * g.astype(jnp.float32)).astype(x.dtype)


def layer_norm(x, g, b):
    x32 = x.astype(jnp.float32)
    mu = jnp.mean(x32, axis=-1, keepdims=True)
    xc = x32 - mu
    y = xc * lax.rsqrt(jnp.mean(xc * xc, axis=-1, keepdims=True) + EPS)
    return (y * g.astype(jnp.float32) + b.astype(jnp.float32)).astype(x.dtype)


def causal_dwconv(x, w, b):
    K, C = w.shape
    y = lax.conv_general_dilated(x, w[:, None, :].astype(x.dtype), window_strides=(1,),
                                 padding=[(K - 1, 0)], dimension_numbers=("NWC", "WIO", "NWC"),
                                 feature_group_count=C)
    return y + b.astype(x.dtype)


def partial_rope(t, cos, sin):
    half = ROT_DIM // 2
    t1, t2 = t[..., :half], t[..., half:ROT_DIM]
    return jnp.concatenate([t1 * cos - t2 * sin, t2 * cos + t1 * sin, t[..., ROT_DIM:]], axis=-1)


def dilated_band_attention(q, k, v, window, dilation):
    B, S, H, Dh = q.shape
    band = window // dilation
    sub_len = -(-S // (dilation * BLOCK)) * BLOCK
    pad = sub_len * dilation - S
    nb = sub_len // BLOCK

    def to_sub(t):
        t = jnp.pad(t, ((0, 0), (0, pad), (0, 0), (0, 0)))
        t = t.reshape(B, sub_len, dilation, H, Dh).transpose(0, 2, 3, 1, 4)
        return t.reshape(B, dilation, H, nb, BLOCK, Dh)

    def with_prev(t):
        prev = jnp.pad(t[:, :, :, :-1], ((0, 0), (0, 0), (0, 0), (1, 0), (0, 0), (0, 0)))
        return jnp.concatenate([prev, t], axis=4)

    qb = to_sub(q)
    kw = with_prev(to_sub(k))
    vw = with_prev(to_sub(v))
    s = jnp.einsum('brhnqd,brhnkd->brhnqk', qb, kw) * (Dh ** -0.5)
    qi = jnp.arange(BLOCK)[:, None]
    kj = jnp.arange(2 * BLOCK)[None, :]
    dist = qi + BLOCK - kj
    valid = (dist >= 0) & (dist <= band)
    not_first = jnp.arange(nb)[:, None, None] > 0
    valid = valid[None] & (not_first | (kj >= BLOCK)[None])
    s = jnp.where(valid, s, -jnp.inf)
    m = jnp.max(s, axis=-1, keepdims=True)
    p = jnp.exp(s - m)
    l = jnp.sum(p, axis=-1, keepdims=True)
    o = jnp.einsum('brhnqk,brhnkd->brhnqd', p, vw) / l
    lse = (m + jnp.log(l))[..., 0]
    o = o.reshape(B, dilation, H, sub_len, Dh).transpose(0, 3, 1, 2, 4)
    o = o.reshape(B, sub_len * dilation, H, Dh)[:, :S]
    lse = lse.reshape(B, dilation, H, sub_len).transpose(0, 3, 1, 2)
    lse = lse.reshape(B, sub_len * dilation, H)[:, :S]
    return o, lse


def _fwd_setup_inputs(seed: int = 0) -> dict:
    key = jax.random.key(seed)
    ks = jax.random.split(key, 24)
    f32 = jnp.float32

    def nrm(k, shape, fan_in, s=1.0):
        return jax.random.normal(k, shape, f32) * (s * fan_in ** -0.5)

    def gain(k, shape):
        return 1.0 + 0.05 * jax.random.normal(k, shape, f32)

    def bias(k, shape):
        return 0.02 * jax.random.normal(k, shape, f32)

    return {
        "x": jax.random.normal(ks[0], (BATCH, SEQ, D_MODEL), f32),
        "c": jax.random.normal(ks[1], (BATCH, D_MODEL), f32),
        "positions": jnp.broadcast_to(jnp.arange(SEQ, dtype=jnp.int32), (BATCH, SEQ)),
        "w_ada": nrm(ks[2], (DEPTH, D_MODEL, 6 * D_MODEL), D_MODEL, 0.5),
        "b_ada": bias(ks[3], (DEPTH, 6 * D_MODEL)),
        "g_norm1": gain(ks[4], (DEPTH, D_MODEL)),
        "w_in": nrm(ks[5], (DEPTH, D_MODEL, IN_WIDTH), D_MODEL),
        "g_q": gain(ks[6], (DEPTH, HEAD_DIM)),
        "g_k": gain(ks[7], (DEPTH, HEAD_DIM)),
        "w_attn_proj": nrm(ks[8], (DEPTH, ATTN_OUT, D_MODEL), ATTN_OUT),
        "w_conv_dw": nrm(ks[9], (DEPTH, CONV_K, D_CONV), CONV_K),
        "b_conv_dw": bias(ks[10], (DEPTH, D_CONV)),
        "g_conv_ln": gain(ks[11], (DEPTH, D_CONV)),
        "b_conv_ln": bias(ks[12], (DEPTH, D_CONV)),
        "w_conv_out": nrm(ks[13], (DEPTH, D_CONV, D_MODEL), D_CONV),
        "w_o": nrm(ks[14], (DEPTH, D_MODEL, D_MODEL), D_MODEL),
        "g_norm2": gain(ks[15], (DEPTH, D_MODEL)),
        "w_ffn_in": nrm(ks[16], (DEPTH, D_MODEL, 2 * D_FF), D_MODEL),
        "w_ffn_dw": nrm(ks[17], (DEPTH, FFN_K, D_FF), FFN_K),
        "b_ffn_dw": bias(ks[18], (DEPTH, D_FF)),
        "w_ffn_down": nrm(ks[19], (DEPTH, D_FF, D_MODEL), D_FF),
    }


def _fwd_reference(x, c, positions, w_ada, b_ada, g_norm1, w_in, g_q, g_k, w_attn_proj,
              w_conv_dw, b_conv_dw, g_conv_ln, b_conv_ln, w_conv_out, w_o, g_norm2,
              w_ffn_in, w_ffn_dw, b_ffn_dw, w_ffn_down):
    f32 = jnp.float32
    B, S, _ = x.shape
    inv_freq = ROPE_THETA ** (-jnp.arange(0, ROT_DIM, 2, dtype=f32) / ROT_DIM)
    ang = positions.astype(f32)[..., None] * inv_freq
    cos = jnp.cos(ang)[:, :, None, :]
    sin = jnp.sin(ang)[:, :, None, :]
    c_act = jax.nn.silu(c)
    split_at = [ATTN_WIDTH, 2 * ATTN_WIDTH, 3 * ATTN_WIDTH,
                3 * ATTN_WIDTH + D_CONV, 3 * ATTN_WIDTH + 2 * D_CONV,
                3 * ATTN_WIDTH + 2 * D_CONV + D_MODEL]

    for l in range(DEPTH):
        mod = (c_act @ w_ada[l] + b_ada[l])[:, None, :]
        sh1, sc1, gt1, sh2, sc2, gt2 = jnp.split(mod, 6, axis=-1)

        h = rms_norm(x, g_norm1[l]) * (1.0 + sc1) + sh1
        z = h @ w_in[l]
        q, k, v, c_val, c_gate, gate_a, gate_b = jnp.split(z, split_at, axis=-1)

        q = partial_rope(rms_norm(q.astype(f32).reshape(B, S, N_ATTN_HEADS, HEAD_DIM), g_q[l]), cos, sin)
        k = partial_rope(rms_norm(k.astype(f32).reshape(B, S, N_ATTN_HEADS, HEAD_DIM), g_k[l]), cos, sin)
        v = v.astype(f32).reshape(B, S, N_ATTN_HEADS, HEAD_DIM)
        outs, lses = [], []
        for gi, (win, dil) in enumerate(DILATED_GROUPS):
            hs = slice(gi * HEADS_PER_GROUP, (gi + 1) * HEADS_PER_GROUP)
            o_g, lse_g = dilated_band_attention(q[:, :, hs], k[:, :, hs], v[:, :, hs], win, dil)
            outs.append(o_g)
            lses.append(lse_g)
        wts = jax.nn.softmax(jnp.stack(lses, axis=0), axis=0)
        attn = jnp.sum(wts[..., None] * jnp.stack(outs, axis=0), axis=0)
        y_a = attn.reshape(B, S, ATTN_OUT).astype(x.dtype) @ w_attn_proj[l]

        u = c_val * jax.nn.sigmoid(c_gate)
        u = causal_dwconv(u, w_conv_dw[l], b_conv_dw[l])
        u = jax.nn.silu(layer_norm(u, g_conv_ln[l], b_conv_ln[l]))
        y_b = u @ w_conv_out[l]

        merged = jax.nn.sigmoid(gate_a) * y_a + jax.nn.sigmoid(gate_b) * y_b
        x = x + gt1 * (merged @ w_o[l])

        h2 = rms_norm(x, g_norm2[l]) * (1.0 + sc2) + sh2
        gu = h2 @ w_ffn_in[l]
        g_path, u_path = jnp.split(gu, 2, axis=-1)
        g_path = causal_dwconv(g_path, w_ffn_dw[l], b_ffn_dw[l])
        x = x + gt2 * ((jax.nn.silu(g_path) * u_path) @ w_ffn_down[l])

    return x


import jax as _jax
import jax.numpy as _jnp

TWIN_FORMAT = 'train_step'
FWD_PARAMS = ['x', 'c', 'positions', 'w_ada', 'b_ada', 'g_norm1', 'w_in', 'g_q', 'g_k', 'w_attn_proj', 'w_conv_dw', 'b_conv_dw', 'g_conv_ln', 'b_conv_ln', 'w_conv_out', 'w_o', 'g_norm2', 'w_ffn_in', 'w_ffn_dw', 'b_ffn_dw', 'w_ffn_down']
TWIN_WEIGHTS = ['w_ada', 'b_ada', 'g_norm1', 'w_in', 'g_q', 'g_k', 'w_attn_proj', 'w_conv_dw', 'b_conv_dw', 'g_conv_ln', 'b_conv_ln', 'w_conv_out', 'w_o', 'g_norm2', 'w_ffn_in', 'w_ffn_dw', 'b_ffn_dw', 'w_ffn_down']
TWIN_DIFF_INPUT = 'x'
TWIN_INPUTS = ['x', 'c', 'positions', 'w_ada', 'b_ada', 'g_norm1', 'w_in', 'g_q', 'g_k', 'w_attn_proj', 'w_conv_dw', 'b_conv_dw', 'g_conv_ln', 'b_conv_ln', 'w_conv_out', 'w_o', 'g_norm2', 'w_ffn_in', 'w_ffn_dw', 'b_ffn_dw', 'w_ffn_down', 'loss_target', 'm_w_ada', 'm_b_ada', 'm_g_norm1', 'm_w_in', 'm_g_q', 'm_g_k', 'm_w_attn_proj', 'm_w_conv_dw', 'm_b_conv_dw', 'm_g_conv_ln', 'm_b_conv_ln', 'm_w_conv_out', 'm_w_o', 'm_g_norm2', 'm_w_ffn_in', 'm_w_ffn_dw', 'm_b_ffn_dw', 'm_w_ffn_down', 'v_w_ada', 'v_b_ada', 'v_g_norm1', 'v_w_in', 'v_g_q', 'v_g_k', 'v_w_attn_proj', 'v_w_conv_dw', 'v_b_conv_dw', 'v_g_conv_ln', 'v_b_conv_ln', 'v_w_conv_out', 'v_w_o', 'v_g_norm2', 'v_w_ffn_in', 'v_w_ffn_dw', 'v_b_ffn_dw', 'v_w_ffn_down']
TWIN_OUTPUTS = ['loss', 'grad_x', 'grad_w_ada', 'grad_b_ada', 'grad_g_norm1', 'grad_w_in', 'grad_g_q', 'grad_g_k', 'grad_w_attn_proj', 'grad_w_conv_dw', 'grad_b_conv_dw', 'grad_g_conv_ln', 'grad_b_conv_ln', 'grad_w_conv_out', 'grad_w_o', 'grad_g_norm2', 'grad_w_ffn_in', 'grad_w_ffn_dw', 'grad_b_ffn_dw', 'grad_w_ffn_down', 'delta_w_ada', 'delta_b_ada', 'delta_g_norm1', 'delta_w_in', 'delta_g_q', 'delta_g_k', 'delta_w_attn_proj', 'delta_w_conv_dw', 'delta_b_conv_dw', 'delta_g_conv_ln', 'delta_b_conv_ln', 'delta_w_conv_out', 'delta_w_o', 'delta_g_norm2', 'delta_w_ffn_in', 'delta_w_ffn_dw', 'delta_b_ffn_dw', 'delta_w_ffn_down', 'new_m_w_ada', 'new_m_b_ada', 'new_m_g_norm1', 'new_m_w_in', 'new_m_g_q', 'new_m_g_k', 'new_m_w_attn_proj', 'new_m_w_conv_dw', 'new_m_b_conv_dw', 'new_m_g_conv_ln', 'new_m_b_conv_ln', 'new_m_w_conv_out', 'new_m_w_o', 'new_m_g_norm2', 'new_m_w_ffn_in', 'new_m_w_ffn_dw', 'new_m_b_ffn_dw', 'new_m_w_ffn_down', 'new_v_w_ada', 'new_v_b_ada', 'new_v_g_norm1', 'new_v_w_in', 'new_v_g_q', 'new_v_g_k', 'new_v_w_attn_proj', 'new_v_w_conv_dw', 'new_v_b_conv_dw', 'new_v_g_conv_ln', 'new_v_b_conv_ln', 'new_v_w_conv_out', 'new_v_w_o', 'new_v_g_norm2', 'new_v_w_ffn_in', 'new_v_w_ffn_dw', 'new_v_b_ffn_dw', 'new_v_w_ffn_down']
TWIN_LEAF_KINDS = {'loss': 'loss', 'grad_x': 'grad_x', 'grad_w_ada': 'grad_w', 'grad_b_ada': 'grad_w', 'grad_g_norm1': 'grad_w', 'grad_w_in': 'grad_w', 'grad_g_q': 'grad_w', 'grad_g_k': 'grad_w', 'grad_w_attn_proj': 'grad_w', 'grad_w_conv_dw': 'grad_w', 'grad_b_conv_dw': 'grad_w', 'grad_g_conv_ln': 'grad_w', 'grad_b_conv_ln': 'grad_w', 'grad_w_conv_out': 'grad_w', 'grad_w_o': 'grad_w', 'grad_g_norm2': 'grad_w', 'grad_w_ffn_in': 'grad_w', 'grad_w_ffn_dw': 'grad_w', 'grad_b_ffn_dw': 'grad_w', 'grad_w_ffn_down': 'grad_w', 'delta_w_ada': 'delta_w', 'delta_b_ada': 'delta_w', 'delta_g_norm1': 'delta_w', 'delta_w_in': 'delta_w', 'delta_g_q': 'delta_w', 'delta_g_k': 'delta_w', 'delta_w_attn_proj': 'delta_w', 'delta_w_conv_dw': 'delta_w', 'delta_b_conv_dw': 'delta_w', 'delta_g_conv_ln': 'delta_w', 'delta_b_conv_ln': 'delta_w', 'delta_w_conv_out': 'delta_w', 'delta_w_o': 'delta_w', 'delta_g_norm2': 'delta_w', 'delta_w_ffn_in': 'delta_w', 'delta_w_ffn_dw': 'delta_w', 'delta_b_ffn_dw': 'delta_w', 'delta_w_ffn_down': 'delta_w', 'new_m_w_ada': 'new_m', 'new_m_b_ada': 'new_m', 'new_m_g_norm1': 'new_m', 'new_m_w_in': 'new_m', 'new_m_g_q': 'new_m', 'new_m_g_k': 'new_m', 'new_m_w_attn_proj': 'new_m', 'new_m_w_conv_dw': 'new_m', 'new_m_b_conv_dw': 'new_m', 'new_m_g_conv_ln': 'new_m', 'new_m_b_conv_ln': 'new_m', 'new_m_w_conv_out': 'new_m', 'new_m_w_o': 'new_m', 'new_m_g_norm2': 'new_m', 'new_m_w_ffn_in': 'new_m', 'new_m_w_ffn_dw': 'new_m', 'new_m_b_ffn_dw': 'new_m', 'new_m_w_ffn_down': 'new_m', 'new_v_w_ada': 'new_v', 'new_v_b_ada': 'new_v', 'new_v_g_norm1': 'new_v', 'new_v_w_in': 'new_v', 'new_v_g_q': 'new_v', 'new_v_g_k': 'new_v', 'new_v_w_attn_proj': 'new_v', 'new_v_w_conv_dw': 'new_v', 'new_v_b_conv_dw': 'new_v', 'new_v_g_conv_ln': 'new_v', 'new_v_b_conv_ln': 'new_v', 'new_v_w_conv_out': 'new_v', 'new_v_w_o': 'new_v', 'new_v_g_norm2': 'new_v', 'new_v_w_ffn_in': 'new_v', 'new_v_w_ffn_dw': 'new_v', 'new_v_b_ffn_dw': 'new_v', 'new_v_w_ffn_down': 'new_v'}


def _forward(args):
    return _fwd_reference(*[args[k] for k in FWD_PARAMS])


def _output_shape():
    def fwd():
        inp = _fwd_setup_inputs(0)
        return _fwd_reference(*[inp[k] for k in FWD_PARAMS])
    out = _jax.eval_shape(fwd)
    return out.shape, out.dtype

N_MICROBATCH = 1
ADAM_LR = 0.001
ADAM_B1 = 0.9
ADAM_B2 = 0.999
ADAM_EPS = 1e-08
ADAM_WD = 0.01
ADAM_STEP = 10
PER_EXAMPLE_BATCH_AXIS = {'x': 0, 'c': 0, 'positions': 0, 'loss_target': 0}
SHARED_INPUTS = []
_WEIGHT_DTYPES = {'w_ada': _jnp.float32, 'b_ada': _jnp.float32, 'g_norm1': _jnp.float32, 'w_in': _jnp.float32, 'g_q': _jnp.float32, 'g_k': _jnp.float32, 'w_attn_proj': _jnp.float32, 'w_conv_dw': _jnp.float32, 'b_conv_dw': _jnp.float32, 'g_conv_ln': _jnp.float32, 'b_conv_ln': _jnp.float32, 'w_conv_out': _jnp.float32, 'w_o': _jnp.float32, 'g_norm2': _jnp.float32, 'w_ffn_in': _jnp.float32, 'w_ffn_dw': _jnp.float32, 'b_ffn_dw': _jnp.float32, 'w_ffn_down': _jnp.float32}
MOMENT_SCALE = {'w_ada': 2.351910e+00, 'b_ada': 6.588925e+00, 'g_norm1': 1.167718e-01, 'w_in': 5.555833e-02, 'g_q': 1.164081e-01, 'g_k': 1.157846e-01, 'w_attn_proj': 1.228706e-01, 'w_conv_dw': 1.197740e-01, 'b_conv_dw': 9.756925e-01, 'g_conv_ln': 1.561379e+00, 'b_conv_ln': 1.129241e+00, 'w_conv_out': 2.387789e-01, 'w_o': 2.246314e-01, 'g_norm2': 1.314127e+01, 'w_ffn_in': 2.080916e-01, 'w_ffn_dw': 1.525365e+00, 'b_ffn_dw': 1.703250e+00, 'w_ffn_down': 2.061843e-01}


def _to_microbatches(a, axis):
    t = _jnp.moveaxis(a, axis, 0)
    t = t.reshape((N_MICROBATCH, t.shape[0] // N_MICROBATCH) + t.shape[1:])
    return _jnp.moveaxis(t, 1, axis + 1)


def setup_inputs(seed: int = 0) -> dict:
    inp = _fwd_setup_inputs(seed)
    key = _jax.random.fold_in(_jax.random.key(seed), 7919)
    shape, _ = _output_shape()
    out = dict(inp)
    out["loss_target"] = _jax.random.normal(_jax.random.fold_in(key, 0), shape, _jnp.float32)
    for i, name in enumerate(TWIN_WEIGHTS):
        w = inp[name].astype(_jnp.float32)
        if MOMENT_SCALE is None:
            s = _jnp.sqrt(_jnp.mean(_jnp.square(w)) + 1e-30)
        else:
            s = MOMENT_SCALE[name]
        km, kv = _jax.random.split(_jax.random.fold_in(key, i + 1))
        out[name] = w
        out["m_" + name] = s * _jax.random.normal(km, w.shape, _jnp.float32)
        out["v_" + name] = (s * s) * _jax.random.uniform(kv, w.shape, _jnp.float32, 0.5, 1.5)
    if N_MICROBATCH > 1:
        for name, axis in PER_EXAMPLE_BATCH_AXIS.items():
            out[name] = _to_microbatches(out[name], axis)
    return {'x': out['x'], 'c': out['c'], 'positions': out['positions'], 'w_ada': out['w_ada'], 'b_ada': out['b_ada'], 'g_norm1': out['g_norm1'], 'w_in': out['w_in'], 'g_q': out['g_q'], 'g_k': out['g_k'], 'w_attn_proj': out['w_attn_proj'], 'w_conv_dw': out['w_conv_dw'], 'b_conv_dw': out['b_conv_dw'], 'g_conv_ln': out['g_conv_ln'], 'b_conv_ln': out['b_conv_ln'], 'w_conv_out': out['w_conv_out'], 'w_o': out['w_o'], 'g_norm2': out['g_norm2'], 'w_ffn_in': out['w_ffn_in'], 'w_ffn_dw': out['w_ffn_dw'], 'b_ffn_dw': out['b_ffn_dw'], 'w_ffn_down': out['w_ffn_down'], 'loss_target': out['loss_target'], 'm_w_ada': out['m_w_ada'], 'm_b_ada': out['m_b_ada'], 'm_g_norm1': out['m_g_norm1'], 'm_w_in': out['m_w_in'], 'm_g_q': out['m_g_q'], 'm_g_k': out['m_g_k'], 'm_w_attn_proj': out['m_w_attn_proj'], 'm_w_conv_dw': out['m_w_conv_dw'], 'm_b_conv_dw': out['m_b_conv_dw'], 'm_g_conv_ln': out['m_g_conv_ln'], 'm_b_conv_ln': out['m_b_conv_ln'], 'm_w_conv_out': out['m_w_conv_out'], 'm_w_o': out['m_w_o'], 'm_g_norm2': out['m_g_norm2'], 'm_w_ffn_in': out['m_w_ffn_in'], 'm_w_ffn_dw': out['m_w_ffn_dw'], 'm_b_ffn_dw': out['m_b_ffn_dw'], 'm_w_ffn_down': out['m_w_ffn_down'], 'v_w_ada': out['v_w_ada'], 'v_b_ada': out['v_b_ada'], 'v_g_norm1': out['v_g_norm1'], 'v_w_in': out['v_w_in'], 'v_g_q': out['v_g_q'], 'v_g_k': out['v_g_k'], 'v_w_attn_proj': out['v_w_attn_proj'], 'v_w_conv_dw': out['v_w_conv_dw'], 'v_b_conv_dw': out['v_b_conv_dw'], 'v_g_conv_ln': out['v_g_conv_ln'], 'v_b_conv_ln': out['v_b_conv_ln'], 'v_w_conv_out': out['v_w_conv_out'], 'v_w_o': out['v_w_o'], 'v_g_norm2': out['v_g_norm2'], 'v_w_ffn_in': out['v_w_ffn_in'], 'v_w_ffn_dw': out['v_w_ffn_dw'], 'v_b_ffn_dw': out['v_b_ffn_dw'], 'v_w_ffn_down': out['v_w_ffn_down']}


def _loss(weights, diff, rest, loss_target):
    with _jax.named_scope("forward"):
        args = {**rest, TWIN_DIFF_INPUT: diff, **{k: w.astype(_WEIGHT_DTYPES[k]) for k, w in weights.items()}}
        y = _forward(args)
    with _jax.named_scope("loss_head"):
        err = _jnp.square(y.astype(_jnp.float32) - loss_target)
        return 0.5 * _jnp.sum(_jnp.mean(err, axis=-1)) if err.ndim else 0.5 * err


def _adamw(w, g, m, v):
    m = ADAM_B1 * m + (1.0 - ADAM_B1) * g
    v = ADAM_B2 * v + (1.0 - ADAM_B2) * _jnp.square(g)
    m_hat = m / (1.0 - ADAM_B1 ** ADAM_STEP)
    v_hat = v / (1.0 - ADAM_B2 ** ADAM_STEP)
    delta = -ADAM_LR * (m_hat / (_jnp.sqrt(v_hat) + ADAM_EPS) + ADAM_WD * w)
    return delta, m, v


def reference(x, c, positions, w_ada, b_ada, g_norm1, w_in, g_q, g_k, w_attn_proj, w_conv_dw, b_conv_dw, g_conv_ln, b_conv_ln, w_conv_out, w_o, g_norm2, w_ffn_in, w_ffn_dw, b_ffn_dw, w_ffn_down, loss_target, m_w_ada, m_b_ada, m_g_norm1, m_w_in, m_g_q, m_g_k, m_w_attn_proj, m_w_conv_dw, m_b_conv_dw, m_g_conv_ln, m_b_conv_ln, m_w_conv_out, m_w_o, m_g_norm2, m_w_ffn_in, m_w_ffn_dw, m_b_ffn_dw, m_w_ffn_down, v_w_ada, v_b_ada, v_g_norm1, v_w_in, v_g_q, v_g_k, v_w_attn_proj, v_w_conv_dw, v_b_conv_dw, v_g_conv_ln, v_b_conv_ln, v_w_conv_out, v_w_o, v_g_norm2, v_w_ffn_in, v_w_ffn_dw, v_b_ffn_dw, v_w_ffn_down):
    given = dict(x=x, c=c, positions=positions, w_ada=w_ada, b_ada=b_ada, g_norm1=g_norm1, w_in=w_in, g_q=g_q, g_k=g_k, w_attn_proj=w_attn_proj, w_conv_dw=w_conv_dw, b_conv_dw=b_conv_dw, g_conv_ln=g_conv_ln, b_conv_ln=b_conv_ln, w_conv_out=w_conv_out, w_o=w_o, g_norm2=g_norm2, w_ffn_in=w_ffn_in, w_ffn_dw=w_ffn_dw, b_ffn_dw=b_ffn_dw, w_ffn_down=w_ffn_down, loss_target=loss_target, m_w_ada=m_w_ada, m_b_ada=m_b_ada, m_g_norm1=m_g_norm1, m_w_in=m_w_in, m_g_q=m_g_q, m_g_k=m_g_k, m_w_attn_proj=m_w_attn_proj, m_w_conv_dw=m_w_conv_dw, m_b_conv_dw=m_b_conv_dw, m_g_conv_ln=m_g_conv_ln, m_b_conv_ln=m_b_conv_ln, m_w_conv_out=m_w_conv_out, m_w_o=m_w_o, m_g_norm2=m_g_norm2, m_w_ffn_in=m_w_ffn_in, m_w_ffn_dw=m_w_ffn_dw, m_b_ffn_dw=m_b_ffn_dw, m_w_ffn_down=m_w_ffn_down, v_w_ada=v_w_ada, v_b_ada=v_b_ada, v_g_norm1=v_g_norm1, v_w_in=v_w_in, v_g_q=v_g_q, v_g_k=v_g_k, v_w_attn_proj=v_w_attn_proj, v_w_conv_dw=v_w_conv_dw, v_b_conv_dw=v_b_conv_dw, v_g_conv_ln=v_g_conv_ln, v_b_conv_ln=v_b_conv_ln, v_w_conv_out=v_w_conv_out, v_w_o=v_w_o, v_g_norm2=v_g_norm2, v_w_ffn_in=v_w_ffn_in, v_w_ffn_dw=v_w_ffn_dw, v_b_ffn_dw=v_b_ffn_dw, v_w_ffn_down=v_w_ffn_down)
    weights = {n: given[n] for n in TWIN_WEIGHTS}
    shared = {n: given[n] for n in SHARED_INPUTS}
    per_example = {n: given[n] for n in ['x', 'c', 'positions']}
    grad_fn = _jax.value_and_grad(_loss, argnums=(0, 1))

    def one_microbatch(ex, loss_target):
        ex = dict(ex)
        diff = ex.pop(TWIN_DIFF_INPUT)
        return grad_fn(weights, diff, {**shared, **ex}, loss_target)

    if N_MICROBATCH == 1:
        loss, (grad_w, grad_x) = one_microbatch(per_example, given["loss_target"])
    else:
        def body(carry, xs):
            loss_sum, grad_sum = carry
            l_k, (gw_k, gx_k) = one_microbatch(xs[0], xs[1])
            with _jax.named_scope("update"):
                return (loss_sum + l_k, _jax.tree.map(_jnp.add, grad_sum, gw_k)), gx_k

        init = (_jnp.zeros((), _jnp.float32), _jax.tree.map(_jnp.zeros_like, weights))
        (loss, grad_w), grad_x = _jax.lax.scan(body, init, (per_example, given["loss_target"]))
    with _jax.named_scope("update"):
        delta_w, new_m, new_v = {}, {}, {}
        for n in TWIN_WEIGHTS:
            delta_w[n], new_m[n], new_v[n] = _adamw(weights[n], grad_w[n], given["m_" + n], given["v_" + n])
    return (loss, grad_x, *[grad_w[n] for n in TWIN_WEIGHTS], *[delta_w[n] for n in TWIN_WEIGHTS],
            *[new_m[n] for n in TWIN_WEIGHTS], *[new_v[n] for n in TWIN_WEIGHTS])
```

```python
import functools

import jax
import jax.numpy as jnp
from jax import lax
from jax.experimental import pallas as pl
from jax.experimental.pallas import tpu as pltpu

F32 = jnp.float32
BF16 = jnp.bfloat16
SDS = jax.ShapeDtypeStruct
MESH = pl.DeviceIdType.MESH

HEAD_DIM = 128
BLOCK = 128
HEADS_PER_GROUP = 4
GROUP_W = HEADS_PER_GROUP * HEAD_DIM
DILATIONS = (1, 4, 16)
ATTN_W = len(DILATIONS) * GROUP_W
ROT_DIM = HEAD_DIM // 4
ROPE_THETA = 500000.0
CONV_K = 31
CONV_HALO = 32
FFN_K = 3
FFN_HALO = 8
EPS = 1e-6
NEG = -1e30
CHUNK = 512

ADAM_LR = 0.001
ADAM_B1 = 0.9
ADAM_B2 = 0.999
ADAM_EPS = 1e-08
ADAM_WD = 0.01
ADAM_STEP = 10

VMEM_LIMIT_BYTES = 48 * 1024 * 1024
TILE_BYTES = 2 * 1024 * 1024

NT_DIMS = (((1,), (1,)), ((), ()))
TN_DIMS = (((0,), (0,)), ((), ()))


def _cp(*sem):
    return pltpu.CompilerParams(dimension_semantics=sem if sem else None,
                                vmem_limit_bytes=VMEM_LIMIT_BYTES)


def _tm(s, cap=512):
    return min(cap, s)


def _row_tile(rows, cols, itemsize=4):
    best = None
    for t in range(8, rows + 1, 8):
        if rows % t == 0 and t * cols * itemsize <= TILE_BYTES:
            best = t
    return best if best is not None else rows


def _sigmoid(v):
    return jax.nn.sigmoid(v)


def _mod_fwd(c, w_ada, b_ada):
    L, D, N6 = w_ada.shape
    tn = N6 // 4

    def body(c_ref, w_ref, b_ref, mod_ref, cact_ref):
        cv = c_ref[...]
        ca = cv * _sigmoid(cv)
        cact_ref[...] = ca
        a8 = jnp.broadcast_to(ca, (8, D)).astype(BF16)
        acc = jnp.dot(a8, w_ref[0], preferred_element_type=F32)
        mod_ref[0] = acc[0:1, :] + b_ref[0]

    mod, cact = pl.pallas_call(
        body, name="mod_fwd", grid=(L, 4),
        in_specs=[pl.BlockSpec((1, D), lambda l, j: (0, 0)),
                  pl.BlockSpec((1, D, tn), lambda l, j: (l, 0, j)),
                  pl.BlockSpec((1, 1, tn), lambda l, j: (l, 0, j))],
        out_specs=[pl.BlockSpec((1, 1, tn), lambda l, j: (l, 0, j)),
                   pl.BlockSpec((1, D), lambda l, j: (0, 0))],
        out_shape=[SDS((L, 1, N6), F32), SDS((1, D), F32)],
        compiler_params=_cp("arbitrary", "arbitrary"),
    )(c, w_ada, b_ada.reshape(L, 1, N6))
    return mod.reshape(L, N6), cact


def _normmod_mm(x, g, sc, sh, w, l, name):
    S, D = x.shape
    N = w.shape[2]
    tm, tn = _tm(S, 1024), CHUNK

    def body(x_ref, g_ref, sc_ref, sh_ref, w_ref, z_ref, h_ref, hs):
        @pl.when(pl.program_id(1) == 0)
        def _():
            xv = x_ref[...]
            r = lax.rsqrt(jnp.mean(xv * xv, axis=-1, keepdims=True) + EPS)
            hv = (xv * r) * g_ref[...] * (1.0 + sc_ref[...]) + sh_ref[...]
            hs[...] = hv.astype(BF16)
            h_ref[...] = hs[...]
        z_ref[...] = jnp.dot(hs[...], w_ref[0], preferred_element_type=F32)

    vec = pl.BlockSpec((1, D), lambda i, j: (0, 0))
    return pl.pallas_call(
        body, name=name, grid=(S // tm, N // tn),
        in_specs=[pl.BlockSpec((tm, D), lambda i, j: (i, 0)), vec, vec, vec,
                  pl.BlockSpec((1, D, tn), lambda i, j: (l, 0, j))],
        out_specs=[pl.BlockSpec((tm, tn), lambda i, j: (i, j)),
                   pl.BlockSpec((tm, D), lambda i, j: (i, 0))],
        out_shape=[SDS((S, N), F32), SDS((S, D), BF16)],
        scratch_shapes=[pltpu.VMEM((tm, D), BF16)],
        compiler_params=_cp("parallel", "arbitrary"),
    )(x, g, sc, sh, w)


def _rope(t, cos_t, sa_t, sb_t):
    return t * cos_t + pltpu.roll(t, HEAD_DIM - ROT_DIM // 2, 1) * sa_t + pltpu.roll(t, ROT_DIM // 2, 1) * sb_t


def _unrope(d, cos_t, sa_t, sb_t):
    return d * cos_t + pltpu.roll(d * sa_t, ROT_DIM // 2, 1) + pltpu.roll(d * sb_t, HEAD_DIM - ROT_DIM // 2, 1)


def _qk_prep(z, g_q, g_k, tabs):
    S = z.shape[0]
    tm = _tm(S)
    n_heads = ATTN_W // HEAD_DIM

    def body(q_ref, k_ref, gq_ref, gk_ref, c_ref, sa_ref, sb_ref, qo_ref, ko_ref):
        cos_t, sa_t, sb_t = c_ref[...], sa_ref[...], sb_ref[...]
        for src, gref, dst in ((q_ref, gq_ref, qo_ref), (k_ref, gk_ref, ko_ref)):
            gv = gref[...]
            for h in range(n_heads):
                sl = slice(h * HEAD_DIM, (h + 1) * HEAD_DIM)
                t = src[:, sl]
                r = lax.rsqrt(jnp.mean(t * t, axis=-1, keepdims=True) + EPS)
                dst[:, sl] = _rope((t * r) * gv, cos_t, sa_t, sb_t).astype(BF16)

    tab = pl.BlockSpec((tm, HEAD_DIM), lambda i: (i, 0))
    gsp = pl.BlockSpec((1, HEAD_DIM), lambda i: (0, 0))
    return pl.pallas_call(
        body, name="qk_prep", grid=(S // tm,),
        in_specs=[pl.BlockSpec((tm, ATTN_W), lambda i: (i, 0)),
                  pl.BlockSpec((tm, ATTN_W), lambda i: (i, 1)), gsp, gsp, tab, tab, tab],
        out_specs=[pl.BlockSpec((tm, ATTN_W), lambda i: (i, 0))] * 2,
        out_shape=[SDS((S, ATTN_W), BF16)] * 2,
        compiler_params=_cp("parallel"),
    )(z, z, g_q, g_k, *tabs)


def _attn_fwd(qn, kn, z, gi):
    d = DILATIONS[gi]
    S, N = z.shape
    Sd = S // d
    nb = Sd // BLOCK
    q2 = qn.reshape(Sd, d * ATTN_W)
    k2 = kn.reshape(Sd, d * ATTN_W)
    z2 = z.reshape(Sd, d * N)
    gpr = ATTN_W // GROUP_W
    vcol = N // GROUP_W
    voff = 2 * ATTN_W // GROUP_W + gi
    scale = HEAD_DIM ** -0.5

    def body(q_ref, kc_ref, kp_ref, vc_ref, vp_ref, o_ref, lse_ref):
        n = pl.program_id(1)
        qi = lax.broadcasted_iota(jnp.int32, (BLOCK, BLOCK), 0)
        kj = lax.broadcasted_iota(jnp.int32, (BLOCK, BLOCK), 1)
        mask_c = kj <= qi
        mask_p = jnp.logical_and(kj >= qi, n > 0)
        for h in range(HEADS_PER_GROUP):
            sl = slice(h * HEAD_DIM, (h + 1) * HEAD_DIM)
            q = q_ref[:, sl]
            s_c = lax.dot_general(q, kc_ref[:, sl], NT_DIMS, preferred_element_type=F32) * scale
            s_p = lax.dot_general(q, kp_ref[:, sl], NT_DIMS, preferred_element_type=F32) * scale
            s_c = jnp.where(mask_c, s_c, NEG)
            s_p = jnp.where(mask_p, s_p, NEG)
            m = jnp.maximum(jnp.max(s_c, axis=-1, keepdims=True), jnp.max(s_p, axis=-1, keepdims=True))
            p_c = jnp.exp(s_c - m)
            p_p = jnp.exp(s_p - m)
            den = jnp.sum(p_c, axis=-1, keepdims=True) + jnp.sum(p_p, axis=-1, keepdims=True)
            acc = jnp.dot(p_c.astype(BF16), vc_ref[:, sl].astype(BF16), preferred_element_type=F32)
            acc = acc + jnp.dot(p_p.astype(BF16), vp_ref[:, sl].astype(BF16), preferred_element_type=F32)
            o_ref[:, sl] = acc / den
            lse_ref[:, sl] = jnp.broadcast_to(m + jnp.log(den), (BLOCK, HEAD_DIM))

    def blk(colfn, prev):
        if prev:
            return pl.BlockSpec((BLOCK, GROUP_W), lambda r, n: (jnp.maximum(n - 1, 0), colfn(r)))
        return pl.BlockSpec((BLOCK, GROUP_W), lambda r, n: (n, colfn(r)))

    qcol = lambda r: r * gpr + gi
    vcolf = lambda r: r * vcol + voff
    ocol = lambda r: r
    o, lse = pl.pallas_call(
        body, name=f"attn_fwd_g{gi}", grid=(d, nb),
        in_specs=[blk(qcol, False), blk(qcol, False), blk(qcol, True), blk(vcolf, False), blk(vcolf, True)],
        out_specs=[blk(ocol, False)] * 2,
        out_shape=[SDS((Sd, d * GROUP_W), F32)] * 2,
        compiler_params=_cp("parallel", "parallel"),
    )(q2, k2, k2, z2, z2)
    return o.reshape(S, GROUP_W), lse.reshape(S, GROUP_W)


def _group_weights(lses):
    m = jnp.maximum(jnp.maximum(lses[0], lses[1]), lses[2])
    es = [jnp.exp(v - m) for v in lses]
    inv = 1.0 / (es[0] + es[1] + es[2])
    return [e * inv for e in es]


def _attn_combine(outs, lses):
    S = outs[0].shape[0]
    tm = _tm(S)

    def body(o0, o1, o2, l0, l1, l2, a_ref):
        w = _group_weights([l0[...], l1[...], l2[...]])
        a_ref[...] = (w[0] * o0[...] + w[1] * o1[...] + w[2] * o2[...]).astype(BF16)

    sp = pl.BlockSpec((tm, GROUP_W), lambda i: (i, 0))
    return pl.pallas_call(
        body, name="attn_combine", grid=(S // tm,), in_specs=[sp] * 6, out_specs=sp,
        out_shape=SDS((S, GROUP_W), BF16), compiler_params=_cp("parallel"),
    )(*outs, *lses)


def _mm_nn(a, w, l, name, res=None, gate=None):
    S, K = a.shape
    N = w.shape[2]
    tm = _tm(S)
    gated = res is not None

    def body(*refs):
        if gated:
            a_ref, w_ref, r_ref, g_ref, t_ref, o_ref = refs
        else:
            a_ref, w_ref, t_ref = refs
        t = jnp.dot(a_ref[...], w_ref[0], preferred_element_type=F32)
        t_ref[...] = t
        if gated:
            o_ref[...] = r_ref[...] + g_ref[...] * t

    row = pl.BlockSpec((tm, N), lambda i: (i, 0))
    in_specs = [pl.BlockSpec((tm, K), lambda i: (i, 0)), pl.BlockSpec((1, K, N), lambda i: (l, 0, 0))]
    args = [a, w]
    if gated:
        in_specs += [row, pl.BlockSpec((1, N), lambda i: (0, 0))]
        args += [res, gate]
    n_out = 2 if gated else 1
    out = pl.pallas_call(
        body, name=name, grid=(S // tm,), in_specs=in_specs, out_specs=[row] * n_out,
        out_shape=[SDS((S, N), F32)] * n_out, compiler_params=_cp("parallel"),
    )(*args)
    return out if gated else out[0]


def _glu(cv, cg):
    return cv * _sigmoid(cg)


def _glu_conv(z, w_dw, b_dw, l, D):
    S = z.shape[0]
    tm = _tm(S)
    cc0 = 3 * ATTN_W // CHUNK
    ncc = D // CHUNK
    hb = tm // CONV_HALO

    def body(cv_ref, cg_ref, hv_ref, hg_ref, w_ref, b_ref, o_ref, ext):
        i = pl.program_id(1)
        halo = _glu(hv_ref[...], hg_ref[...])
        ext[0:CONV_HALO, :] = jnp.where(i > 0, halo, 0.0)
        ext[CONV_HALO:, :] = _glu(cv_ref[...], cg_ref[...])
        acc = jnp.broadcast_to(b_ref[...], (tm, CHUNK))
        for k in range(CONV_K):
            acc = acc + w_ref[0, k:k + 1, :] * ext[pl.ds(CONV_HALO - (CONV_K - 1) + k, tm), :]
        o_ref[...] = acc

    cur = lambda off: pl.BlockSpec((tm, CHUNK), lambda cc, i: (i, cc0 + off + cc))
    halo = lambda off: pl.BlockSpec((CONV_HALO, CHUNK), lambda cc, i: (jnp.maximum(i * hb - 1, 0), cc0 + off + cc))
    return pl.pallas_call(
        body, name="glu_conv", grid=(ncc, S // tm),
        in_specs=[cur(0), cur(ncc), halo(0), halo(ncc),
                  pl.BlockSpec((1, CONV_K, CHUNK), lambda cc, i: (l, 0, cc)),
                  pl.BlockSpec((None, 1, CHUNK), lambda cc, i: (l, 0, cc))],
        out_specs=pl.BlockSpec((tm, CHUNK), lambda cc, i: (i, cc)),
        out_shape=SDS((S, D), F32),
        scratch_shapes=[pltpu.VMEM((tm + CONV_HALO, CHUNK), F32)],
        compiler_params=_cp("parallel", "parallel"),
    )(z, z, z, z, w_dw, b_dw[:, None, :])


def _ln_stats(u):
    mu = jnp.mean(u, axis=-1, keepdims=True)
    xc = u - mu
    rstd = lax.rsqrt(jnp.mean(xc * xc, axis=-1, keepdims=True) + EPS)
    return xc * rstd, rstd


def _ln_silu(u1, g, b):
    S, D = u1.shape
    tm = _tm(S)

    def body(u_ref, g_ref, b_ref, o_ref):
        xh, _ = _ln_stats(u_ref[...])
        yv = xh * g_ref[...] + b_ref[...]
        o_ref[...] = (yv * _sigmoid(yv)).astype(BF16)

    vec = pl.BlockSpec((1, D), lambda i: (0, 0))
    return pl.pallas_call(
        body, name="ln_silu", grid=(S // tm,),
        in_specs=[pl.BlockSpec((tm, D), lambda i: (i, 0)), vec, vec],
        out_specs=pl.BlockSpec((tm, D), lambda i: (i, 0)),
        out_shape=SDS((S, D), BF16), compiler_params=_cp("parallel"),
    )(u1, g, b)


def _merge(z, y_a, y_b, D):
    S = z.shape[0]
    tm = _tm(S)
    ncc = D // CHUNK
    ga0 = (3 * ATTN_W + 2 * D) // CHUNK

    def body(ga_ref, gb_ref, ya_ref, yb_ref, o_ref):
        o_ref[...] = (_sigmoid(ga_ref[...]) * ya_ref[...] + _sigmoid(gb_ref[...]) * yb_ref[...]).astype(BF16)

    ysp = pl.BlockSpec((tm, CHUNK), lambda i, cc: (i, cc))
    return pl.pallas_call(
        body, name="merge", grid=(S // tm, ncc),
        in_specs=[pl.BlockSpec((tm, CHUNK), lambda i, cc: (i, ga0 + cc)),
                  pl.BlockSpec((tm, CHUNK), lambda i, cc: (i, ga0 + ncc + cc)), ysp, ysp],
        out_specs=ysp, out_shape=SDS((S, D), BF16), compiler_params=_cp("parallel", "parallel"),
    )(z, z, y_a, y_b)


def _ffn_chunk(F):
    best = 128
    for t in range(128, min(F, 1536) + 1, 128):
        if F % t == 0:
            best = t
    return best


def _ffn_conv(ext, w_ref, b_ref, tm):
    gp = jnp.broadcast_to(b_ref[...], (tm, ext.shape[1]))
    for k in range(FFN_K):
        gp = gp + w_ref[0, k:k + 1, :] * ext[pl.ds(FFN_HALO - (FFN_K - 1) + k, tm), :]
    return gp


def _ffn_act(gu, w_dw, b_dw, l):
    S, F2 = gu.shape
    F = F2 // 2
    cw = _ffn_chunk(F)
    ncc = F // cw
    tm = _tm(S)
    hb = tm // FFN_HALO

    def body(g_ref, u_ref, h_ref, w_ref, b_ref, o_ref, ext):
        i = pl.program_id(1)
        ext[0:FFN_HALO, :] = jnp.where(i > 0, h_ref[...], 0.0)
        ext[FFN_HALO:, :] = g_ref[...]
        gp = _ffn_conv(ext, w_ref, b_ref, tm)
        o_ref[...] = (gp * _sigmoid(gp) * u_ref[...]).astype(BF16)

    return pl.pallas_call(
        body, name="ffn_act", grid=(ncc, S // tm),
        in_specs=[pl.BlockSpec((tm, cw), lambda cc, i: (i, cc)),
                  pl.BlockSpec((tm, cw), lambda cc, i: (i, ncc + cc)),
                  pl.BlockSpec((FFN_HALO, cw), lambda cc, i: (jnp.maximum(i * hb - 1, 0), cc)),
                  pl.BlockSpec((1, FFN_K, cw), lambda cc, i: (l, 0, cc)),
                  pl.BlockSpec((None, 1, cw), lambda cc, i: (l, 0, cc))],
        out_specs=pl.BlockSpec((tm, cw), lambda cc, i: (i, cc)),
        out_shape=SDS((S, F), BF16),
        scratch_shapes=[pltpu.VMEM((tm + FFN_HALO, cw), F32)],
        compiler_params=_cp("parallel", "parallel"),
    )(gu, gu, gu, w_dw, b_dw[:, None, :])


def _loss_grad(y, target):
    S, D = y.shape
    tm = _tm(S)

    def body(y_ref, t_ref, dy_ref, l_ref):
        @pl.when(pl.program_id(0) == 0)
        def _():
            l_ref[...] = jnp.zeros_like(l_ref)
        e = y_ref[...] - t_ref[...]
        dy_ref[...] = e * (1.0 / D)
        l_ref[...] += jnp.broadcast_to(0.5 * jnp.sum(jnp.mean(e * e, axis=-1, keepdims=True)), (1, HEAD_DIM))

    row = pl.BlockSpec((tm, D), lambda i: (i, 0))
    dy, lsum = pl.pallas_call(
        body, name="loss_grad", grid=(S // tm,), in_specs=[row, row],
        out_specs=[row, pl.BlockSpec((1, HEAD_DIM), lambda i: (0, 0))],
        out_shape=[SDS((S, D), F32), SDS((1, HEAD_DIM), F32)],
        compiler_params=_cp("arbitrary"),
    )(y, target)
    return dy, lsum[0, 0]


def _gate_bwd(dx, t, gate, name):
    S, D = dx.shape
    tm = _tm(S)

    def body(dx_ref, t_ref, g_ref, e_ref, dg_ref):
        @pl.when(pl.program_id(0) == 0)
        def _():
            dg_ref[...] = jnp.zeros_like(dg_ref)
        dxv = dx_ref[...]
        e_ref[...] = (dxv * g_ref[...]).astype(BF16)
        dg_ref[...] += jnp.sum(dxv * t_ref[...], axis=0, keepdims=True)

    row = pl.BlockSpec((tm, D), lambda i: (i, 0))
    vec = pl.BlockSpec((1, D), lambda i: (0, 0))
    return pl.pallas_call(
        body, name=name, grid=(S // tm,), in_specs=[row, row, vec], out_specs=[row, vec],
        out_shape=[SDS((S, D), BF16), SDS((1, D), F32)], compiler_params=_cp("arbitrary"),
    )(dx, t, gate)


def _mm_nt(dy, w, l, name):
    S, C = dy.shape
    K = w.shape[1]
    tm = _tm(S, 1024 if K <= 1024 else 512)
    tc = CHUNK if C % CHUNK == 0 and C > 1024 else C
    nc = C // tc

    def body(dy_ref, w_ref, o_ref, acc):
        kk = pl.program_id(1)

        @pl.when(kk == 0)
        def _():
            acc[...] = jnp.zeros_like(acc)
        acc[...] += lax.dot_general(dy_ref[...], w_ref[0], NT_DIMS, preferred_element_type=F32)

        @pl.when(kk == nc - 1)
        def _():
            o_ref[...] = acc[...]

    return pl.pallas_call(
        body, name=name, grid=(S // tm, nc),
        in_specs=[pl.BlockSpec((tm, tc), lambda i, kk: (i, kk)),
                  pl.BlockSpec((1, K, tc), lambda i, kk: (l, 0, kk))],
        out_specs=pl.BlockSpec((tm, K), lambda i, kk: (i, 0)),
        out_shape=SDS((S, K), F32),
        scratch_shapes=[pltpu.VMEM((tm, K), F32)],
        compiler_params=_cp("parallel", "arbitrary"),
    )(dy, w)


def _mm_tn(a, b, buf, l, n_layers, col_sharded, name):
    S, K = a.shape
    N = b.shape[1]
    ts = _tm(S, 1024)
    ns = S // ts
    if col_sharded:
        cols, r2 = N // 4, K // 2
        grid = (4, ns)

        def body(*refs):
            a_ref, b_ref, o_ref = refs[0], refs[1], refs[-1]
            s = pl.program_id(1)

            @pl.when(s == 0)
            def _():
                o_ref[...] = jnp.zeros_like(o_ref)
            for hh in range(2):
                o_ref[hh] += lax.dot_general(a_ref[:, hh * r2:(hh + 1) * r2], b_ref[...], TN_DIMS,
                                             preferred_element_type=F32)

        in_specs = [pl.BlockSpec((ts, K), lambda jj, s: (s, 0)), pl.BlockSpec((ts, cols), lambda jj, s: (s, jj))]
        out_spec = pl.BlockSpec((None, None, 2, r2, cols), lambda jj, s: (l, jj, 0, 0, 0))
        scratch = []
    else:
        cols, r2 = N, K // 8
        tn = CHUNK if N % CHUNK == 0 else N
        grid = (N // tn, ns)

        def body(*refs):
            a_ref, b_ref, o_ref, acc = refs[0], refs[1], refs[-2], refs[-1]
            s = pl.program_id(1)

            @pl.when(s == 0)
            def _():
                acc[...] = jnp.zeros_like(acc)
            acc[...] += lax.dot_general(a_ref[...], b_ref[...], TN_DIMS, preferred_element_type=F32)

            @pl.when(s == ns - 1)
            def _():
                for jj in range(4):
                    for hh in range(2):
                        o_ref[jj, hh] = acc[(2 * jj + hh) * r2:(2 * jj + hh + 1) * r2, :]

        in_specs = [pl.BlockSpec((ts, K), lambda nb, s: (s, 0)), pl.BlockSpec((ts, tn), lambda nb, s: (s, nb))]
        out_spec = pl.BlockSpec((None, 4, 2, r2, tn), lambda nb, s: (l, 0, 0, 0, nb))
        scratch = [pltpu.VMEM((K, tn), F32)]
    args = [a, b]
    aliases = {}
    if buf is not None:
        in_specs = in_specs + [pl.BlockSpec(memory_space=pl.ANY)]
        args.append(buf)
        aliases = {2: 0}
    return pl.pallas_call(
        body, name=name, grid=grid, in_specs=in_specs, out_specs=out_spec,
        out_shape=SDS((n_layers, 4, 2, r2, cols), F32), scratch_shapes=scratch,
        input_output_aliases=aliases, compiler_params=_cp("parallel", "arbitrary"),
    )(*args)


def _norm_bwd(dh, x, dres, g, sc, name):
    S, D = x.shape
    tm = _tm(S)

    def body(dh_ref, x_ref, dr_ref, g_ref, sc_ref, dx_ref, sums_ref):
        @pl.when(pl.program_id(0) == 0)
        def _():
            sums_ref[...] = jnp.zeros_like(sums_ref)
        xv, dhv = x_ref[...], dh_ref[...]
        r = lax.rsqrt(jnp.mean(xv * xv, axis=-1, keepdims=True) + EPS)
        xn = xv * r
        gv, sv = g_ref[...], 1.0 + sc_ref[...]
        dxn = dhv * gv * sv
        dx_ref[...] = dr_ref[...] + r * (dxn - xn * jnp.mean(dxn * xn, axis=-1, keepdims=True))
        dhxn = dhv * xn
        sums_ref[0:1, :] += jnp.sum(dhv, axis=0, keepdims=True)
        sums_ref[1:2, :] += jnp.sum(dhxn * gv, axis=0, keepdims=True)
        sums_ref[2:3, :] += jnp.sum(dhxn * sv, axis=0, keepdims=True)

    row = pl.BlockSpec((tm, D), lambda i: (i, 0))
    vec = pl.BlockSpec((1, D), lambda i: (0, 0))
    return pl.pallas_call(
        body, name=name, grid=(S // tm,), in_specs=[row, row, row, vec, vec],
        out_specs=[row, pl.BlockSpec((8, D), lambda i: (0, 0))],
        out_shape=[SDS((S, D), F32), SDS((8, D), F32)], compiler_params=_cp("arbitrary"),
    )(dh, x, dres, g, sc)


def _ffn_act_bwd1(da, gu, w_dw, b_dw, l):
    S, F2 = gu.shape
    F = F2 // 2
    cw = _ffn_chunk(F)
    ncc = F // cw
    tm = _tm(S)
    hb = tm // FFN_HALO

    def body(da_ref, g_ref, u_ref, h_ref, w_ref, b_ref, dgp_ref, du_ref, ext):
        i = pl.program_id(1)
        ext[0:FFN_HALO, :] = jnp.where(i > 0, h_ref[...], 0.0)
        ext[FFN_HALO:, :] = g_ref[...]
        gp = _ffn_conv(ext, w_ref, b_ref, tm)
        sg = _sigmoid(gp)
        dav = da_ref[...]
        du_ref[...] = (dav * gp * sg).astype(BF16)
        dgp_ref[...] = dav * u_ref[...] * (sg * (1.0 + gp * (1.0 - sg)))

    return pl.pallas_call(
        body, name="ffn_act_bwd1", grid=(ncc, S // tm),
        in_specs=[pl.BlockSpec((tm, cw), lambda cc, i: (i, cc)),
                  pl.BlockSpec((tm, cw), lambda cc, i: (i, cc)),
                  pl.BlockSpec((tm, cw), lambda cc, i: (i, ncc + cc)),
                  pl.BlockSpec((FFN_HALO, cw), lambda cc, i: (jnp.maximum(i * hb - 1, 0), cc)),
                  pl.BlockSpec((1, FFN_K, cw), lambda cc, i: (l, 0, cc)),
                  pl.BlockSpec((None, 1, cw), lambda cc, i: (l, 0, cc))],
        out_specs=[pl.BlockSpec((tm, cw), lambda cc, i: (i, cc)),
                   pl.BlockSpec((tm, cw), lambda cc, i: (i, ncc + cc))],
        out_shape=[SDS((S, F), F32), SDS((S, F2), BF16)],
        scratch_shapes=[pltpu.VMEM((tm + FFN_HALO, cw), F32)],
        compiler_params=_cp("parallel", "parallel"),
    )(da, gu, gu, gu, w_dw, b_dw[:, None, :])


def _ffn_act_bwd2(dgp, dgu, gu, w_dw, l):
    S, F = dgp.shape
    cw = _ffn_chunk(F)
    ncc = F // cw
    tm = _tm(S)
    ni = S // tm
    hb = tm // FFN_HALO
    last_hb = S // FFN_HALO - 1

    def body(d_ref, dn_ref, g_ref, gh_ref, w_ref, _, dg_ref, sums_ref, extd, extg):
        i = pl.program_id(1)

        @pl.when(i == 0)
        def _():
            sums_ref[...] = jnp.zeros_like(sums_ref)
        dv = d_ref[...]
        extd[0:tm, :] = dv
        extd[tm:, :] = jnp.where(i < ni - 1, dn_ref[...], 0.0)
        extg[0:FFN_HALO, :] = jnp.where(i > 0, gh_ref[...], 0.0)
        extg[FFN_HALO:, :] = g_ref[...]
        acc = jnp.zeros((tm, cw), F32)
        for k in range(FFN_K):
            acc = acc + w_ref[0, k:k + 1, :] * extd[pl.ds(FFN_K - 1 - k, tm), :]
            sums_ref[k:k + 1, :] += jnp.sum(dv * extg[pl.ds(FFN_HALO - (FFN_K - 1) + k, tm), :], axis=0, keepdims=True)
        sums_ref[FFN_K:FFN_K + 1, :] += jnp.sum(dv, axis=0, keepdims=True)
        dg_ref[...] = acc.astype(BF16)

    return pl.pallas_call(
        body, name="ffn_act_bwd2", grid=(ncc, ni),
        in_specs=[pl.BlockSpec((tm, cw), lambda cc, i: (i, cc)),
                  pl.BlockSpec((FFN_HALO, cw), lambda cc, i: (jnp.minimum((i + 1) * hb, last_hb), cc)),
                  pl.BlockSpec((tm, cw), lambda cc, i: (i, cc)),
                  pl.BlockSpec((FFN_HALO, cw), lambda cc, i: (jnp.maximum(i * hb - 1, 0), cc)),
                  pl.BlockSpec((1, FFN_K, cw), lambda cc, i: (l, 0, cc)),
                  pl.BlockSpec(memory_space=pl.ANY)],
        out_specs=[pl.BlockSpec((tm, cw), lambda cc, i: (i, cc)),
                   pl.BlockSpec((8, cw), lambda cc, i: (0, cc))],
        out_shape=[SDS((S, 2 * F), BF16), SDS((8, F), F32)],
        scratch_shapes=[pltpu.VMEM((tm + FFN_HALO, cw), F32), pltpu.VMEM((tm + FFN_HALO, cw), F32)],
        input_output_aliases={5: 0},
        compiler_params=_cp("parallel", "arbitrary"),
    )(dgp, dgp, gu, gu, w_dw, dgu)


def _merge_bwd(dm, z, y_a, y_b, D):
    S = z.shape[0]
    tm = _tm(S)
    ncc = D // CHUNK
    ga0 = (3 * ATTN_W + 2 * D) // CHUNK

    def body(dm_ref, ga_ref, gb_ref, ya_ref, yb_ref, dya_ref, dyb_ref, dga_ref, dgb_ref):
        dmv = dm_ref[...]
        sa, sb = _sigmoid(ga_ref[...]), _sigmoid(gb_ref[...])
        dya_ref[...] = (dmv * sa).astype(BF16)
        dyb_ref[...] = (dmv * sb).astype(BF16)
        dga_ref[...] = (dmv * ya_ref[...] * sa * (1.0 - sa)).astype(BF16)
        dgb_ref[...] = (dmv * yb_ref[...] * sb * (1.0 - sb)).astype(BF16)

    ysp = pl.BlockSpec((tm, CHUNK), lambda i, cc: (i, cc))
    return pl.pallas_call(
        body, name="merge_bwd", grid=(S // tm, ncc),
        in_specs=[ysp, pl.BlockSpec((tm, CHUNK), lambda i, cc: (i, ga0 + cc)),
                  pl.BlockSpec((tm, CHUNK), lambda i, cc: (i, ga0 + ncc + cc)), ysp, ysp],
        out_specs=[ysp] * 4, out_shape=[SDS((S, D), BF16)] * 4,
        compiler_params=_cp("parallel", "parallel"),
    )(dm, z, z, y_a, y_b)


def _ln_silu_bwd(du3, u1, g, b):
    S, D = u1.shape
    tm = _tm(S)

    def body(d_ref, u_ref, g_ref, b_ref, du_ref, sums_ref):
        @pl.when(pl.program_id(0) == 0)
        def _():
            sums_ref[...] = jnp.zeros_like(sums_ref)
        xh, rstd = _ln_stats(u_ref[...])
        gv = g_ref[...]
        yv = xh * gv + b_ref[...]
        sg = _sigmoid(yv)
        dy = d_ref[...] * (sg * (1.0 + yv * (1.0 - sg)))
        dxh = dy * gv
        du_ref[...] = rstd * (dxh - jnp.mean(dxh, axis=-1, keepdims=True)
                              - xh * jnp.mean(dxh * xh, axis=-1, keepdims=True))
        sums_ref[0:1, :] += jnp.sum(dy * xh, axis=0, keepdims=True)
        sums_ref[1:2, :] += jnp.sum(dy, axis=0, keepdims=True)

    row = pl.BlockSpec((tm, D), lambda i: (i, 0))
    vec = pl.BlockSpec((1, D), lambda i: (0, 0))
    return pl.pallas_call(
        body, name="ln_silu_bwd", grid=(S // tm,), in_specs=[row, row, vec, vec],
        out_specs=[row, pl.BlockSpec((8, D), lambda i: (0, 0))],
        out_shape=[SDS((S, D), F32), SDS((8, D), F32)], compiler_params=_cp("arbitrary"),
    )(du3, u1, g, b)


def _glu_conv_bwd(du1, z, w_dw, l, D):
    S = z.shape[0]
    tm = _tm(S)
    ni = S // tm
    cc0 = 3 * ATTN_W // CHUNK
    ncc = D // CHUNK
    hb = tm // CONV_HALO
    last_hb = S // CONV_HALO - 1

    def body(d_ref, dn_ref, cv_ref, cg_ref, hv_ref, hg_ref, w_ref, dcv_ref, dcg_ref, sums_ref, extd, extu):
        i = pl.program_id(1)

        @pl.when(i == 0)
        def _():
            sums_ref[...] = jnp.zeros_like(sums_ref)
        dv = d_ref[...]
        extd[0:tm, :] = dv
        extd[tm:, :] = jnp.where(i < ni - 1, dn_ref[...], 0.0)
        cv, cg = cv_ref[...], cg_ref[...]
        sg = _sigmoid(cg)
        extu[0:CONV_HALO, :] = jnp.where(i > 0, _glu(hv_ref[...], hg_ref[...]), 0.0)
        extu[CONV_HALO:, :] = cv * sg
        acc = jnp.zeros((tm, CHUNK), F32)
        for k in range(CONV_K):
            acc = acc + w_ref[0, k:k + 1, :] * extd[pl.ds(CONV_K - 1 - k, tm), :]
            sums_ref[k:k + 1, :] += jnp.sum(
                dv * extu[pl.ds(CONV_HALO - (CONV_K - 1) + k, tm), :], axis=0, keepdims=True)
        sums_ref[CONV_K:CONV_K + 1, :] += jnp.sum(dv, axis=0, keepdims=True)
        dcv_ref[...] = (acc * sg).astype(BF16)
        dcg_ref[...] = (acc * cv * sg * (1.0 - sg)).astype(BF16)

    cur = lambda off: pl.BlockSpec((tm, CHUNK), lambda cc, i: (i, cc0 + off + cc))
    halo = lambda off: pl.BlockSpec((CONV_HALO, CHUNK), lambda cc, i: (jnp.maximum(i * hb - 1, 0), cc0 + off + cc))
    osp = pl.BlockSpec((tm, CHUNK), lambda cc, i: (i, cc))
    return pl.pallas_call(
        body, name="glu_conv_bwd", grid=(ncc, ni),
        in_specs=[osp,
                  pl.BlockSpec((CONV_HALO, CHUNK), lambda cc, i: (jnp.minimum((i + 1) * hb, last_hb), cc)),
                  cur(0), cur(ncc), halo(0), halo(ncc),
                  pl.BlockSpec((1, CONV_K, CHUNK), lambda cc, i: (l, 0, cc))],
        out_specs=[osp, osp, pl.BlockSpec((CONV_HALO, CHUNK), lambda cc, i: (0, cc))],
        out_shape=[SDS((S, D), BF16), SDS((S, D), BF16), SDS((CONV_HALO, D), F32)],
        scratch_shapes=[pltpu.VMEM((tm + CONV_HALO, CHUNK), F32), pltpu.VMEM((tm + CONV_HALO, CHUNK), F32)],
        compiler_params=_cp("parallel", "arbitrary"),
    )(du1, du1, z, z, z, z, w_dw)


def _attn_bwd_prep(dattn, outs, lses):
    S = dattn.shape[0]
    tm = _tm(S)

    def body(da_ref, o0, o1, o2, l0, l1, l2, d0, d1, d2, t0, t1, t2):
        w = _group_weights([l0[...], l1[...], l2[...]])
        dav = da_ref[...]
        prod = dav * (w[0] * o0[...] + w[1] * o1[...] + w[2] * o2[...])
        for h in range(HEADS_PER_GROUP):
            sl = slice(h * HEAD_DIM, (h + 1) * HEAD_DIM)
            rs = jnp.broadcast_to(jnp.sum(prod[:, sl], axis=-1, keepdims=True), (tm, HEAD_DIM))
            for wg, t_ref in zip(w, (t0, t1, t2)):
                t_ref[:, sl] = -wg[:, sl] * rs
        for wg, d_ref in zip(w, (d0, d1, d2)):
            d_ref[...] = (wg * dav).astype(BF16)

    sp = pl.BlockSpec((tm, GROUP_W), lambda i: (i, 0))
    out = pl.pallas_call(
        body, name="attn_bwd_prep", grid=(S // tm,), in_specs=[sp] * 7, out_specs=[sp] * 6,
        out_shape=[SDS((S, GROUP_W), BF16)] * 3 + [SDS((S, GROUP_W), F32)] * 3,
        compiler_params=_cp("parallel"),
    )(dattn, *outs, *lses)
    return out[:3], out[3:]


def _attn_bwd(qn, kn, z, do, lse, dterm, gi):
    d = DILATIONS[gi]
    S, N = z.shape
    Sd = S // d
    nb = Sd // BLOCK
    q2 = qn.reshape(Sd, d * ATTN_W)
    k2 = kn.reshape(Sd, d * ATTN_W)
    z2 = z.reshape(Sd, d * N)
    do2 = do.reshape(Sd, d * GROUP_W)
    lse2 = lse.reshape(Sd, d * GROUP_W)
    dt2 = dterm.reshape(Sd, d * GROUP_W)
    gpr = ATTN_W // GROUP_W
    vcol = N // GROUP_W
    voff = 2 * ATTN_W // GROUP_W + gi
    scale = HEAD_DIM ** -0.5

    def body(q_ref, qx_ref, kc_ref, kp_ref, vc_ref, vp_ref, do_ref, dox_ref, l_ref, lx_ref, t_ref, tx_ref,
             dq_ref, dk_ref, dv_ref):
        n = pl.program_id(1)
        qi = lax.broadcasted_iota(jnp.int32, (BLOCK, BLOCK), 0)
        kj = lax.broadcasted_iota(jnp.int32, (BLOCK, BLOCK), 1)
        mask_a = kj <= qi
        mask_b = jnp.logical_and(kj >= qi, n > 0)
        mask_c = jnp.logical_and(kj >= qi, n < nb - 1)

        def pair(q, k, v, dov, lv, tv, mask):
            s = lax.dot_general(q, k, NT_DIMS, preferred_element_type=F32) * scale
            p = jnp.exp(jnp.where(mask, s - lv, NEG))
            dp = lax.dot_general(dov, v, NT_DIMS, preferred_element_type=F32)
            ds = p * (dp + tv) * scale
            return p.astype(BF16), ds.astype(BF16)

        for h in range(HEADS_PER_GROUP):
            sl = slice(h * HEAD_DIM, (h + 1) * HEAD_DIM)
            q, qx = q_ref[:, sl], qx_ref[:, sl]
            kc, kp = kc_ref[:, sl], kp_ref[:, sl]
            vc, vp = vc_ref[:, sl].astype(BF16), vp_ref[:, sl].astype(BF16)
            dov, dox = do_ref[:, sl], dox_ref[:, sl]
            p_a, ds_a = pair(q, kc, vc, dov, l_ref[:, sl], t_ref[:, sl], mask_a)
            _, ds_b = pair(q, kp, vp, dov, l_ref[:, sl], t_ref[:, sl], mask_b)
            p_c, ds_c = pair(qx, kc, vc, dox, lx_ref[:, sl], tx_ref[:, sl], mask_c)
            dq_ref[:, sl] = (jnp.dot(ds_a, kc, preferred_element_type=F32)
                             + jnp.dot(ds_b, kp, preferred_element_type=F32))
            dk_ref[:, sl] = (lax.dot_general(ds_a, q, TN_DIMS, preferred_element_type=F32)
                             + lax.dot_general(ds_c, qx, TN_DIMS, preferred_element_type=F32))
            dv_ref[:, sl] = (lax.dot_general(p_a, dov, TN_DIMS, preferred_element_type=F32)
                             + lax.dot_general(p_c, dox, TN_DIMS, preferred_element_type=F32)).astype(BF16)

    def blk(colfn, shift):
        if shift < 0:
            return pl.BlockSpec((BLOCK, GROUP_W), lambda r, n: (jnp.maximum(n - 1, 0), colfn(r)))
        if shift > 0:
            return pl.BlockSpec((BLOCK, GROUP_W), lambda r, n: (jnp.minimum(n + 1, nb - 1), colfn(r)))
        return pl.BlockSpec((BLOCK, GROUP_W), lambda r, n: (n, colfn(r)))

    qcol = lambda r: r * gpr + gi
    vcolf = lambda r: r * vcol + voff
    ocol = lambda r: r
    dq, dk, dv = pl.pallas_call(
        body, name=f"attn_bwd_g{gi}", grid=(d, nb),
        in_specs=[blk(qcol, 0), blk(qcol, 1), blk(qcol, 0), blk(qcol, -1), blk(vcolf, 0), blk(vcolf, -1),
                  blk(ocol, 0), blk(ocol, 1), blk(ocol, 0), blk(ocol, 1), blk(ocol, 0), blk(ocol, 1)],
        out_specs=[blk(ocol, 0)] * 3,
        out_shape=[SDS((Sd, d * GROUP_W), F32)] * 2 + [SDS((Sd, d * GROUP_W), BF16)],
        compiler_params=_cp("parallel", "parallel"),
    )(q2, q2, k2, k2, z2, z2, do2, do2, lse2, lse2, dt2, dt2)
    return dq.reshape(S, GROUP_W), dk.reshape(S, GROUP_W), dv.reshape(S, GROUP_W)


def _qk_prep_bwd(dqs, dks, z, g_q, g_k, tabs):
    S = z.shape[0]
    tm = _tm(S)
    n_heads = ATTN_W // HEAD_DIM

    def body(dq0, dq1, dq2, dk0, dk1, dk2, q_ref, k_ref, gq_ref, gk_ref, c_ref, sa_ref, sb_ref,
             dqo_ref, dko_ref, sums_ref):
        @pl.when(pl.program_id(0) == 0)
        def _():
            sums_ref[...] = jnp.zeros_like(sums_ref)
        cos_t, sa_t, sb_t = c_ref[...], sa_ref[...], sb_ref[...]
        for row, (drefs, src, gref, dst) in enumerate((((dq0, dq1, dq2), q_ref, gq_ref, dqo_ref),
                                                       ((dk0, dk1, dk2), k_ref, gk_ref, dko_ref))):
            gv = gref[...]
            gsum = jnp.zeros((1, HEAD_DIM), F32)
            for h in range(n_heads):
                sl = slice(h * HEAD_DIM, (h + 1) * HEAD_DIM)
                hs = slice((h % HEADS_PER_GROUP) * HEAD_DIM, (h % HEADS_PER_GROUP + 1) * HEAD_DIM)
                dyn = _unrope(drefs[h // HEADS_PER_GROUP][:, hs], cos_t, sa_t, sb_t)
                t = src[:, sl]
                r = lax.rsqrt(jnp.mean(t * t, axis=-1, keepdims=True) + EPS)
                xh = t * r
                gsum = gsum + jnp.sum(dyn * xh, axis=0, keepdims=True)
                gy = dyn * gv
                dst[:, sl] = (r * (gy - xh * jnp.mean(gy * xh, axis=-1, keepdims=True))).astype(BF16)
            sums_ref[row:row + 1, :] += gsum

    grp = pl.BlockSpec((tm, GROUP_W), lambda i: (i, 0))
    tab = pl.BlockSpec((tm, HEAD_DIM), lambda i: (i, 0))
    gsp = pl.BlockSpec((1, HEAD_DIM), lambda i: (0, 0))
    wide = pl.BlockSpec((tm, ATTN_W), lambda i: (i, 0))
    return pl.pallas_call(
        body, name="qk_prep_bwd", grid=(S // tm,),
        in_specs=[grp] * 6 + [wide, pl.BlockSpec((tm, ATTN_W), lambda i: (i, 1)), gsp, gsp, tab, tab, tab],
        out_specs=[wide, wide, pl.BlockSpec((8, HEAD_DIM), lambda i: (0, 0))],
        out_shape=[SDS((S, ATTN_W), BF16), SDS((S, ATTN_W), BF16), SDS((8, HEAD_DIM), F32)],
        compiler_params=_cp("arbitrary"),
    )(*dqs, *dks, z, z, g_q, g_k, *tabs)


def _adamw(w, g, m, v, name):
    R, C = w.shape
    tr = _row_tile(R, C * 2)
    c1 = 1.0 - ADAM_B1 ** ADAM_STEP
    c2 = 1.0 - ADAM_B2 ** ADAM_STEP

    def body(w_ref, g_ref, m_ref, v_ref, d_ref, mo_ref, vo_ref):
        gv = g_ref[...]
        mn = ADAM_B1 * m_ref[...] + (1.0 - ADAM_B1) * gv
        vn = ADAM_B2 * v_ref[...] + (1.0 - ADAM_B2) * (gv * gv)
        mo_ref[...] = mn
        vo_ref[...] = vn
        d_ref[...] = -ADAM_LR * ((mn / c1) / (jnp.sqrt(vn / c2) + ADAM_EPS) + ADAM_WD * w_ref[...])

    sp = pl.BlockSpec((tr, C), lambda i: (i, 0))
    return pl.pallas_call(
        body, name=name, grid=(R // tr,), in_specs=[sp] * 4, out_specs=[sp] * 3,
        out_shape=[SDS((R, C), F32)] * 3, compiler_params=_cp("parallel"),
    )(w, g, m, v)


def _sum_parts(parts):
    n, R, C = parts.shape

    def body(p_ref, o_ref):
        acc = p_ref[0]
        for k in range(1, n):
            acc = acc + p_ref[k]
        o_ref[...] = acc

    return pl.pallas_call(
        body, name="sum_parts", out_shape=SDS((R, C), F32),
        compiler_params=pltpu.CompilerParams(vmem_limit_bytes=VMEM_LIMIT_BYTES),
    )(parts)


def _wada_grad(cact_t, dmod):
    D, n = cact_t.shape
    L, _, cols = dmod.shape

    def body(c_ref, d_ref, o_ref):
        acc = c_ref[:, 0:1] * d_ref[0, 0:1, :]
        for k in range(1, n):
            acc = acc + c_ref[:, k:k + 1] * d_ref[0, k:k + 1, :]
        o_ref[0] = acc

    return pl.pallas_call(
        body, name="wada_grad", grid=(L,),
        in_specs=[pl.BlockSpec((D, n), lambda l: (0, 0)), pl.BlockSpec((1, n, cols), lambda l: (l, 0, 0))],
        out_specs=pl.BlockSpec((1, D, cols), lambda l: (l, 0, 0)),
        out_shape=SDS((L, D, cols), F32), compiler_params=_cp("parallel"),
    )(cact_t, dmod)


def _add_halves(dw, land, c_idx, name):
    L, _, _, r2, cols = dw.shape
    tr = _row_tile(r2, cols)
    dw4 = dw.reshape(L * 4, 2, r2, cols)
    land3 = land.reshape(L * 4, r2, cols)

    def body(pf, a_ref, b_ref, o_ref):
        o_ref[...] = a_ref[...] + b_ref[...]

    out = pl.pallas_call(
        body, name=name,
        grid_spec=pltpu.PrefetchScalarGridSpec(
            num_scalar_prefetch=1, grid=(L * 4, r2 // tr),
            in_specs=[pl.BlockSpec((None, None, tr, cols), lambda a, i, pf: (a, pf[0], i, 0)),
                      pl.BlockSpec((None, tr, cols), lambda a, i, pf: (a, i, 0))],
            out_specs=pl.BlockSpec((None, tr, cols), lambda a, i, pf: (a, i, 0))),
        out_shape=SDS((L * 4, r2, cols), F32), compiler_params=_cp("parallel", "parallel"),
    )(c_idx, dw4, land3)
    return out.reshape(L, 4, r2, cols)


def _add_chips(s1, land2, j_idx, name):
    L, _, r2, cols = s1.shape
    tr = _row_tile(r2, cols)

    def body(pf, a_ref, b0, b1, b2, o_ref):
        o_ref[...] = ((a_ref[...] + b0[...]) + b1[...]) + b2[...]

    lsp = lambda p: pl.BlockSpec((None, None, tr, cols), lambda l, i, pf: (p, l, i, 0))
    return pl.pallas_call(
        body, name=name,
        grid_spec=pltpu.PrefetchScalarGridSpec(
            num_scalar_prefetch=1, grid=(L, r2 // tr),
            in_specs=[pl.BlockSpec((None, None, tr, cols), lambda l, i, pf: (l, pf[0], i, 0)),
                      lsp(0), lsp(1), lsp(2)],
            out_specs=pl.BlockSpec((None, tr, cols), lambda l, i, pf: (l, i, 0))),
        out_shape=SDS((L, r2, cols), F32), compiler_params=_cp("parallel", "parallel"),
    )(j_idx, s1, land2, land2, land2)


HBM_SPEC = pl.BlockSpec(memory_space=pltpu.HBM)


def _place():
    x, y, c = lax.axis_index("x"), lax.axis_index("y"), lax.axis_index("c")
    chips = [(x, 1 - y), (1 - x, y), (1 - x, 1 - y)]
    return x, y, c, chips


def _remote(src, dst, send_sem, recv_sem, device):
    return pltpu.make_async_remote_copy(src_ref=src, dst_ref=dst, send_sem=send_sem, recv_sem=recv_sem,
                                        device_id=device, device_id_type=MESH)


def _comm_call(body, name, ins, out_shapes, n_sems):
    return pl.pallas_call(
        body, name=name, in_specs=[HBM_SPEC] * len(ins), out_specs=[HBM_SPEC] * len(out_shapes),
        out_shape=out_shapes,
        scratch_shapes=[pltpu.SemaphoreType.DMA((n,)) for n in n_sems],
    )(*ins)


def _all_gather_weights(shards, small):
    nw = len(shards)
    L = shards[0].shape[0]
    Lh = L // 2

    def body(*refs):
        sh, sm = refs[:nw], refs[nw]
        full, smo = refs[nw + 1:2 * nw + 1], refs[2 * nw + 1]
        ici_s, ici_r, d2d_s, d2d_r, loc = refs[2 * nw + 2:]
        x, y, c, chips = _place()
        j = 2 * x + y
        jps = [2 * px + py for px, py in chips]
        me, sib = (x, y, c), (x, y, 1 - c)
        mine, theirs = pl.ds(c * Lh, Lh), pl.ds((1 - c) * Lh, Lh)

        local = [pltpu.make_async_copy(sh[w], full[w].at[j], loc.at[w]) for w in range(nw)]
        local.append(pltpu.make_async_copy(sm, smo.at[j], loc.at[nw]))
        for cp in local:
            cp.start()
        sends = []
        for w in range(nw):
            for p in range(3):
                sends.append(_remote(sh[w].at[mine], full[w].at[j, mine], ici_s.at[3 * w + p], ici_r.at[3 * w + p],
                                     (*chips[p], c)))
        for p in range(3):
            sends.append(_remote(sm, smo.at[j], ici_s.at[3 * nw + p], ici_r.at[3 * nw + p], (*chips[p], c)))
        for cp in sends:
            cp.start()
        passed = []
        for w in range(nw):
            for p in range(3):
                slot = full[w].at[jps[p], mine]
                _remote(slot, slot, ici_s.at[3 * w + p], ici_r.at[3 * w + p], me).wait_recv()
                cp = _remote(slot, slot, d2d_s.at[3 * w + p], d2d_r.at[3 * w + p], sib)
                cp.start()
                passed.append(cp)
        for p in range(3):
            slot = smo.at[jps[p]]
            _remote(slot, slot, ici_s.at[3 * nw + p], ici_r.at[3 * nw + p], me).wait_recv()
        for w in range(nw):
            for p in range(3):
                slot = full[w].at[jps[p], theirs]
                _remote(slot, slot, d2d_s.at[3 * w + p], d2d_r.at[3 * w + p], me).wait_recv()
        for cp in sends + passed:
            cp.wait_send()
        for cp in local:
            cp.wait()

    outs = [SDS((4,) + s.shape, s.dtype) for s in shards] + [SDS((4,) + small.shape, small.dtype)]
    res = _comm_call(body, "all_gather_weights", list(shards) + [small], outs,
                     [3 * nw + 3, 3 * nw + 3, 3 * nw, 3 * nw, nw + 1])
    return res[:nw], res[nw]


def _all_gather_small(part):
    def body(p_ref, o_ref, send_s, recv_s, loc):
        x, y, c, _ = _place()
        me_id = 4 * x + 2 * y + c
        own = pltpu.make_async_copy(p_ref, o_ref.at[me_id], loc.at[0])
        own.start()
        sends = []
        for k in range(1, 8):
            peer = (x ^ (k >> 2), y ^ ((k >> 1) & 1), c ^ (k & 1))
            sends.append(_remote(p_ref, o_ref.at[me_id], send_s.at[k - 1], recv_s.at[k - 1], peer))
        for cp in sends:
            cp.start()
        for k in range(1, 8):
            peer_id = 4 * (x ^ (k >> 2)) + 2 * (y ^ ((k >> 1) & 1)) + (c ^ (k & 1))
            slot = o_ref.at[peer_id]
            _remote(slot, slot, send_s.at[k - 1], recv_s.at[k - 1], (x, y, c)).wait_recv()
        for cp in sends:
            cp.wait_send()
        own.wait()

    return _comm_call(body, "all_gather_small", [part], [SDS((8,) + part.shape, part.dtype)], [7, 7, 1])[0]


def _rs_to_sibling(dws):
    nw = len(dws)

    def body(*refs):
        src, land = refs[:nw], refs[nw:2 * nw]
        send_s, recv_s = refs[2 * nw:]
        x, y, c, _ = _place()
        cps = [_remote(src[w].at[:, :, 1 - c], land[w], send_s.at[w], recv_s.at[w], (x, y, 1 - c))
               for w in range(nw)]
        for cp in cps:
            cp.start()
        for cp in cps:
            cp.wait_recv()
        for cp in cps:
            cp.wait_send()

    outs = [SDS((a.shape[0], 4, a.shape[3], a.shape[4]), a.dtype) for a in dws]
    return _comm_call(body, "rs_to_sibling", list(dws), outs, [nw, nw])


def _rs_to_chips(s1s):
    nw = len(s1s)

    def body(*refs):
        src, land = refs[:nw], refs[nw:2 * nw]
        send_s, recv_s = refs[2 * nw:]
        x, y, c, chips = _place()
        cps = []
        for w in range(nw):
            for p in range(3):
                jp = 2 * chips[p][0] + chips[p][1]
                cps.append(_remote(src[w].at[:, jp], land[w].at[p], send_s.at[3 * w + p], recv_s.at[3 * w + p],
                                   (*chips[p], c)))
        for cp in cps:
            cp.start()
        for cp in cps:
            cp.wait_recv()
        for cp in cps:
            cp.wait_send()

    outs = [SDS((3, a.shape[0], a.shape[2], a.shape[3]), a.dtype) for a in s1s]
    return _comm_call(body, "rs_to_chips", list(s1s), outs, [3 * nw, 3 * nw])


def _rs_join_halves(halves):
    nw = len(halves)

    def body(*refs):
        src, full = refs[:nw], refs[nw:2 * nw]
        send_s, recv_s, loc = refs[2 * nw:]
        x, y, c, _ = _place()
        own = [pltpu.make_async_copy(src[w], full[w].at[:, c], loc.at[w]) for w in range(nw)]
        cps = [_remote(src[w], full[w].at[:, c], send_s.at[w], recv_s.at[w], (x, y, 1 - c)) for w in range(nw)]
        for cp in own + cps:
            cp.start()
        for w in range(nw):
            slot = full[w].at[:, 1 - c]
            _remote(slot, slot, send_s.at[w], recv_s.at[w], (x, y, c)).wait_recv()
        for cp in cps:
            cp.wait_send()
        for cp in own:
            cp.wait()

    outs = [SDS((a.shape[0], 2, a.shape[1], a.shape[2]), a.dtype) for a in halves]
    return _comm_call(body, "rs_join_halves", list(halves), outs, [nw, nw, nw])


BIG = ("w_in", "w_attn_proj", "w_conv_out", "w_o", "w_ffn_in", "w_ffn_down")
COL_SHARDED = {"w_in": True, "w_attn_proj": True, "w_conv_out": False, "w_o": False,
               "w_ffn_in": True, "w_ffn_down": False}
SMALL = ("b_ada", "g_norm1", "g_q", "g_k", "w_conv_dw", "b_conv_dw", "g_conv_ln", "b_conv_ln",
         "g_norm2", "w_ffn_dw", "b_ffn_dw")
WEIGHTS = ("w_ada", "b_ada", "g_norm1", "w_in", "g_q", "g_k", "w_attn_proj", "w_conv_dw", "b_conv_dw",
           "g_conv_ln", "b_conv_ln", "w_conv_out", "w_o", "g_norm2", "w_ffn_in", "w_ffn_dw", "b_ffn_dw",
           "w_ffn_down")


def _pack_rows(arrs):
    flat = jnp.concatenate([a.reshape(-1) for a in arrs])
    pad = (-flat.shape[0]) % (8 * 128)
    return jnp.pad(flat, (0, pad)).reshape(-1, 128)


def _unpack_rows(packed, shapes):
    flat = packed.reshape(-1)
    out, off = [], 0
    for shp in shapes:
        n = 1
        for s in shp:
            n *= s
        out.append(flat[off:off + n].reshape(shp))
        off += n
    return out


def _rope_tables(positions):
    half = ROT_DIM // 2
    inv_freq = ROPE_THETA ** (-jnp.arange(0, ROT_DIM, 2, dtype=F32) / ROT_DIM)
    ang = positions.astype(F32)[..., None] * inv_freq
    cos, sin = jnp.cos(ang), jnp.sin(ang)
    S = cos.shape[0]
    cos_t = jnp.concatenate([cos, cos, jnp.ones((S, HEAD_DIM - ROT_DIM), F32)], axis=1)
    sa_t = jnp.concatenate([-sin, jnp.zeros((S, HEAD_DIM - half), F32)], axis=1)
    sb_t = jnp.concatenate([jnp.zeros((S, half), F32), sin, jnp.zeros((S, HEAD_DIM - ROT_DIM), F32)], axis=1)
    return cos_t, sa_t, sb_t


def _local_step(x, c, positions, loss_target, wf, small_w):
    S, D = x.shape
    L = wf["w_in"].shape[0]
    tabs = _rope_tables(positions)
    mod, cact = _mod_fwd(c, wf["w_ada"], small_w["b_ada"])
    vec = lambda a: a.reshape(1, -1)
    mods = [[vec(mod[l, k * D:(k + 1) * D]) for k in range(6)] for l in range(L)]

    saved = []
    xs = x
    for l in range(L):
        sh1, sc1, gt1, sh2, sc2, gt2 = mods[l]
        g1, g2 = vec(small_w["g_norm1"][l]), vec(small_w["g_norm2"][l])
        gq, gk = vec(small_w["g_q"][l]), vec(small_w["g_k"][l])
        z, h = _normmod_mm(xs, g1, sc1, sh1, wf["w_in"], l, "in_proj")
        qn, kn = _qk_prep(z, gq, gk, tabs)
        outs, lses = [], []
        for gi in range(len(DILATIONS)):
            o_g, lse_g = _attn_fwd(qn, kn, z, gi)
            outs.append(o_g)
            lses.append(lse_g)
        attn = _attn_combine(outs, lses)
        y_a = _mm_nn(attn, wf["w_attn_proj"], l, "attn_proj")
        u1 = _glu_conv(z, small_w["w_conv_dw"], small_w["b_conv_dw"], l, D)
        u3 = _ln_silu(u1, vec(small_w["g_conv_ln"][l]), vec(small_w["b_conv_ln"][l]))
        y_b = _mm_nn(u3, wf["w_conv_out"], l, "conv_out")
        merged = _merge(z, y_a, y_b, D)
        t_o, x2 = _mm_nn(merged, wf["w_o"], l, "out_proj", res=xs, gate=gt1)
        gu, h2 = _normmod_mm(x2, g2, sc2, sh2, wf["w_ffn_in"], l, "ffn_in")
        a = _ffn_act(gu, small_w["w_ffn_dw"], small_w["b_ffn_dw"], l)
        t_f, x3 = _mm_nn(a, wf["w_ffn_down"], l, "ffn_down", res=x2, gate=gt2)
        saved.append(dict(x=xs, z=z, h=h, qn=qn, kn=kn, outs=outs, lses=lses, attn=attn, y_a=y_a, u1=u1, u3=u3,
                          y_b=y_b, merged=merged, t_o=t_o, x2=x2, gu=gu, h2=h2, a=a, t_f=t_f))
        xs = x3

    dx, loss = _loss_grad(xs, loss_target)

    bufs = {n: None for n in BIG}
    small_g = {n: [None] * L for n in SMALL}
    dmods = [None] * L

    def wgrad(name, a_mat, b_mat, l):
        bufs[name] = _mm_tn(a_mat, b_mat, bufs[name], l, L, COL_SHARDED[name], "grad_" + name)

    for l in reversed(range(L)):
        sv = saved[l]
        sh1, sc1, gt1, sh2, sc2, gt2 = mods[l]
        g1, g2 = vec(small_w["g_norm1"][l]), vec(small_w["g_norm2"][l])
        gq, gk = vec(small_w["g_q"][l]), vec(small_w["g_k"][l])
        e2, dgt2 = _gate_bwd(dx, sv["t_f"], gt2, "gate_bwd_ffn")
        wgrad("w_ffn_down", sv["a"], e2, l)
        da = _mm_nt(e2, wf["w_ffn_down"], l, "ffn_down_bwd")
        dgp, dgu = _ffn_act_bwd1(da, sv["gu"], small_w["w_ffn_dw"], small_w["b_ffn_dw"], l)
        dgu, ffn_sums = _ffn_act_bwd2(dgp, dgu, sv["gu"], small_w["w_ffn_dw"], l)
        wgrad("w_ffn_in", sv["h2"], dgu, l)
        dh2 = _mm_nt(dgu, wf["w_ffn_in"], l, "ffn_in_bwd")
        dx2, n2_sums = _norm_bwd(dh2, sv["x2"], dx, g2, sc2, "norm2_bwd")
        e1, dgt1 = _gate_bwd(dx2, sv["t_o"], gt1, "gate_bwd_mix")
        wgrad("w_o", sv["merged"], e1, l)
        dmerged = _mm_nt(e1, wf["w_o"], l, "out_proj_bwd")
        dya, dyb, dga, dgb = _merge_bwd(dmerged, sv["z"], sv["y_a"], sv["y_b"], D)
        wgrad("w_attn_proj", sv["attn"], dya, l)
        wgrad("w_conv_out", sv["u3"], dyb, l)
        dattn = _mm_nt(dya, wf["w_attn_proj"], l, "attn_proj_bwd")
        du3 = _mm_nt(dyb, wf["w_conv_out"], l, "conv_out_bwd")
        du1, ln_sums = _ln_silu_bwd(du3, sv["u1"], vec(small_w["g_conv_ln"][l]), vec(small_w["b_conv_ln"][l]))
        dcv, dcg, conv_sums = _glu_conv_bwd(du1, sv["z"], small_w["w_conv_dw"], l, D)
        dos, dterms = _attn_bwd_prep(dattn, sv["outs"], sv["lses"])
        dqs, dks, dvs = [], [], []
        for gi in range(len(DILATIONS)):
            dq_g, dk_g, dv_g = _attn_bwd(sv["qn"], sv["kn"], sv["z"], dos[gi], sv["lses"][gi], dterms[gi], gi)
            dqs.append(dq_g)
            dks.append(dk_g)
            dvs.append(dv_g)
        dq, dk, qk_sums = _qk_prep_bwd(dqs, dks, sv["z"], gq, gk, tabs)
        dz = jnp.concatenate([dq, dk] + dvs + [dcv, dcg, dga, dgb], axis=1)
        wgrad("w_in", sv["h"], dz, l)
        dh = _mm_nt(dz, wf["w_in"], l, "in_proj_bwd")
        dx, n1_sums = _norm_bwd(dh, sv["x"], dx2, g1, sc1, "norm1_bwd")

        dmods[l] = jnp.concatenate([n1_sums[0], n1_sums[1], dgt1[0], n2_sums[0], n2_sums[1], dgt2[0]])
        small_g["b_ada"][l] = dmods[l]
        small_g["g_norm1"][l] = n1_sums[2]
        small_g["g_q"][l] = qk_sums[0]
        small_g["g_k"][l] = qk_sums[1]
        small_g["w_conv_dw"][l] = conv_sums[:CONV_K]
        small_g["b_conv_dw"][l] = conv_sums[CONV_K]
        small_g["g_conv_ln"][l] = ln_sums[0]
        small_g["b_conv_ln"][l] = ln_sums[1]
        small_g["g_norm2"][l] = n2_sums[2]
        small_g["w_ffn_dw"][l] = ffn_sums[:FFN_K]
        small_g["b_ffn_dw"][l] = ffn_sums[FFN_K]

    small_g = {n: jnp.stack(v) for n, v in small_g.items()}
    return loss, dx, bufs, small_g, cact


def kernel(x, c, positions, w_ada, b_ada, g_norm1, w_in, g_q, g_k, w_attn_proj, w_conv_dw, b_conv_dw, g_conv_ln, b_conv_ln, w_conv_out, w_o, g_norm2, w_ffn_in, w_ffn_dw, b_ffn_dw, w_ffn_down, loss_target, m_w_ada, m_b_ada, m_g_norm1, m_w_in, m_g_q, m_g_k, m_w_attn_proj, m_w_conv_dw, m_b_conv_dw, m_g_conv_ln, m_b_conv_ln, m_w_conv_out, m_w_o, m_g_norm2, m_w_ffn_in, m_w_ffn_dw, m_b_ffn_dw, m_w_ffn_down, v_w_ada, v_b_ada, v_g_norm1, v_w_in, v_g_q, v_g_k, v_w_attn_proj, v_w_conv_dw, v_b_conv_dw, v_g_conv_ln, v_b_conv_ln, v_w_conv_out, v_w_o, v_g_norm2, v_w_ffn_in, v_w_ffn_dw, v_b_ffn_dw, v_w_ffn_down):
    w = dict(w_ada=w_ada, b_ada=b_ada, g_norm1=g_norm1, w_in=w_in, g_q=g_q, g_k=g_k, w_attn_proj=w_attn_proj,
             w_conv_dw=w_conv_dw, b_conv_dw=b_conv_dw, g_conv_ln=g_conv_ln, b_conv_ln=b_conv_ln,
             w_conv_out=w_conv_out, w_o=w_o, g_norm2=g_norm2, w_ffn_in=w_ffn_in, w_ffn_dw=w_ffn_dw,
             b_ffn_dw=b_ffn_dw, w_ffn_down=w_ffn_down)
    m = dict(w_ada=m_w_ada, b_ada=m_b_ada, g_norm1=m_g_norm1, w_in=m_w_in, g_q=m_g_q, g_k=m_g_k,
             w_attn_proj=m_w_attn_proj, w_conv_dw=m_w_conv_dw, b_conv_dw=m_b_conv_dw, g_conv_ln=m_g_conv_ln,
             b_conv_ln=m_b_conv_ln, w_conv_out=m_w_conv_out, w_o=m_w_o, g_norm2=m_g_norm2, w_ffn_in=m_w_ffn_in,
             w_ffn_dw=m_w_ffn_dw, b_ffn_dw=m_b_ffn_dw, w_ffn_down=m_w_ffn_down)
    v = dict(w_ada=v_w_ada, b_ada=v_b_ada, g_norm1=v_g_norm1, w_in=v_w_in, g_q=v_g_q, g_k=v_g_k,
             w_attn_proj=v_w_attn_proj, w_conv_dw=v_w_conv_dw, b_conv_dw=v_b_conv_dw, g_conv_ln=v_g_conv_ln,
             b_conv_ln=v_b_conv_ln, w_conv_out=v_w_conv_out, w_o=v_w_o, g_norm2=v_g_norm2, w_ffn_in=v_w_ffn_in,
             w_ffn_dw=v_w_ffn_dw, b_ffn_dw=v_b_ffn_dw, w_ffn_down=v_w_ffn_down)
    xi, yi, ci = lax.axis_index("x"), lax.axis_index("y"), lax.axis_index("c")
    j = 2 * xi + yi
    L = w_ada.shape[0]
    D = x.shape[-1]

    mats = ("w_ada",) + BIG
    col = dict(COL_SHARDED, w_ada=True)
    shards = [w[n].astype(BF16) for n in mats]
    conv_shapes = [w["w_conv_dw"].shape, w["w_ffn_dw"].shape]
    gathered, small_g4 = _all_gather_weights(shards, _pack_rows([w["w_conv_dw"], w["w_ffn_dw"]]))
    wf = {}
    for n, g4 in zip(mats, gathered):
        _, _, r, cols = g4.shape
        if col[n]:
            wf[n] = g4.transpose(1, 2, 0, 3).reshape(L, r, 4 * cols)
        else:
            wf[n] = g4.transpose(1, 0, 2, 3).reshape(L, 4 * r, cols)
    conv_parts = [_unpack_rows(small_g4[k], conv_shapes) for k in range(4)]
    small_w = {n: w[n] for n in SMALL}
    small_w["w_conv_dw"] = jnp.concatenate([p[0] for p in conv_parts], axis=2)
    small_w["w_ffn_dw"] = jnp.concatenate([p[1] for p in conv_parts], axis=2)

    loss, dx, bufs, small_g, cact = _local_step(x[0], c, positions[0], loss_target[0], wf, small_w)
    loss = lax.psum(loss, ("x", "y", "c"))

    dws = [bufs[n] for n in BIG]
    land1 = _rs_to_sibling(dws)
    c_idx = jnp.reshape(ci, (1,)).astype(jnp.int32)
    j_idx = jnp.reshape(j, (1,)).astype(jnp.int32)
    s1s = [_add_halves(a, b, c_idx, "rs_add_halves") for a, b in zip(dws, land1)]
    land2 = _rs_to_chips(s1s)
    halves = [_add_chips(a, b, j_idx, "rs_add_chips") for a, b in zip(s1s, land2)]
    joined = _rs_join_halves(halves)
    grads = {n: g.reshape(w[n].shape) for n, g in zip(BIG, joined)}

    part = _pack_rows([small_g[n] for n in SMALL] + [cact])
    parts = _all_gather_small(part)
    full_shapes = [small_g[n].shape for n in SMALL]
    summed = _unpack_rows(_sum_parts(parts), full_shapes)
    for n, g in zip(SMALL, summed):
        if n in ("w_conv_dw", "w_ffn_dw"):
            cols = w[n].shape[2]
            g = lax.dynamic_slice_in_dim(g, j * cols, cols, axis=2)
        grads[n] = g
    flat = parts.reshape(8, -1)
    off, dmod_all = 0, None
    for n, shp in zip(SMALL, full_shapes):
        size = 1
        for s in shp:
            size *= s
        if n == "b_ada":
            dmod_all = flat[:, off:off + size].reshape(8, L, 6 * D)
        off += size
    cact_all = flat[:, off:off + D]
    cols = w_ada.shape[2]
    dmod_cols = lax.dynamic_slice_in_dim(dmod_all, j * cols, cols, axis=2).transpose(1, 0, 2)
    grads["w_ada"] = _wada_grad(cact_all.T, dmod_cols)

    delta, new_m, new_v = {}, {}, {}
    for n in ("w_ada",) + BIG:
        shp = w[n].shape
        two_d = lambda a: a.reshape(shp[0] * shp[1], shp[2])
        dl, mn, vn = _adamw(two_d(w[n]), two_d(grads[n]), two_d(m[n]), two_d(v[n]), "adamw_" + n)
        delta[n], new_m[n], new_v[n] = dl.reshape(shp), mn.reshape(shp), vn.reshape(shp)
    small_shapes = [w[n].shape for n in SMALL]
    packs = [_pack_rows([d[n] for n in SMALL]) for d in (w, grads, m, v)]
    outs = _adamw(*packs, "adamw_small")
    for d, packed in zip((delta, new_m, new_v), outs):
        for n, a in zip(SMALL, _unpack_rows(packed, small_shapes)):
            d[n] = a

    return (loss, dx[None], *[grads[n] for n in WEIGHTS], *[delta[n] for n in WEIGHTS],
            *[new_m[n] for n in WEIGHTS], *[new_v[n] for n in WEIGHTS])
```

```python
import functools

import jax
import jax.numpy as jnp
from jax import lax
from jax.experimental import pallas as pl
from jax.experimental.pallas import tpu as pltpu

F32 = jnp.float32
BF16 = jnp.bfloat16
SDS = jax.ShapeDtypeStruct
MESH = pl.DeviceIdType.MESH

HEAD_DIM = 128
BLOCK = 128
HEADS_PER_GROUP = 4
GROUP_W = HEADS_PER_GROUP * HEAD_DIM
DILATIONS = (1, 4, 16)
ATTN_W = len(DILATIONS) * GROUP_W
ROT_DIM = HEAD_DIM // 4
ROPE_THETA = 500000.0
CONV_K = 31
CONV_HALO = 32
FFN_K = 3
FFN_HALO = 8
EPS = 1e-6
NEG = -1e30
CHUNK = 512

ADAM_LR = 0.001
ADAM_B1 = 0.9
ADAM_B2 = 0.999
ADAM_EPS = 1e-08
ADAM_WD = 0.01
ADAM_STEP = 10

VMEM_LIMIT_BYTES = 48 * 1024 * 1024
TILE_BYTES = 2 * 1024 * 1024

NT_DIMS = (((1,), (1,)), ((), ()))
TN_DIMS = (((0,), (0,)), ((), ()))


def _cp(*sem):
    return pltpu.CompilerParams(dimension_semantics=sem if sem else None,
                                vmem_limit_bytes=VMEM_LIMIT_BYTES)


def _tm(s, cap=512):
    return min(cap, s)


def _row_tile(rows, cols, itemsize=4, step=8):
    best = None
    for t in range(step, rows + 1, step):
        if rows % t == 0 and t * cols * itemsize <= TILE_BYTES:
            best = t
    return best if best is not None else rows


def _sigmoid(v):
    return jax.nn.sigmoid(v)


def _mod_fwd(c, w_ada, b_ada):
    L, D, N6 = w_ada.shape
    tn = N6 // 4

    def body(c_ref, w_ref, b_ref, mod_ref, cact_ref):
        cv = c_ref[...]
        ca = cv * _sigmoid(cv)
        cact_ref[...] = ca
        a8 = jnp.broadcast_to(ca, (8, D)).astype(BF16)
        acc = jnp.dot(a8, w_ref[0], preferred_element_type=F32)
        mod_ref[0] = acc[0:1, :] + b_ref[0]

    mod, cact = pl.pallas_call(
        body, name="mod_fwd", grid=(L, 4),
        in_specs=[pl.BlockSpec((1, D), lambda l, j: (0, 0)),
                  pl.BlockSpec((1, D, tn), lambda l, j: (l, 0, j)),
                  pl.BlockSpec((1, 1, tn), lambda l, j: (l, 0, j))],
        out_specs=[pl.BlockSpec((1, 1, tn), lambda l, j: (l, 0, j)),
                   pl.BlockSpec((1, D), lambda l, j: (0, 0))],
        out_shape=[SDS((L, 1, N6), F32), SDS((1, D), F32)],
        compiler_params=_cp("arbitrary", "arbitrary"),
    )(c, w_ada, b_ada.reshape(L, 1, N6))
    return mod.reshape(L, N6), cact


def _normmod_mm(x, g, sc, sh, w, l, name):
    S, D = x.shape
    N = w.shape[2]
    tm, tn = _tm(S, 1024), CHUNK

    def body(x_ref, g_ref, sc_ref, sh_ref, w_ref, z_ref, h_ref, hs):
        @pl.when(pl.program_id(1) == 0)
        def _():
            xv = x_ref[...]
            r = lax.rsqrt(jnp.mean(xv * xv, axis=-1, keepdims=True) + EPS)
            hv = (xv * r) * g_ref[...] * (1.0 + sc_ref[...]) + sh_ref[...]
            hs[...] = hv.astype(BF16)
            h_ref[...] = hs[...]
        z_ref[...] = jnp.dot(hs[...], w_ref[0], preferred_element_type=F32)

    vec = pl.BlockSpec((1, D), lambda i, j: (0, 0))
    return pl.pallas_call(
        body, name=name, grid=(S // tm, N // tn),
        in_specs=[pl.BlockSpec((tm, D), lambda i, j: (i, 0)), vec, vec, vec,
                  pl.BlockSpec((1, D, tn), lambda i, j: (l, 0, j))],
        out_specs=[pl.BlockSpec((tm, tn), lambda i, j: (i, j)),
                   pl.BlockSpec((tm, D), lambda i, j: (i, 0))],
        out_shape=[SDS((S, N), F32), SDS((S, D), BF16)],
        scratch_shapes=[pltpu.VMEM((tm, D), BF16)],
        compiler_params=_cp("parallel", "arbitrary"),
    )(x, g, sc, sh, w)


def _rope(t, cos_t, sa_t, sb_t):
    return t * cos_t + pltpu.roll(t, HEAD_DIM - ROT_DIM // 2, 1) * sa_t + pltpu.roll(t, ROT_DIM // 2, 1) * sb_t


def _unrope(d, cos_t, sa_t, sb_t):
    return d * cos_t + pltpu.roll(d * sa_t, ROT_DIM // 2, 1) + pltpu.roll(d * sb_t, HEAD_DIM - ROT_DIM // 2, 1)


def _dil_spec(d, tm):
    return pl.BlockSpec((d, tm // d, GROUP_W), lambda i: (0, i, 0))


def _to_dilated(scr, val, dst_ref, d, cols):
    if d == 1:
        dst_ref[0, :, cols] = val.astype(dst_ref.dtype)
        return
    n = val.shape[0] // d
    scr[...] = val
    for r in range(d):
        dst_ref[r, :, cols] = scr[pl.ds(r, n, stride=d), :].astype(dst_ref.dtype)


def _from_dilated(scr, src_ref, d, cols):
    if d == 1:
        return src_ref[0, :, cols].astype(F32)
    n = src_ref.shape[1]
    for r in range(d):
        scr[pl.ds(r, n, stride=d), :] = src_ref[r, :, cols].astype(F32)
    return scr[...]


def _qk_prep(z, g_q, g_k, tabs):
    S = z.shape[0]
    tm = _tm(S)
    n_heads = ATTN_W // HEAD_DIM
    ng = len(DILATIONS)

    def body(q_ref, k_ref, v_ref, gq_ref, gk_ref, c_ref, sa_ref, sb_ref, *rest):
        outs, scr = rest[:3 * ng], rest[3 * ng]
        cos_t, sa_t, sb_t = c_ref[...], sa_ref[...], sb_ref[...]
        for wi, (src, gref) in enumerate(((q_ref, gq_ref), (k_ref, gk_ref), (v_ref, None))):
            for h in range(n_heads):
                gi = h // HEADS_PER_GROUP
                cols = slice((h % HEADS_PER_GROUP) * HEAD_DIM, (h % HEADS_PER_GROUP + 1) * HEAD_DIM)
                t = src[:, h * HEAD_DIM:(h + 1) * HEAD_DIM]
                if gref is not None:
                    r = lax.rsqrt(jnp.mean(t * t, axis=-1, keepdims=True) + EPS)
                    t = _rope((t * r) * gref[...], cos_t, sa_t, sb_t)
                _to_dilated(scr, t, outs[wi * ng + gi], DILATIONS[gi], cols)

    tab = pl.BlockSpec((tm, HEAD_DIM), lambda i: (i, 0))
    gsp = pl.BlockSpec((1, HEAD_DIM), lambda i: (0, 0))
    wide = lambda cb: pl.BlockSpec((tm, ATTN_W), lambda i: (i, cb))
    out = pl.pallas_call(
        body, name="qk_prep", grid=(S // tm,),
        in_specs=[wide(0), wide(1), wide(2), gsp, gsp, tab, tab, tab],
        out_specs=[_dil_spec(d, tm) for _ in range(3) for d in DILATIONS],
        out_shape=[SDS((d, S // d, GROUP_W), BF16) for _ in range(3) for d in DILATIONS],
        scratch_shapes=[pltpu.VMEM((tm, HEAD_DIM), F32)],
        compiler_params=_cp("parallel"),
    )(z, z, z, g_q, g_k, *tabs)
    return out[:ng], out[ng:2 * ng], out[2 * ng:]


def _attn_blk(nb, shift):
    if shift < 0:
        return pl.BlockSpec((None, BLOCK, GROUP_W), lambda r, n: (r, jnp.maximum(n - 1, 0), 0))
    if shift > 0:
        return pl.BlockSpec((None, BLOCK, GROUP_W), lambda r, n: (r, jnp.minimum(n + 1, nb - 1), 0))
    return pl.BlockSpec((None, BLOCK, GROUP_W), lambda r, n: (r, n, 0))


def _attn_fwd(q, k, v, gi):
    d, Sd, _ = q.shape
    nb = Sd // BLOCK
    scale = HEAD_DIM ** -0.5

    def body(q_ref, kc_ref, kp_ref, vc_ref, vp_ref, o_ref, lse_ref):
        n = pl.program_id(1)
        qi = lax.broadcasted_iota(jnp.int32, (BLOCK, BLOCK), 0)
        kj = lax.broadcasted_iota(jnp.int32, (BLOCK, BLOCK), 1)
        mask_c = kj <= qi
        mask_p = jnp.logical_and(kj >= qi, n > 0)
        for h in range(HEADS_PER_GROUP):
            sl = slice(h * HEAD_DIM, (h + 1) * HEAD_DIM)
            qh = q_ref[:, sl]
            s_c = lax.dot_general(qh, kc_ref[:, sl], NT_DIMS, preferred_element_type=F32) * scale
            s_p = lax.dot_general(qh, kp_ref[:, sl], NT_DIMS, preferred_element_type=F32) * scale
            s_c = jnp.where(mask_c, s_c, NEG)
            s_p = jnp.where(mask_p, s_p, NEG)
            m = jnp.maximum(jnp.max(s_c, axis=-1, keepdims=True), jnp.max(s_p, axis=-1, keepdims=True))
            p_c = jnp.exp(s_c - m)
            p_p = jnp.exp(s_p - m)
            den = jnp.sum(p_c, axis=-1, keepdims=True) + jnp.sum(p_p, axis=-1, keepdims=True)
            acc = jnp.dot(p_c.astype(BF16), vc_ref[:, sl], preferred_element_type=F32)
            acc = acc + jnp.dot(p_p.astype(BF16), vp_ref[:, sl], preferred_element_type=F32)
            o_ref[:, sl] = acc / den
            lse_ref[:, sl] = jnp.broadcast_to(m + jnp.log(den), (BLOCK, HEAD_DIM))

    cur, prev = _attn_blk(nb, 0), _attn_blk(nb, -1)
    return pl.pallas_call(
        body, name=f"attn_fwd_g{gi}", grid=(d, nb),
        in_specs=[cur, cur, prev, cur, prev], out_specs=[cur] * 2,
        out_shape=[SDS((d, Sd, GROUP_W), F32)] * 2,
        compiler_params=_cp("parallel", "parallel"),
    )(q, k, k, v, v)


def _group_weights(lses):
    m = jnp.maximum(jnp.maximum(lses[0], lses[1]), lses[2])
    es = [jnp.exp(v - m) for v in lses]
    inv = 1.0 / (es[0] + es[1] + es[2])
    return [e * inv for e in es]


def _attn_combine(outs, lses):
    S = outs[0].shape[0] * outs[0].shape[1]
    tm = _tm(S)
    ng = len(DILATIONS)

    def body(*refs):
        o_refs, l_refs, a_ref, scr = refs[:ng], refs[ng:2 * ng], refs[2 * ng], refs[2 * ng + 1:]
        for h in range(HEADS_PER_GROUP):
            cols = slice(h * HEAD_DIM, (h + 1) * HEAD_DIM)
            ov = [_from_dilated(scr[g], o_refs[g], DILATIONS[g], cols) for g in range(ng)]
            lv = [_from_dilated(scr[ng + g], l_refs[g], DILATIONS[g], cols) for g in range(ng)]
            w = _group_weights(lv)
            a_ref[:, cols] = (w[0] * ov[0] + w[1] * ov[1] + w[2] * ov[2]).astype(BF16)

    dil = [_dil_spec(d, tm) for d in DILATIONS]
    return pl.pallas_call(
        body, name="attn_combine", grid=(S // tm,), in_specs=dil + dil,
        out_specs=pl.BlockSpec((tm, GROUP_W), lambda i: (i, 0)),
        out_shape=SDS((S, GROUP_W), BF16),
        scratch_shapes=[pltpu.VMEM((tm, HEAD_DIM), F32)] * (2 * ng),
        compiler_params=_cp("parallel"),
    )(*outs, *lses)


def _mm_nn(a, w, l, name, res=None, gate=None):
    S, K = a.shape
    N = w.shape[2]
    tm = _tm(S)
    gated = res is not None

    def body(*refs):
        if gated:
            a_ref, w_ref, r_ref, g_ref, t_ref, o_ref = refs
        else:
            a_ref, w_ref, t_ref = refs
        t = jnp.dot(a_ref[...], w_ref[0], preferred_element_type=F32)
        t_ref[...] = t
        if gated:
            o_ref[...] = r_ref[...] + g_ref[...] * t

    row = pl.BlockSpec((tm, N), lambda i: (i, 0))
    in_specs = [pl.BlockSpec((tm, K), lambda i: (i, 0)), pl.BlockSpec((1, K, N), lambda i: (l, 0, 0))]
    args = [a, w]
    if gated:
        in_specs += [row, pl.BlockSpec((1, N), lambda i: (0, 0))]
        args += [res, gate]
    n_out = 2 if gated else 1
    out = pl.pallas_call(
        body, name=name, grid=(S // tm,), in_specs=in_specs, out_specs=[row] * n_out,
        out_shape=[SDS((S, N), F32)] * n_out, compiler_params=_cp("parallel"),
    )(*args)
    return out if gated else out[0]


def _glu(cv, cg):
    return cv * _sigmoid(cg)


STRIP = 16
SHIFTS = 8


def _fill_shifted(sh, ext, rows):
    sh[0] = ext[...]
    for b in range(1, SHIFTS):
        sh[b, 0:rows, :] = ext[pl.ds(b, rows), :]


def _tap(sh, base, off, rows=STRIP):
    return sh[off % SHIFTS, pl.ds(base + SHIFTS * (off // SHIFTS), rows), :]


def _glu_conv(z, w_dw, b_dw, l, D):
    S = z.shape[0]
    tm = _tm(S)
    cc0 = 3 * ATTN_W // CHUNK
    ncc = D // CHUNK
    hb = tm // CONV_HALO
    lead = CONV_HALO - (CONV_K - 1)

    def body(cv_ref, cg_ref, hv_ref, hg_ref, w_ref, b_ref, o_ref, ext, sh):
        i = pl.program_id(1)
        ext[0:CONV_HALO, :] = jnp.where(i > 0, _glu(hv_ref[...], hg_ref[...]), 0.0)
        ext[CONV_HALO:, :] = _glu(cv_ref[...], cg_ref[...])
        _fill_shifted(sh, ext, tm + CONV_HALO - SHIFTS)

        def strip(s, carry):
            base = pl.multiple_of(s * STRIP, STRIP)
            acc = jnp.broadcast_to(b_ref[...], (STRIP, CHUNK))
            for k in range(CONV_K):
                acc = acc + w_ref[0, k:k + 1, :] * _tap(sh, base, lead + k)
            o_ref[pl.ds(base, STRIP), :] = acc
            return carry

        lax.fori_loop(0, tm // STRIP, strip, 0)

    cur = lambda off: pl.BlockSpec((tm, CHUNK), lambda cc, i: (i, cc0 + off + cc))
    halo = lambda off: pl.BlockSpec((CONV_HALO, CHUNK), lambda cc, i: (jnp.maximum(i * hb - 1, 0), cc0 + off + cc))
    return pl.pallas_call(
        body, name="glu_conv", grid=(ncc, S // tm),
        in_specs=[cur(0), cur(ncc), halo(0), halo(ncc),
                  pl.BlockSpec((1, CONV_K, CHUNK), lambda cc, i: (l, 0, cc)),
                  pl.BlockSpec((None, 1, CHUNK), lambda cc, i: (l, 0, cc))],
        out_specs=pl.BlockSpec((tm, CHUNK), lambda cc, i: (i, cc)),
        out_shape=SDS((S, D), F32),
        scratch_shapes=[pltpu.VMEM((tm + CONV_HALO, CHUNK), F32),
                        pltpu.VMEM((SHIFTS, tm + CONV_HALO, CHUNK), F32)],
        compiler_params=_cp("parallel", "parallel"),
    )(z, z, z, z, w_dw, b_dw[:, None, :])


def _ln_stats(u):
    mu = jnp.mean(u, axis=-1, keepdims=True)
    xc = u - mu
    rstd = lax.rsqrt(jnp.mean(xc * xc, axis=-1, keepdims=True) + EPS)
    return xc * rstd, rstd


def _ln_silu(u1, g, b):
    S, D = u1.shape
    tm = _tm(S)

    def body(u_ref, g_ref, b_ref, o_ref):
        xh, _ = _ln_stats(u_ref[...])
        yv = xh * g_ref[...] + b_ref[...]
        o_ref[...] = (yv * _sigmoid(yv)).astype(BF16)

    vec = pl.BlockSpec((1, D), lambda i: (0, 0))
    return pl.pallas_call(
        body, name="ln_silu", grid=(S // tm,),
        in_specs=[pl.BlockSpec((tm, D), lambda i: (i, 0)), vec, vec],
        out_specs=pl.BlockSpec((tm, D), lambda i: (i, 0)),
        out_shape=SDS((S, D), BF16), compiler_params=_cp("parallel"),
    )(u1, g, b)


def _merge(z, y_a, y_b, D):
    S = z.shape[0]
    tm = _tm(S)
    ncc = D // CHUNK
    ga0 = (3 * ATTN_W + 2 * D) // CHUNK

    def body(ga_ref, gb_ref, ya_ref, yb_ref, o_ref):
        o_ref[...] = (_sigmoid(ga_ref[...]) * ya_ref[...] + _sigmoid(gb_ref[...]) * yb_ref[...]).astype(BF16)

    ysp = pl.BlockSpec((tm, CHUNK), lambda i, cc: (i, cc))
    return pl.pallas_call(
        body, name="merge", grid=(S // tm, ncc),
        in_specs=[pl.BlockSpec((tm, CHUNK), lambda i, cc: (i, ga0 + cc)),
                  pl.BlockSpec((tm, CHUNK), lambda i, cc: (i, ga0 + ncc + cc)), ysp, ysp],
        out_specs=ysp, out_shape=SDS((S, D), BF16), compiler_params=_cp("parallel", "parallel"),
    )(z, z, y_a, y_b)


def _ffn_chunk(F):
    best = 128
    for t in range(128, min(F, 1536) + 1, 128):
        if F % t == 0:
            best = t
    return best


def _ffn_conv(ext, w_ref, b_ref, tm):
    gp = jnp.broadcast_to(b_ref[...], (tm, ext.shape[1]))
    for k in range(FFN_K):
        gp = gp + w_ref[0, k:k + 1, :] * ext[pl.ds(FFN_HALO - (FFN_K - 1) + k, tm), :]
    return gp


def _ffn_act(gu, w_dw, b_dw, l):
    S, F2 = gu.shape
    F = F2 // 2
    cw = _ffn_chunk(F)
    ncc = F // cw
    tm = _tm(S)
    hb = tm // FFN_HALO

    def body(g_ref, u_ref, h_ref, w_ref, b_ref, o_ref, ext):
        i = pl.program_id(1)
        ext[0:FFN_HALO, :] = jnp.where(i > 0, h_ref[...], 0.0)
        ext[FFN_HALO:, :] = g_ref[...]
        gp = _ffn_conv(ext, w_ref, b_ref, tm)
        o_ref[...] = (gp * _sigmoid(gp) * u_ref[...]).astype(BF16)

    return pl.pallas_call(
        body, name="ffn_act", grid=(ncc, S // tm),
        in_specs=[pl.BlockSpec((tm, cw), lambda cc, i: (i, cc)),
                  pl.BlockSpec((tm, cw), lambda cc, i: (i, ncc + cc)),
                  pl.BlockSpec((FFN_HALO, cw), lambda cc, i: (jnp.maximum(i * hb - 1, 0), cc)),
                  pl.BlockSpec((1, FFN_K, cw), lambda cc, i: (l, 0, cc)),
                  pl.BlockSpec((None, 1, cw), lambda cc, i: (l, 0, cc))],
        out_specs=pl.BlockSpec((tm, cw), lambda cc, i: (i, cc)),
        out_shape=SDS((S, F), BF16),
        scratch_shapes=[pltpu.VMEM((tm + FFN_HALO, cw), F32)],
        compiler_params=_cp("parallel", "parallel"),
    )(gu, gu, gu, w_dw, b_dw[:, None, :])


def _loss_grad(y, target):
    S, D = y.shape
    tm = _tm(S)

    def body(y_ref, t_ref, dy_ref, l_ref):
        @pl.when(pl.program_id(0) == 0)
        def _():
            l_ref[...] = jnp.zeros_like(l_ref)
        e = y_ref[...] - t_ref[...]
        dy_ref[...] = e * (1.0 / D)
        l_ref[...] += jnp.broadcast_to(0.5 * jnp.sum(jnp.mean(e * e, axis=-1, keepdims=True)), (1, HEAD_DIM))

    row = pl.BlockSpec((tm, D), lambda i: (i, 0))
    dy, lsum = pl.pallas_call(
        body, name="loss_grad", grid=(S // tm,), in_specs=[row, row],
        out_specs=[row, pl.BlockSpec((1, HEAD_DIM), lambda i: (0, 0))],
        out_shape=[SDS((S, D), F32), SDS((1, HEAD_DIM), F32)],
        compiler_params=_cp("arbitrary"),
    )(y, target)
    return dy, lsum[0, 0]


def _gate_bwd(dx, t, gate, name):
    S, D = dx.shape
    tm = _tm(S)

    def body(dx_ref, t_ref, g_ref, e_ref, dg_ref):
        @pl.when(pl.program_id(0) == 0)
        def _():
            dg_ref[...] = jnp.zeros_like(dg_ref)
        dxv = dx_ref[...]
        e_ref[...] = (dxv * g_ref[...]).astype(BF16)
        dg_ref[...] += jnp.sum(dxv * t_ref[...], axis=0, keepdims=True)

    row = pl.BlockSpec((tm, D), lambda i: (i, 0))
    vec = pl.BlockSpec((1, D), lambda i: (0, 0))
    return pl.pallas_call(
        body, name=name, grid=(S // tm,), in_specs=[row, row, vec], out_specs=[row, vec],
        out_shape=[SDS((S, D), BF16), SDS((1, D), F32)], compiler_params=_cp("arbitrary"),
    )(dx, t, gate)


def _mm_nt(dy, w, l, name):
    S, C = dy.shape
    K = w.shape[1]
    tm = _tm(S, 1024 if K <= 1024 else 512)
    tc = CHUNK if C % CHUNK == 0 and C > 1024 else C
    nc = C // tc

    def body(dy_ref, w_ref, o_ref, acc):
        kk = pl.program_id(1)

        @pl.when(kk == 0)
        def _():
            acc[...] = jnp.zeros_like(acc)
        acc[...] += lax.dot_general(dy_ref[...], w_ref[0], NT_DIMS, preferred_element_type=F32)

        @pl.when(kk == nc - 1)
        def _():
            o_ref[...] = acc[...]

    return pl.pallas_call(
        body, name=name, grid=(S // tm, nc),
        in_specs=[pl.BlockSpec((tm, tc), lambda i, kk: (i, kk)),
                  pl.BlockSpec((1, K, tc), lambda i, kk: (l, 0, kk))],
        out_specs=pl.BlockSpec((tm, K), lambda i, kk: (i, 0)),
        out_shape=SDS((S, K), F32),
        scratch_shapes=[pltpu.VMEM((tm, K), F32)],
        compiler_params=_cp("parallel", "arbitrary"),
    )(dy, w)


def _mm_tn(a, b, buf, l, n_layers, col_sharded, name):
    S, K = a.shape
    N = b.shape[1]
    ts = _tm(S, 1024)
    ns = S // ts
    if col_sharded:
        cols, r2 = N // 4, K // 2
        grid = (4, ns)

        def body(*refs):
            a_ref, b_ref, o_ref = refs[0], refs[1], refs[-1]
            s = pl.program_id(1)

            @pl.when(s == 0)
            def _():
                o_ref[...] = jnp.zeros_like(o_ref)
            for hh in range(2):
                o_ref[hh] += lax.dot_general(a_ref[:, hh * r2:(hh + 1) * r2], b_ref[...], TN_DIMS,
                                             preferred_element_type=F32)

        in_specs = [pl.BlockSpec((ts, K), lambda jj, s: (s, 0)), pl.BlockSpec((ts, cols), lambda jj, s: (s, jj))]
        out_spec = pl.BlockSpec((None, None, 2, r2, cols), lambda jj, s: (l, jj, 0, 0, 0))
        scratch = []
    else:
        cols, r2 = N, K // 8
        tn = CHUNK if N % CHUNK == 0 else N
        grid = (N // tn, ns)

        def body(*refs):
            a_ref, b_ref, o_ref, acc = refs[0], refs[1], refs[-2], refs[-1]
            s = pl.program_id(1)

            @pl.when(s == 0)
            def _():
                acc[...] = jnp.zeros_like(acc)
            acc[...] += lax.dot_general(a_ref[...], b_ref[...], TN_DIMS, preferred_element_type=F32)

            @pl.when(s == ns - 1)
            def _():
                for jj in range(4):
                    for hh in range(2):
                        o_ref[jj, hh] = acc[(2 * jj + hh) * r2:(2 * jj + hh + 1) * r2, :]

        in_specs = [pl.BlockSpec((ts, K), lambda nb, s: (s, 0)), pl.BlockSpec((ts, tn), lambda nb, s: (s, nb))]
        out_spec = pl.BlockSpec((None, 4, 2, r2, tn), lambda nb, s: (l, 0, 0, 0, nb))
        scratch = [pltpu.VMEM((K, tn), F32)]
    args = [a, b]
    aliases = {}
    if buf is not None:
        in_specs = in_specs + [pl.BlockSpec(memory_space=pl.ANY)]
        args.append(buf)
        aliases = {2: 0}
    return pl.pallas_call(
        body, name=name, grid=grid, in_specs=in_specs, out_specs=out_spec,
        out_shape=SDS((n_layers, 4, 2, r2, cols), F32), scratch_shapes=scratch,
        input_output_aliases=aliases, compiler_params=_cp("parallel", "arbitrary"),
    )(*args)


def _norm_bwd(dh, x, dres, g, sc, name):
    S, D = x.shape
    tm = _tm(S)

    def body(dh_ref, x_ref, dr_ref, g_ref, sc_ref, dx_ref, sums_ref):
        @pl.when(pl.program_id(0) == 0)
        def _():
            sums_ref[...] = jnp.zeros_like(sums_ref)
        xv, dhv = x_ref[...], dh_ref[...]
        r = lax.rsqrt(jnp.mean(xv * xv, axis=-1, keepdims=True) + EPS)
        xn = xv * r
        gv, sv = g_ref[...], 1.0 + sc_ref[...]
        dxn = dhv * gv * sv
        dx_ref[...] = dr_ref[...] + r * (dxn - xn * jnp.mean(dxn * xn, axis=-1, keepdims=True))
        dhxn = dhv * xn
        sums_ref[0:1, :] += jnp.sum(dhv, axis=0, keepdims=True)
        sums_ref[1:2, :] += jnp.sum(dhxn * gv, axis=0, keepdims=True)
        sums_ref[2:3, :] += jnp.sum(dhxn * sv, axis=0, keepdims=True)

    row = pl.BlockSpec((tm, D), lambda i: (i, 0))
    vec = pl.BlockSpec((1, D), lambda i: (0, 0))
    return pl.pallas_call(
        body, name=name, grid=(S // tm,), in_specs=[row, row, row, vec, vec],
        out_specs=[row, pl.BlockSpec((8, D), lambda i: (0, 0))],
        out_shape=[SDS((S, D), F32), SDS((8, D), F32)], compiler_params=_cp("arbitrary"),
    )(dh, x, dres, g, sc)


def _ffn_act_bwd1(da, gu, w_dw, b_dw, l):
    S, F2 = gu.shape
    F = F2 // 2
    cw = _ffn_chunk(F)
    ncc = F // cw
    tm = _tm(S)
    hb = tm // FFN_HALO

    def body(da_ref, g_ref, u_ref, h_ref, w_ref, b_ref, dgp_ref, du_ref, ext):
        i = pl.program_id(1)
        ext[0:FFN_HALO, :] = jnp.where(i > 0, h_ref[...], 0.0)
        ext[FFN_HALO:, :] = g_ref[...]
        gp = _ffn_conv(ext, w_ref, b_ref, tm)
        sg = _sigmoid(gp)
        dav = da_ref[...]
        du_ref[...] = (dav * gp * sg).astype(BF16)
        dgp_ref[...] = dav * u_ref[...] * (sg * (1.0 + gp * (1.0 - sg)))

    return pl.pallas_call(
        body, name="ffn_act_bwd1", grid=(ncc, S // tm),
        in_specs=[pl.BlockSpec((tm, cw), lambda cc, i: (i, cc)),
                  pl.BlockSpec((tm, cw), lambda cc, i: (i, cc)),
                  pl.BlockSpec((tm, cw), lambda cc, i: (i, ncc + cc)),
                  pl.BlockSpec((FFN_HALO, cw), lambda cc, i: (jnp.maximum(i * hb - 1, 0), cc)),
                  pl.BlockSpec((1, FFN_K, cw), lambda cc, i: (l, 0, cc)),
                  pl.BlockSpec((None, 1, cw), lambda cc, i: (l, 0, cc))],
        out_specs=[pl.BlockSpec((tm, cw), lambda cc, i: (i, cc)),
                   pl.BlockSpec((tm, cw), lambda cc, i: (i, ncc + cc))],
        out_shape=[SDS((S, F), F32), SDS((S, F2), BF16)],
        scratch_shapes=[pltpu.VMEM((tm + FFN_HALO, cw), F32)],
        compiler_params=_cp("parallel", "parallel"),
    )(da, gu, gu, gu, w_dw, b_dw[:, None, :])


def _ffn_act_bwd2(dgp, dgu, gu, w_dw, l):
    S, F = dgp.shape
    cw = _ffn_chunk(F)
    ncc = F // cw
    tm = _tm(S)
    ni = S // tm
    hb = tm // FFN_HALO
    last_hb = S // FFN_HALO - 1

    def body(d_ref, dn_ref, g_ref, gh_ref, w_ref, _, dg_ref, sums_ref, extd, extg):
        i = pl.program_id(1)

        @pl.when(i == 0)
        def _():
            sums_ref[...] = jnp.zeros_like(sums_ref)
        dv = d_ref[...]
        extd[0:tm, :] = dv
        extd[tm:, :] = jnp.where(i < ni - 1, dn_ref[...], 0.0)
        extg[0:FFN_HALO, :] = jnp.where(i > 0, gh_ref[...], 0.0)
        extg[FFN_HALO:, :] = g_ref[...]
        acc = jnp.zeros((tm, cw), F32)
        for k in range(FFN_K):
            acc = acc + w_ref[0, k:k + 1, :] * extd[pl.ds(FFN_K - 1 - k, tm), :]
            sums_ref[k:k + 1, :] += jnp.sum(dv * extg[pl.ds(FFN_HALO - (FFN_K - 1) + k, tm), :], axis=0, keepdims=True)
        sums_ref[FFN_K:FFN_K + 1, :] += jnp.sum(dv, axis=0, keepdims=True)
        dg_ref[...] = acc.astype(BF16)

    return pl.pallas_call(
        body, name="ffn_act_bwd2", grid=(ncc, ni),
        in_specs=[pl.BlockSpec((tm, cw), lambda cc, i: (i, cc)),
                  pl.BlockSpec((FFN_HALO, cw), lambda cc, i: (jnp.minimum((i + 1) * hb, last_hb), cc)),
                  pl.BlockSpec((tm, cw), lambda cc, i: (i, cc)),
                  pl.BlockSpec((FFN_HALO, cw), lambda cc, i: (jnp.maximum(i * hb - 1, 0), cc)),
                  pl.BlockSpec((1, FFN_K, cw), lambda cc, i: (l, 0, cc)),
                  pl.BlockSpec(memory_space=pl.ANY)],
        out_specs=[pl.BlockSpec((tm, cw), lambda cc, i: (i, cc)),
                   pl.BlockSpec((8, cw), lambda cc, i: (0, cc))],
        out_shape=[SDS((S, 2 * F), BF16), SDS((8, F), F32)],
        scratch_shapes=[pltpu.VMEM((tm + FFN_HALO, cw), F32), pltpu.VMEM((tm + FFN_HALO, cw), F32)],
        input_output_aliases={5: 0},
        compiler_params=_cp("parallel", "arbitrary"),
    )(dgp, dgp, gu, gu, w_dw, dgu)


def _merge_bwd(dm, z, y_a, y_b, D):
    S = z.shape[0]
    tm = _tm(S)
    ncc = D // CHUNK
    ga0 = (3 * ATTN_W + 2 * D) // CHUNK

    def body(dm_ref, ga_ref, gb_ref, ya_ref, yb_ref, dya_ref, dyb_ref, dga_ref, dgb_ref):
        dmv = dm_ref[...]
        sa, sb = _sigmoid(ga_ref[...]), _sigmoid(gb_ref[...])
        dya_ref[...] = (dmv * sa).astype(BF16)
        dyb_ref[...] = (dmv * sb).astype(BF16)
        dga_ref[...] = (dmv * ya_ref[...] * sa * (1.0 - sa)).astype(BF16)
        dgb_ref[...] = (dmv * yb_ref[...] * sb * (1.0 - sb)).astype(BF16)

    ysp = pl.BlockSpec((tm, CHUNK), lambda i, cc: (i, cc))
    return pl.pallas_call(
        body, name="merge_bwd", grid=(S // tm, ncc),
        in_specs=[ysp, pl.BlockSpec((tm, CHUNK), lambda i, cc: (i, ga0 + cc)),
                  pl.BlockSpec((tm, CHUNK), lambda i, cc: (i, ga0 + ncc + cc)), ysp, ysp],
        out_specs=[ysp] * 4, out_shape=[SDS((S, D), BF16)] * 4,
        compiler_params=_cp("parallel", "parallel"),
    )(dm, z, z, y_a, y_b)


def _ln_silu_bwd(du3, u1, g, b):
    S, D = u1.shape
    tm = _tm(S)

    def body(d_ref, u_ref, g_ref, b_ref, du_ref, sums_ref):
        @pl.when(pl.program_id(0) == 0)
        def _():
            sums_ref[...] = jnp.zeros_like(sums_ref)
        xh, rstd = _ln_stats(u_ref[...])
        gv = g_ref[...]
        yv = xh * gv + b_ref[...]
        sg = _sigmoid(yv)
        dy = d_ref[...] * (sg * (1.0 + yv * (1.0 - sg)))
        dxh = dy * gv
        du_ref[...] = rstd * (dxh - jnp.mean(dxh, axis=-1, keepdims=True)
                              - xh * jnp.mean(dxh * xh, axis=-1, keepdims=True))
        sums_ref[0:1, :] += jnp.sum(dy * xh, axis=0, keepdims=True)
        sums_ref[1:2, :] += jnp.sum(dy, axis=0, keepdims=True)

    row = pl.BlockSpec((tm, D), lambda i: (i, 0))
    vec = pl.BlockSpec((1, D), lambda i: (0, 0))
    return pl.pallas_call(
        body, name="ln_silu_bwd", grid=(S // tm,), in_specs=[row, row, vec, vec],
        out_specs=[row, pl.BlockSpec((8, D), lambda i: (0, 0))],
        out_shape=[SDS((S, D), F32), SDS((8, D), F32)], compiler_params=_cp("arbitrary"),
    )(du3, u1, g, b)


def _glu_conv_bwd(du1, z, w_dw, l, D):
    S = z.shape[0]
    tm = _tm(S)
    ni = S // tm
    cc0 = 3 * ATTN_W // CHUNK
    ncc = D // CHUNK
    hb = tm // CONV_HALO
    last_hb = S // CONV_HALO - 1
    lead = CONV_HALO - (CONV_K - 1)
    group = 8

    def body(d_ref, dn_ref, cv_ref, cg_ref, hv_ref, hg_ref, w_ref, dcv_ref, dcg_ref, sums_ref,
             extd, extu, shd, shu):
        i = pl.program_id(1)

        @pl.when(i == 0)
        def _():
            sums_ref[...] = jnp.zeros_like(sums_ref)
        extd[0:tm, :] = d_ref[...]
        extd[tm:, :] = jnp.where(i < ni - 1, dn_ref[...], 0.0)
        extu[0:CONV_HALO, :] = jnp.where(i > 0, _glu(hv_ref[...], hg_ref[...]), 0.0)
        extu[CONV_HALO:, :] = _glu(cv_ref[...], cg_ref[...])
        _fill_shifted(shd, extd, tm + CONV_HALO - SHIFTS)
        _fill_shifted(shu, extu, tm + CONV_HALO - SHIFTS)

        def strip(s, carry):
            base = pl.multiple_of(s * STRIP, STRIP)
            acc = jnp.zeros((STRIP, CHUNK), F32)
            for k in range(CONV_K):
                acc = acc + w_ref[0, k:k + 1, :] * _tap(shd, base, CONV_K - 1 - k)
            cv, cg = cv_ref[pl.ds(base, STRIP), :], cg_ref[pl.ds(base, STRIP), :]
            sg = _sigmoid(cg)
            dcv_ref[pl.ds(base, STRIP), :] = (acc * sg).astype(BF16)
            dcg_ref[pl.ds(base, STRIP), :] = (acc * cv * sg * (1.0 - sg)).astype(BF16)
            return carry

        lax.fori_loop(0, tm // STRIP, strip, 0)

        for k0 in range(0, CONV_K + 1, group):
            ks = list(range(k0, min(k0 + group, CONV_K + 1)))

            def rows(s, accs, ks=ks):
                base = pl.multiple_of(s * SHIFTS, SHIFTS)
                dv = extd[pl.ds(base, SHIFTS), :]
                return tuple(a + (dv if k == CONV_K else dv * _tap(shu, base, lead + k, SHIFTS))
                             for a, k in zip(accs, ks))

            accs = lax.fori_loop(0, tm // SHIFTS, rows,
                                 tuple(jnp.zeros((SHIFTS, CHUNK), F32) for _ in ks))
            for k, a in zip(ks, accs):
                sums_ref[k:k + 1, :] += jnp.sum(a, axis=0, keepdims=True)

    cur = lambda off: pl.BlockSpec((tm, CHUNK), lambda cc, i: (i, cc0 + off + cc))
    halo = lambda off: pl.BlockSpec((CONV_HALO, CHUNK), lambda cc, i: (jnp.maximum(i * hb - 1, 0), cc0 + off + cc))
    osp = pl.BlockSpec((tm, CHUNK), lambda cc, i: (i, cc))
    big = pltpu.VMEM((tm + CONV_HALO, CHUNK), F32)
    shifted = pltpu.VMEM((SHIFTS, tm + CONV_HALO, CHUNK), F32)
    return pl.pallas_call(
        body, name="glu_conv_bwd", grid=(ncc, ni),
        in_specs=[osp,
                  pl.BlockSpec((CONV_HALO, CHUNK), lambda cc, i: (jnp.minimum((i + 1) * hb, last_hb), cc)),
                  cur(0), cur(ncc), halo(0), halo(ncc),
                  pl.BlockSpec((1, CONV_K, CHUNK), lambda cc, i: (l, 0, cc))],
        out_specs=[osp, osp, pl.BlockSpec((CONV_HALO, CHUNK), lambda cc, i: (0, cc))],
        out_shape=[SDS((S, D), BF16), SDS((S, D), BF16), SDS((CONV_HALO, D), F32)],
        scratch_shapes=[big, big, shifted, shifted],
        compiler_params=_cp("parallel", "arbitrary"),
    )(du1, du1, z, z, z, z, w_dw)


def _attn_bwd_prep(dattn, outs, lses):
    S = dattn.shape[0]
    tm = _tm(S)
    ng = len(DILATIONS)

    def body(*refs):
        da_ref, o_refs, l_refs = refs[0], refs[1:1 + ng], refs[1 + ng:1 + 2 * ng]
        d_refs, t_refs = refs[1 + 2 * ng:1 + 3 * ng], refs[1 + 3 * ng:1 + 4 * ng]
        scr = refs[1 + 4 * ng:]
        for h in range(HEADS_PER_GROUP):
            cols = slice(h * HEAD_DIM, (h + 1) * HEAD_DIM)
            ov = [_from_dilated(scr[g], o_refs[g], DILATIONS[g], cols) for g in range(ng)]
            lv = [_from_dilated(scr[ng + g], l_refs[g], DILATIONS[g], cols) for g in range(ng)]
            w = _group_weights(lv)
            dav = da_ref[:, cols]
            rs = jnp.sum(dav * (w[0] * ov[0] + w[1] * ov[1] + w[2] * ov[2]), axis=-1, keepdims=True)
            rs = jnp.broadcast_to(rs, (tm, HEAD_DIM))
            for g in range(ng):
                _to_dilated(scr[g], w[g] * dav, d_refs[g], DILATIONS[g], cols)
                _to_dilated(scr[ng + g], -w[g] * rs, t_refs[g], DILATIONS[g], cols)

    dil = [_dil_spec(d, tm) for d in DILATIONS]
    out = pl.pallas_call(
        body, name="attn_bwd_prep", grid=(S // tm,),
        in_specs=[pl.BlockSpec((tm, GROUP_W), lambda i: (i, 0))] + dil + dil, out_specs=dil + dil,
        out_shape=[SDS((d, S // d, GROUP_W), BF16) for d in DILATIONS]
        + [SDS((d, S // d, GROUP_W), F32) for d in DILATIONS],
        scratch_shapes=[pltpu.VMEM((tm, HEAD_DIM), F32)] * (2 * ng),
        compiler_params=_cp("parallel"),
    )(dattn, *outs, *lses)
    return out[:ng], out[ng:]


def _attn_bwd(q, k, v, do, lse, dterm, gi):
    d, Sd, _ = q.shape
    nb = Sd // BLOCK
    scale = HEAD_DIM ** -0.5

    def body(q_ref, qx_ref, kc_ref, kp_ref, vc_ref, vp_ref, do_ref, dox_ref, l_ref, lx_ref, t_ref, tx_ref,
             dq_ref, dk_ref, dv_ref):
        n = pl.program_id(1)
        qi = lax.broadcasted_iota(jnp.int32, (BLOCK, BLOCK), 0)
        kj = lax.broadcasted_iota(jnp.int32, (BLOCK, BLOCK), 1)
        mask_a = kj <= qi
        mask_b = jnp.logical_and(kj >= qi, n > 0)
        mask_c = jnp.logical_and(kj >= qi, n < nb - 1)

        def pair(qh, kh, vh, dov, lv, tv, mask):
            s = lax.dot_general(qh, kh, NT_DIMS, preferred_element_type=F32) * scale
            p = jnp.exp(jnp.where(mask, s - lv, NEG))
            dp = lax.dot_general(dov, vh, NT_DIMS, preferred_element_type=F32)
            ds = p * (dp + tv) * scale
            return p.astype(BF16), ds.astype(BF16)

        for h in range(HEADS_PER_GROUP):
            sl = slice(h * HEAD_DIM, (h + 1) * HEAD_DIM)
            qh, qx = q_ref[:, sl], qx_ref[:, sl]
            kc, kp = kc_ref[:, sl], kp_ref[:, sl]
            vc, vp = vc_ref[:, sl], vp_ref[:, sl]
            dov, dox = do_ref[:, sl], dox_ref[:, sl]
            p_a, ds_a = pair(qh, kc, vc, dov, l_ref[:, sl], t_ref[:, sl], mask_a)
            _, ds_b = pair(qh, kp, vp, dov, l_ref[:, sl], t_ref[:, sl], mask_b)
            p_c, ds_c = pair(qx, kc, vc, dox, lx_ref[:, sl], tx_ref[:, sl], mask_c)
            dq_ref[:, sl] = (jnp.dot(ds_a, kc, preferred_element_type=F32)
                             + jnp.dot(ds_b, kp, preferred_element_type=F32))
            dk_ref[:, sl] = (lax.dot_general(ds_a, qh, TN_DIMS, preferred_element_type=F32)
                             + lax.dot_general(ds_c, qx, TN_DIMS, preferred_element_type=F32))
            dv_ref[:, sl] = (lax.dot_general(p_a, dov, TN_DIMS, preferred_element_type=F32)
                             + lax.dot_general(p_c, dox, TN_DIMS, preferred_element_type=F32)).astype(BF16)

    cur, prev, nxt = _attn_blk(nb, 0), _attn_blk(nb, -1), _attn_blk(nb, 1)
    return pl.pallas_call(
        body, name=f"attn_bwd_g{gi}", grid=(d, nb),
        in_specs=[cur, nxt, cur, prev, cur, prev, cur, nxt, cur, nxt, cur, nxt],
        out_specs=[cur] * 3,
        out_shape=[SDS((d, Sd, GROUP_W), F32)] * 2 + [SDS((d, Sd, GROUP_W), BF16)],
        compiler_params=_cp("parallel", "parallel"),
    )(q, q, k, k, v, v, do, do, lse, lse, dterm, dterm)


def _qk_prep_bwd(dqs, dks, dvs, z, g_q, g_k, tabs):
    S = z.shape[0]
    tm = _tm(S)
    n_heads = ATTN_W // HEAD_DIM
    ng = len(DILATIONS)

    def body(*refs):
        dq_refs, dk_refs, dv_refs = refs[:ng], refs[ng:2 * ng], refs[2 * ng:3 * ng]
        q_ref, k_ref, gq_ref, gk_ref, c_ref, sa_ref, sb_ref = refs[3 * ng:3 * ng + 7]
        dqo_ref, dko_ref, dvo_ref, sums_ref, scr = refs[3 * ng + 7:]

        @pl.when(pl.program_id(0) == 0)
        def _():
            sums_ref[...] = jnp.zeros_like(sums_ref)
        cos_t, sa_t, sb_t = c_ref[...], sa_ref[...], sb_ref[...]
        for row, (drefs, src, gref, dst) in enumerate(((dq_refs, q_ref, gq_ref, dqo_ref),
                                                       (dk_refs, k_ref, gk_ref, dko_ref))):
            gv = gref[...]
            gsum = jnp.zeros((1, HEAD_DIM), F32)
            for h in range(n_heads):
                gi = h // HEADS_PER_GROUP
                sl = slice(h * HEAD_DIM, (h + 1) * HEAD_DIM)
                cols = slice((h % HEADS_PER_GROUP) * HEAD_DIM, (h % HEADS_PER_GROUP + 1) * HEAD_DIM)
                dyn = _unrope(_from_dilated(scr, drefs[gi], DILATIONS[gi], cols), cos_t, sa_t, sb_t)
                t = src[:, sl]
                r = lax.rsqrt(jnp.mean(t * t, axis=-1, keepdims=True) + EPS)
                xh = t * r
                gsum = gsum + jnp.sum(dyn * xh, axis=0, keepdims=True)
                gy = dyn * gv
                dst[:, sl] = (r * (gy - xh * jnp.mean(gy * xh, axis=-1, keepdims=True))).astype(BF16)
            sums_ref[row:row + 1, :] += gsum
        for h in range(n_heads):
            gi = h // HEADS_PER_GROUP
            cols = slice((h % HEADS_PER_GROUP) * HEAD_DIM, (h % HEADS_PER_GROUP + 1) * HEAD_DIM)
            dvo_ref[:, h * HEAD_DIM:(h + 1) * HEAD_DIM] = _from_dilated(
                scr, dv_refs[gi], DILATIONS[gi], cols).astype(BF16)

    dil = [_dil_spec(d, tm) for d in DILATIONS]
    tab = pl.BlockSpec((tm, HEAD_DIM), lambda i: (i, 0))
    gsp = pl.BlockSpec((1, HEAD_DIM), lambda i: (0, 0))
    wide = pl.BlockSpec((tm, ATTN_W), lambda i: (i, 0))
    return pl.pallas_call(
        body, name="qk_prep_bwd", grid=(S // tm,),
        in_specs=dil * 3 + [wide, pl.BlockSpec((tm, ATTN_W), lambda i: (i, 1)), gsp, gsp, tab, tab, tab],
        out_specs=[wide, wide, wide, pl.BlockSpec((8, HEAD_DIM), lambda i: (0, 0))],
        out_shape=[SDS((S, ATTN_W), BF16)] * 3 + [SDS((8, HEAD_DIM), F32)],
        scratch_shapes=[pltpu.VMEM((tm, HEAD_DIM), F32)],
        compiler_params=_cp("arbitrary"),
    )(*dqs, *dks, *dvs, z, z, g_q, g_k, *tabs)


def _adamw(w, g, m, v, name):
    R, C = w.shape
    tr = _row_tile(R, C * 2)
    c1 = 1.0 - ADAM_B1 ** ADAM_STEP
    c2 = 1.0 - ADAM_B2 ** ADAM_STEP

    def body(w_ref, g_ref, m_ref, v_ref, d_ref, mo_ref, vo_ref):
        gv = g_ref[...]
        mn = ADAM_B1 * m_ref[...] + (1.0 - ADAM_B1) * gv
        vn = ADAM_B2 * v_ref[...] + (1.0 - ADAM_B2) * (gv * gv)
        mo_ref[...] = mn
        vo_ref[...] = vn
        d_ref[...] = -ADAM_LR * ((mn / c1) / (jnp.sqrt(vn / c2) + ADAM_EPS) + ADAM_WD * w_ref[...])

    sp = pl.BlockSpec((tr, C), lambda i: (i, 0))
    return pl.pallas_call(
        body, name=name, grid=(R // tr,), in_specs=[sp] * 4, out_specs=[sp] * 3,
        out_shape=[SDS((R, C), F32)] * 3, compiler_params=_cp("parallel"),
    )(w, g, m, v)


def _sum_parts(parts):
    n, R, C = parts.shape

    def body(p_ref, o_ref):
        acc = p_ref[0]
        for k in range(1, n):
            acc = acc + p_ref[k]
        o_ref[...] = acc

    return pl.pallas_call(
        body, name="sum_parts", out_shape=SDS((R, C), F32),
        compiler_params=pltpu.CompilerParams(vmem_limit_bytes=VMEM_LIMIT_BYTES),
    )(parts)


def _wada_grad(cact_t, dmod):
    D, n = cact_t.shape
    L, _, cols = dmod.shape

    def body(c_ref, d_ref, o_ref):
        acc = c_ref[:, 0:1] * d_ref[0, 0:1, :]
        for k in range(1, n):
            acc = acc + c_ref[:, k:k + 1] * d_ref[0, k:k + 1, :]
        o_ref[0] = acc

    return pl.pallas_call(
        body, name="wada_grad", grid=(L,),
        in_specs=[pl.BlockSpec((D, n), lambda l: (0, 0)), pl.BlockSpec((1, n, cols), lambda l: (l, 0, 0))],
        out_specs=pl.BlockSpec((1, D, cols), lambda l: (l, 0, 0)),
        out_shape=SDS((L, D, cols), F32), compiler_params=_cp("parallel"),
    )(cact_t, dmod)


def _add_halves(dw, land, c_idx, name):
    L, _, _, r2, cols = dw.shape
    tr = _row_tile(r2, cols, step=16)
    dw4 = dw.reshape(L * 4, 2, r2, cols)
    land3 = land.reshape(L * 4, r2, cols)

    def body(pf, a_ref, b_ref, o_ref):
        o_ref[...] = (a_ref[...] + b_ref[...]).astype(BF16)

    out = pl.pallas_call(
        body, name=name,
        grid_spec=pltpu.PrefetchScalarGridSpec(
            num_scalar_prefetch=1, grid=(L * 4, r2 // tr),
            in_specs=[pl.BlockSpec((None, None, tr, cols), lambda a, i, pf: (a, pf[0], i, 0)),
                      pl.BlockSpec((None, tr, cols), lambda a, i, pf: (a, i, 0))],
            out_specs=pl.BlockSpec((None, tr, cols), lambda a, i, pf: (a, i, 0))),
        out_shape=SDS((L * 4, r2, cols), BF16), compiler_params=_cp("parallel", "parallel"),
    )(c_idx, dw4, land3)
    return out.reshape(L, 4, r2, cols)


def _add_chips(dw, land1, land2, cj_idx, name):
    L, _, _, r2, cols = dw.shape
    tr = _row_tile(r2, cols, step=16)

    def body(pf, a_ref, b_ref, c0, c1, c2, o_ref):
        own = a_ref[...] + b_ref[...]
        o_ref[...] = ((own + c0[...].astype(F32)) + c1[...].astype(F32)) + c2[...].astype(F32)

    lsp = lambda p: pl.BlockSpec((None, None, tr, cols), lambda l, i, pf: (p, l, i, 0))
    return pl.pallas_call(
        body, name=name,
        grid_spec=pltpu.PrefetchScalarGridSpec(
            num_scalar_prefetch=1, grid=(L, r2 // tr),
            in_specs=[pl.BlockSpec((None, None, None, tr, cols), lambda l, i, pf: (l, pf[1], pf[0], i, 0)),
                      pl.BlockSpec((None, None, tr, cols), lambda l, i, pf: (l, pf[1], i, 0)),
                      lsp(0), lsp(1), lsp(2)],
            out_specs=pl.BlockSpec((None, None, tr, cols), lambda l, i, pf: (l, pf[0], i, 0))),
        out_shape=SDS((L, 2, r2, cols), F32), compiler_params=_cp("parallel", "parallel"),
    )(cj_idx, dw, land1, land2, land2, land2)


HBM_SPEC = pl.BlockSpec(memory_space=pltpu.HBM)


def _place():
    x, y, c = lax.axis_index("x"), lax.axis_index("y"), lax.axis_index("c")
    chips = [(x, 1 - y), (1 - x, y), (1 - x, 1 - y)]
    return x, y, c, chips


def _remote(src, dst, send_sem, recv_sem, device):
    return pltpu.make_async_remote_copy(src_ref=src, dst_ref=dst, send_sem=send_sem, recv_sem=recv_sem,
                                        device_id=device, device_id_type=MESH)


def _comm_call(body, name, ins, out_shapes, n_sems):
    return pl.pallas_call(
        body, name=name, in_specs=[HBM_SPEC] * len(ins), out_specs=[HBM_SPEC] * len(out_shapes),
        out_shape=out_shapes,
        scratch_shapes=[pltpu.SemaphoreType.DMA((n,)) for n in n_sems],
    )(*ins)


def _all_gather_weights(shards, small):
    nw = len(shards)
    L = shards[0].shape[0]
    Lh = L // 2

    def body(*refs):
        sh, sm = refs[:nw], refs[nw]
        full, smo = refs[nw + 1:2 * nw + 1], refs[2 * nw + 1]
        ici_s, ici_r, d2d_s, d2d_r, loc = refs[2 * nw + 2:]
        x, y, c, chips = _place()
        j = 2 * x + y
        jps = [2 * px + py for px, py in chips]
        me, sib = (x, y, c), (x, y, 1 - c)
        mine, theirs = pl.ds(c * Lh, Lh), pl.ds((1 - c) * Lh, Lh)

        local = [pltpu.make_async_copy(sh[w], full[w].at[j], loc.at[w]) for w in range(nw)]
        local.append(pltpu.make_async_copy(sm, smo.at[j], loc.at[nw]))
        for cp in local:
            cp.start()
        sends = []
        for w in range(nw):
            for p in range(3):
                sends.append(_remote(sh[w].at[mine], full[w].at[j, mine], ici_s.at[3 * w + p], ici_r.at[3 * w + p],
                                     (*chips[p], c)))
        for p in range(3):
            sends.append(_remote(sm, smo.at[j], ici_s.at[3 * nw + p], ici_r.at[3 * nw + p], (*chips[p], c)))
        for cp in sends:
            cp.start()
        passed = []
        for w in range(nw):
            for p in range(3):
                slot = full[w].at[jps[p], mine]
                _remote(slot, slot, ici_s.at[3 * w + p], ici_r.at[3 * w + p], me).wait_recv()
                cp = _remote(slot, slot, d2d_s.at[3 * w + p], d2d_r.at[3 * w + p], sib)
                cp.start()
                passed.append(cp)
        for p in range(3):
            slot = smo.at[jps[p]]
            _remote(slot, slot, ici_s.at[3 * nw + p], ici_r.at[3 * nw + p], me).wait_recv()
        for w in range(nw):
            for p in range(3):
                slot = full[w].at[jps[p], theirs]
                _remote(slot, slot, d2d_s.at[3 * w + p], d2d_r.at[3 * w + p], me).wait_recv()
        for cp in sends + passed:
            cp.wait_send()
        for cp in local:
            cp.wait()

    outs = [SDS((4,) + s.shape, s.dtype) for s in shards] + [SDS((4,) + small.shape, small.dtype)]
    res = _comm_call(body, "all_gather_weights", list(shards) + [small], outs,
                     [3 * nw + 3, 3 * nw + 3, 3 * nw, 3 * nw, nw + 1])
    return res[:nw], res[nw]


def _all_gather_small(part):
    def body(p_ref, o_ref, send_s, recv_s, loc):
        x, y, c, _ = _place()
        me_id = 4 * x + 2 * y + c
        own = pltpu.make_async_copy(p_ref, o_ref.at[me_id], loc.at[0])
        own.start()
        sends = []
        for k in range(1, 8):
            peer = (x ^ (k >> 2), y ^ ((k >> 1) & 1), c ^ (k & 1))
            sends.append(_remote(p_ref, o_ref.at[me_id], send_s.at[k - 1], recv_s.at[k - 1], peer))
        for cp in sends:
            cp.start()
        for k in range(1, 8):
            peer_id = 4 * (x ^ (k >> 2)) + 2 * (y ^ ((k >> 1) & 1)) + (c ^ (k & 1))
            slot = o_ref.at[peer_id]
            _remote(slot, slot, send_s.at[k - 1], recv_s.at[k - 1], (x, y, c)).wait_recv()
        for cp in sends:
            cp.wait_send()
        own.wait()

    return _comm_call(body, "all_gather_small", [part], [SDS((8,) + part.shape, part.dtype)], [7, 7, 1])[0]


def _rs_to_sibling(dws):
    nw = len(dws)

    def body(*refs):
        src, land = refs[:nw], refs[nw:2 * nw]
        send_s, recv_s = refs[2 * nw:]
        x, y, c, _ = _place()
        cps = [_remote(src[w].at[:, :, 1 - c], land[w], send_s.at[w], recv_s.at[w], (x, y, 1 - c))
               for w in range(nw)]
        for cp in cps:
            cp.start()
        for cp in cps:
            cp.wait_recv()
        for cp in cps:
            cp.wait_send()

    outs = [SDS((a.shape[0], 4, a.shape[3], a.shape[4]), a.dtype) for a in dws]
    return _comm_call(body, "rs_to_sibling", list(dws), outs, [nw, nw])


def _rs_to_chips(s1s):
    nw = len(s1s)

    def body(*refs):
        src, land = refs[:nw], refs[nw:2 * nw]
        send_s, recv_s = refs[2 * nw:]
        x, y, c, chips = _place()
        cps = []
        for w in range(nw):
            for p in range(3):
                jp = 2 * chips[p][0] + chips[p][1]
                cps.append(_remote(src[w].at[:, jp], land[w].at[p], send_s.at[3 * w + p], recv_s.at[3 * w + p],
                                   (*chips[p], c)))
        for cp in cps:
            cp.start()
        for cp in cps:
            cp.wait_recv()
        for cp in cps:
            cp.wait_send()

    outs = [SDS((3, a.shape[0], a.shape[2], a.shape[3]), a.dtype) for a in s1s]
    return _comm_call(body, "rs_to_chips", list(s1s), outs, [3 * nw, 3 * nw])


def _rs_join_halves(fulls):
    nw = len(fulls)

    def body(*refs):
        full = refs[nw:2 * nw]
        send_s, recv_s = refs[2 * nw:]
        x, y, c, _ = _place()
        cps = [_remote(full[w].at[:, c], full[w].at[:, c], send_s.at[w], recv_s.at[w], (x, y, 1 - c))
               for w in range(nw)]
        for cp in cps:
            cp.start()
        for w in range(nw):
            slot = full[w].at[:, 1 - c]
            _remote(slot, slot, send_s.at[w], recv_s.at[w], (x, y, c)).wait_recv()
        for cp in cps:
            cp.wait_send()

    return pl.pallas_call(
        body, name="rs_join_halves", in_specs=[HBM_SPEC] * nw, out_specs=[HBM_SPEC] * nw,
        out_shape=[SDS(a.shape, a.dtype) for a in fulls],
        scratch_shapes=[pltpu.SemaphoreType.DMA((nw,)), pltpu.SemaphoreType.DMA((nw,))],
        input_output_aliases={w: w for w in range(nw)},
    )(*fulls)


BIG = ("w_in", "w_attn_proj", "w_conv_out", "w_o", "w_ffn_in", "w_ffn_down")
COL_SHARDED = {"w_in": True, "w_attn_proj": True, "w_conv_out": False, "w_o": False,
               "w_ffn_in": True, "w_ffn_down": False}
SMALL = ("b_ada", "g_norm1", "g_q", "g_k", "w_conv_dw", "b_conv_dw", "g_conv_ln", "b_conv_ln",
         "g_norm2", "w_ffn_dw", "b_ffn_dw")
WEIGHTS = ("w_ada", "b_ada", "g_norm1", "w_in", "g_q", "g_k", "w_attn_proj", "w_conv_dw", "b_conv_dw",
           "g_conv_ln", "b_conv_ln", "w_conv_out", "w_o", "g_norm2", "w_ffn_in", "w_ffn_dw", "b_ffn_dw",
           "w_ffn_down")


def _pack_rows(arrs):
    flat = jnp.concatenate([a.reshape(-1) for a in arrs])
    pad = (-flat.shape[0]) % (8 * 128)
    return jnp.pad(flat, (0, pad)).reshape(-1, 128)


def _unpack_rows(packed, shapes):
    flat = packed.reshape(-1)
    out, off = [], 0
    for shp in shapes:
        n = 1
        for s in shp:
            n *= s
        out.append(flat[off:off + n].reshape(shp))
        off += n
    return out


def _rope_tables(positions):
    half = ROT_DIM // 2
    inv_freq = ROPE_THETA ** (-jnp.arange(0, ROT_DIM, 2, dtype=F32) / ROT_DIM)
    ang = positions.astype(F32)[..., None] * inv_freq
    cos, sin = jnp.cos(ang), jnp.sin(ang)
    S = cos.shape[0]
    cos_t = jnp.concatenate([cos, cos, jnp.ones((S, HEAD_DIM - ROT_DIM), F32)], axis=1)
    sa_t = jnp.concatenate([-sin, jnp.zeros((S, HEAD_DIM - half), F32)], axis=1)
    sb_t = jnp.concatenate([jnp.zeros((S, half), F32), sin, jnp.zeros((S, HEAD_DIM - ROT_DIM), F32)], axis=1)
    return cos_t, sa_t, sb_t


def _local_step(x, c, positions, loss_target, wf, small_w):
    S, D = x.shape
    L = wf["w_in"].shape[0]
    tabs = _rope_tables(positions)
    mod, cact = _mod_fwd(c, wf["w_ada"], small_w["b_ada"])
    vec = lambda a: a.reshape(1, -1)
    mods = [[vec(mod[l, k * D:(k + 1) * D]) for k in range(6)] for l in range(L)]

    saved = []
    xs = x
    for l in range(L):
        sh1, sc1, gt1, sh2, sc2, gt2 = mods[l]
        g1, g2 = vec(small_w["g_norm1"][l]), vec(small_w["g_norm2"][l])
        gq, gk = vec(small_w["g_q"][l]), vec(small_w["g_k"][l])
        z, h = _normmod_mm(xs, g1, sc1, sh1, wf["w_in"], l, "in_proj")
        qs, ks, vs = _qk_prep(z, gq, gk, tabs)
        outs, lses = [], []
        for gi in range(len(DILATIONS)):
            o_g, lse_g = _attn_fwd(qs[gi], ks[gi], vs[gi], gi)
            outs.append(o_g)
            lses.append(lse_g)
        attn = _attn_combine(outs, lses)
        y_a = _mm_nn(attn, wf["w_attn_proj"], l, "attn_proj")
        u1 = _glu_conv(z, small_w["w_conv_dw"], small_w["b_conv_dw"], l, D)
        u3 = _ln_silu(u1, vec(small_w["g_conv_ln"][l]), vec(small_w["b_conv_ln"][l]))
        y_b = _mm_nn(u3, wf["w_conv_out"], l, "conv_out")
        merged = _merge(z, y_a, y_b, D)
        t_o, x2 = _mm_nn(merged, wf["w_o"], l, "out_proj", res=xs, gate=gt1)
        gu, h2 = _normmod_mm(x2, g2, sc2, sh2, wf["w_ffn_in"], l, "ffn_in")
        a = _ffn_act(gu, small_w["w_ffn_dw"], small_w["b_ffn_dw"], l)
        t_f, x3 = _mm_nn(a, wf["w_ffn_down"], l, "ffn_down", res=x2, gate=gt2)
        saved.append(dict(x=xs, z=z, h=h, qs=qs, ks=ks, vs=vs, outs=outs, lses=lses, attn=attn, y_a=y_a, u1=u1, u3=u3,
                          y_b=y_b, merged=merged, t_o=t_o, x2=x2, gu=gu, h2=h2, a=a, t_f=t_f))
        xs = x3

    dx, loss = _loss_grad(xs, loss_target)

    bufs = {n: None for n in BIG}
    small_g = {n: [None] * L for n in SMALL}
    dmods = [None] * L

    def wgrad(name, a_mat, b_mat, l):
        bufs[name] = _mm_tn(a_mat, b_mat, bufs[name], l, L, COL_SHARDED[name], "grad_" + name)

    for l in reversed(range(L)):
        sv = saved[l]
        sh1, sc1, gt1, sh2, sc2, gt2 = mods[l]
        g1, g2 = vec(small_w["g_norm1"][l]), vec(small_w["g_norm2"][l])
        gq, gk = vec(small_w["g_q"][l]), vec(small_w["g_k"][l])
        e2, dgt2 = _gate_bwd(dx, sv["t_f"], gt2, "gate_bwd_ffn")
        wgrad("w_ffn_down", sv["a"], e2, l)
        da = _mm_nt(e2, wf["w_ffn_down"], l, "ffn_down_bwd")
        dgp, dgu = _ffn_act_bwd1(da, sv["gu"], small_w["w_ffn_dw"], small_w["b_ffn_dw"], l)
        dgu, ffn_sums = _ffn_act_bwd2(dgp, dgu, sv["gu"], small_w["w_ffn_dw"], l)
        wgrad("w_ffn_in", sv["h2"], dgu, l)
        dh2 = _mm_nt(dgu, wf["w_ffn_in"], l, "ffn_in_bwd")
        dx2, n2_sums = _norm_bwd(dh2, sv["x2"], dx, g2, sc2, "norm2_bwd")
        e1, dgt1 = _gate_bwd(dx2, sv["t_o"], gt1, "gate_bwd_mix")
        wgrad("w_o", sv["merged"], e1, l)
        dmerged = _mm_nt(e1, wf["w_o"], l, "out_proj_bwd")
        dya, dyb, dga, dgb = _merge_bwd(dmerged, sv["z"], sv["y_a"], sv["y_b"], D)
        wgrad("w_attn_proj", sv["attn"], dya, l)
        wgrad("w_conv_out", sv["u3"], dyb, l)
        dattn = _mm_nt(dya, wf["w_attn_proj"], l, "attn_proj_bwd")
        du3 = _mm_nt(dyb, wf["w_conv_out"], l, "conv_out_bwd")
        du1, ln_sums = _ln_silu_bwd(du3, sv["u1"], vec(small_w["g_conv_ln"][l]), vec(small_w["b_conv_ln"][l]))
        dcv, dcg, conv_sums = _glu_conv_bwd(du1, sv["z"], small_w["w_conv_dw"], l, D)
        dos, dterms = _attn_bwd_prep(dattn, sv["outs"], sv["lses"])
        dqs, dks, dvs = [], [], []
        for gi in range(len(DILATIONS)):
            dq_g, dk_g, dv_g = _attn_bwd(sv["qs"][gi], sv["ks"][gi], sv["vs"][gi], dos[gi], sv["lses"][gi],
                                         dterms[gi], gi)
            dqs.append(dq_g)
            dks.append(dk_g)
            dvs.append(dv_g)
        dq, dk, dv, qk_sums = _qk_prep_bwd(dqs, dks, dvs, sv["z"], gq, gk, tabs)
        dz = jnp.concatenate([dq, dk, dv, dcv, dcg, dga, dgb], axis=1)
        wgrad("w_in", sv["h"], dz, l)
        dh = _mm_nt(dz, wf["w_in"], l, "in_proj_bwd")
        dx, n1_sums = _norm_bwd(dh, sv["x"], dx2, g1, sc1, "norm1_bwd")

        dmods[l] = jnp.concatenate([n1_sums[0], n1_sums[1], dgt1[0], n2_sums[0], n2_sums[1], dgt2[0]])
        small_g["b_ada"][l] = dmods[l]
        small_g["g_norm1"][l] = n1_sums[2]
        small_g["g_q"][l] = qk_sums[0]
        small_g["g_k"][l] = qk_sums[1]
        small_g["w_conv_dw"][l] = conv_sums[:CONV_K]
        small_g["b_conv_dw"][l] = conv_sums[CONV_K]
        small_g["g_conv_ln"][l] = ln_sums[0]
        small_g["b_conv_ln"][l] = ln_sums[1]
        small_g["g_norm2"][l] = n2_sums[2]
        small_g["w_ffn_dw"][l] = ffn_sums[:FFN_K]
        small_g["b_ffn_dw"][l] = ffn_sums[FFN_K]

    small_g = {n: jnp.stack(v) for n, v in small_g.items()}
    return loss, dx, bufs, small_g, cact


def kernel(x, c, positions, w_ada, b_ada, g_norm1, w_in, g_q, g_k, w_attn_proj, w_conv_dw, b_conv_dw, g_conv_ln, b_conv_ln, w_conv_out, w_o, g_norm2, w_ffn_in, w_ffn_dw, b_ffn_dw, w_ffn_down, loss_target, m_w_ada, m_b_ada, m_g_norm1, m_w_in, m_g_q, m_g_k, m_w_attn_proj, m_w_conv_dw, m_b_conv_dw, m_g_conv_ln, m_b_conv_ln, m_w_conv_out, m_w_o, m_g_norm2, m_w_ffn_in, m_w_ffn_dw, m_b_ffn_dw, m_w_ffn_down, v_w_ada, v_b_ada, v_g_norm1, v_w_in, v_g_q, v_g_k, v_w_attn_proj, v_w_conv_dw, v_b_conv_dw, v_g_conv_ln, v_b_conv_ln, v_w_conv_out, v_w_o, v_g_norm2, v_w_ffn_in, v_w_ffn_dw, v_b_ffn_dw, v_w_ffn_down):
    w = dict(w_ada=w_ada, b_ada=b_ada, g_norm1=g_norm1, w_in=w_in, g_q=g_q, g_k=g_k, w_attn_proj=w_attn_proj,
             w_conv_dw=w_conv_dw, b_conv_dw=b_conv_dw, g_conv_ln=g_conv_ln, b_conv_ln=b_conv_ln,
             w_conv_out=w_conv_out, w_o=w_o, g_norm2=g_norm2, w_ffn_in=w_ffn_in, w_ffn_dw=w_ffn_dw,
             b_ffn_dw=b_ffn_dw, w_ffn_down=w_ffn_down)
    m = dict(w_ada=m_w_ada, b_ada=m_b_ada, g_norm1=m_g_norm1, w_in=m_w_in, g_q=m_g_q, g_k=m_g_k,
             w_attn_proj=m_w_attn_proj, w_conv_dw=m_w_conv_dw, b_conv_dw=m_b_conv_dw, g_conv_ln=m_g_conv_ln,
             b_conv_ln=m_b_conv_ln, w_conv_out=m_w_conv_out, w_o=m_w_o, g_norm2=m_g_norm2, w_ffn_in=m_w_ffn_in,
             w_ffn_dw=m_w_ffn_dw, b_ffn_dw=m_b_ffn_dw, w_ffn_down=m_w_ffn_down)
    v = dict(w_ada=v_w_ada, b_ada=v_b_ada, g_norm1=v_g_norm1, w_in=v_w_in, g_q=v_g_q, g_k=v_g_k,
             w_attn_proj=v_w_attn_proj, w_conv_dw=v_w_conv_dw, b_conv_dw=v_b_conv_dw, g_conv_ln=v_g_conv_ln,
             b_conv_ln=v_b_conv_ln, w_conv_out=v_w_conv_out, w_o=v_w_o, g_norm2=v_g_norm2, w_ffn_in=v_w_ffn_in,
             w_ffn_dw=v_w_ffn_dw, b_ffn_dw=v_b_ffn_dw, w_ffn_down=v_w_ffn_down)
    xi, yi, ci = lax.axis_index("x"), lax.axis_index("y"), lax.axis_index("c")
    j = 2 * xi + yi
    L = w_ada.shape[0]
    D = x.shape[-1]

    mats = ("w_ada",) + BIG
    col = dict(COL_SHARDED, w_ada=True)
    shards = [w[n].astype(BF16) for n in mats]
    conv_shapes = [w["w_conv_dw"].shape, w["w_ffn_dw"].shape]
    gathered, small_g4 = _all_gather_weights(shards, _pack_rows([w["w_conv_dw"], w["w_ffn_dw"]]))
    wf = {}
    for n, g4 in zip(mats, gathered):
        _, _, r, cols = g4.shape
        if col[n]:
            wf[n] = g4.transpose(1, 2, 0, 3).reshape(L, r, 4 * cols)
        else:
            wf[n] = g4.transpose(1, 0, 2, 3).reshape(L, 4 * r, cols)
    conv_parts = [_unpack_rows(small_g4[k], conv_shapes) for k in range(4)]
    small_w = {n: w[n] for n in SMALL}
    small_w["w_conv_dw"] = jnp.concatenate([p[0] for p in conv_parts], axis=2)
    small_w["w_ffn_dw"] = jnp.concatenate([p[1] for p in conv_parts], axis=2)

    loss, dx, bufs, small_g, cact = _local_step(x[0], c, positions[0], loss_target[0], wf, small_w)
    loss = lax.psum(loss, ("x", "y", "c"))

    dws = [bufs[n] for n in BIG]
    land1 = _rs_to_sibling(dws)
    c_idx = jnp.reshape(ci, (1,)).astype(jnp.int32)
    s1s = [_add_halves(a, b, c_idx, "rs_add_halves") for a, b in zip(dws, land1)]
    land2 = _rs_to_chips(s1s)
    cj_idx = jnp.stack([ci, j]).astype(jnp.int32)
    halves = [_add_chips(a, b, e, cj_idx, "rs_add_chips") for a, b, e in zip(dws, land1, land2)]
    joined = _rs_join_halves(halves)
    grads = {n: g.reshape(w[n].shape) for n, g in zip(BIG, joined)}

    part = _pack_rows([small_g[n] for n in SMALL] + [cact])
    parts = _all_gather_small(part)
    full_shapes = [small_g[n].shape for n in SMALL]
    summed = _unpack_rows(_sum_parts(parts), full_shapes)
    for n, g in zip(SMALL, summed):
        if n in ("w_conv_dw", "w_ffn_dw"):
            cols = w[n].shape[2]
            g = lax.dynamic_slice_in_dim(g, j * cols, cols, axis=2)
        grads[n] = g
    flat = parts.reshape(8, -1)
    off, dmod_all = 0, None
    for n, shp in zip(SMALL, full_shapes):
        size = 1
        for s in shp:
            size *= s
        if n == "b_ada":
            dmod_all = flat[:, off:off + size].reshape(8, L, 6 * D)
        off += size
    cact_all = flat[:, off:off + D]
    cols = w_ada.shape[2]
    dmod_cols = lax.dynamic_slice_in_dim(dmod_all, j * cols, cols, axis=2).transpose(1, 0, 2)
    grads["w_ada"] = _wada_grad(cact_all.T, dmod_cols)

    delta, new_m, new_v = {}, {}, {}
    for n in ("w_ada",) + BIG:
        shp = w[n].shape
        two_d = lambda a: a.reshape(shp[0] * shp[1], shp[2])
        dl, mn, vn = _adamw(two_d(w[n]), two_d(grads[n]), two_d(m[n]), two_d(v[n]), "adamw_" + n)
        delta[n], new_m[n], new_v[n] = dl.reshape(shp), mn.reshape(shp), vn.reshape(shp)
    small_shapes = [w[n].shape for n in SMALL]
    packs = [_pack_rows([d[n] for n in SMALL]) for d in (w, grads, m, v)]
    outs = _adamw(*packs, "adamw_small")
    for d, packed in zip((delta, new_m, new_v), outs):
        for n, a in zip(SMALL, _unpack_rows(packed, small_shapes)):
            d[n] = a

    return (loss, dx[None], *[grads[n] for n in WEIGHTS], *[delta[n] for n in WEIGHTS],
            *[new_m[n] for n in WEIGHTS], *[new_v[n] for n in WEIGHTS])
```

```python
import functools

import jax
import jax.numpy as jnp
from jax import lax
from jax.experimental import pallas as pl
from jax.experimental.pallas import tpu as pltpu

F32 = jnp.float32
BF16 = jnp.bfloat16
SDS = jax.ShapeDtypeStruct
MESH = pl.DeviceIdType.MESH

HEAD_DIM = 128
BLOCK = 128
HEADS_PER_GROUP = 4
GROUP_W = HEADS_PER_GROUP * HEAD_DIM
DILATIONS = (1, 4, 16)
ATTN_W = len(DILATIONS) * GROUP_W
ROT_DIM = HEAD_DIM // 4
ROPE_THETA = 500000.0
CONV_K = 31
CONV_HALO = 32
FFN_K = 3
FFN_HALO = 8
EPS = 1e-6
NEG = -1e30
CHUNK = 512

ADAM_LR = 0.001
ADAM_B1 = 0.9
ADAM_B2 = 0.999
ADAM_EPS = 1e-08
ADAM_WD = 0.01
ADAM_STEP = 10

VMEM_LIMIT_BYTES = 48 * 1024 * 1024
TILE_BYTES = 2 * 1024 * 1024

NT_DIMS = (((1,), (1,)), ((), ()))
TN_DIMS = (((0,), (0,)), ((), ()))


def _cp(*sem):
    return pltpu.CompilerParams(dimension_semantics=sem if sem else None,
                                vmem_limit_bytes=VMEM_LIMIT_BYTES)


def _tm(s, cap=512):
    return min(cap, s)


def _row_tile(rows, cols, itemsize=4, step=8):
    best = None
    for t in range(step, rows + 1, step):
        if rows % t == 0 and t * cols * itemsize <= TILE_BYTES:
            best = t
    return best if best is not None else rows


def _quarter(n):
    return n // 4 if n % (4 * HEAD_DIM) == 0 else CHUNK


def _sigmoid(v):
    return jax.nn.sigmoid(v)


def _mod_fwd(c, w_ada, b_ada):
    L, D, N6 = w_ada.shape
    tn = N6 // 4

    def body(c_ref, w_ref, b_ref, mod_ref, cact_ref):
        cv = c_ref[...]
        ca = cv * _sigmoid(cv)
        cact_ref[...] = ca
        a8 = jnp.broadcast_to(ca, (8, D)).astype(BF16)
        acc = jnp.dot(a8, w_ref[0], preferred_element_type=F32)
        mod_ref[0] = acc[0:1, :] + b_ref[0]

    mod, cact = pl.pallas_call(
        body, name="mod_fwd", grid=(L, 4),
        in_specs=[pl.BlockSpec((1, D), lambda l, j: (0, 0)),
                  pl.BlockSpec((1, D, tn), lambda l, j: (l, 0, j)),
                  pl.BlockSpec((1, 1, tn), lambda l, j: (l, 0, j))],
        out_specs=[pl.BlockSpec((1, 1, tn), lambda l, j: (l, 0, j)),
                   pl.BlockSpec((1, D), lambda l, j: (0, 0))],
        out_shape=[SDS((L, 1, N6), F32), SDS((1, D), F32)],
        compiler_params=_cp("arbitrary", "arbitrary"),
    )(c, w_ada, b_ada.reshape(L, 1, N6))
    return mod.reshape(L, N6), cact


def _normmod_mm(x, g, sc, sh, w, l, name):
    S, D = x.shape
    N = w.shape[2]
    tm, tn = _tm(S, 1024), _quarter(N)

    def body(x_ref, g_ref, sc_ref, sh_ref, w_ref, z_ref, h_ref, hs):
        @pl.when(pl.program_id(1) == 0)
        def _():
            xv = x_ref[...]
            r = lax.rsqrt(jnp.mean(xv * xv, axis=-1, keepdims=True) + EPS)
            hv = (xv * r) * g_ref[...] * (1.0 + sc_ref[...]) + sh_ref[...]
            hs[...] = hv.astype(BF16)
            h_ref[...] = hs[...]
        z_ref[...] = jnp.dot(hs[...], w_ref[0], preferred_element_type=F32)

    vec = pl.BlockSpec((1, D), lambda i, j: (0, 0))
    return pl.pallas_call(
        body, name=name, grid=(S // tm, N // tn),
        in_specs=[pl.BlockSpec((tm, D), lambda i, j: (i, 0)), vec, vec, vec,
                  pl.BlockSpec((1, D, tn), lambda i, j: (l, 0, j))],
        out_specs=[pl.BlockSpec((tm, tn), lambda i, j: (i, j)),
                   pl.BlockSpec((tm, D), lambda i, j: (i, 0))],
        out_shape=[SDS((S, N), F32), SDS((S, D), BF16)],
        scratch_shapes=[pltpu.VMEM((tm, D), BF16)],
        compiler_params=_cp("parallel", "arbitrary"),
    )(x, g, sc, sh, w)


def _rope(t, cos_t, sa_t, sb_t):
    return t * cos_t + pltpu.roll(t, HEAD_DIM - ROT_DIM // 2, 1) * sa_t + pltpu.roll(t, ROT_DIM // 2, 1) * sb_t


def _unrope(d, cos_t, sa_t, sb_t):
    return d * cos_t + pltpu.roll(d * sa_t, ROT_DIM // 2, 1) + pltpu.roll(d * sb_t, HEAD_DIM - ROT_DIM // 2, 1)


def _dil_spec(d, tm):
    return pl.BlockSpec((d, tm // d, GROUP_W), lambda i: (0, i, 0))


def _to_dilated(scr, val, dst_ref, d, cols):
    if d == 1:
        dst_ref[0, :, cols] = val.astype(dst_ref.dtype)
        return
    n = val.shape[0] // d
    scr[...] = val
    for r in range(d):
        dst_ref[r, :, cols] = scr[pl.ds(r, n, stride=d), :].astype(dst_ref.dtype)


def _from_dilated(scr, src_ref, d, cols):
    if d == 1:
        return src_ref[0, :, cols].astype(F32)
    n = src_ref.shape[1]
    for r in range(d):
        scr[pl.ds(r, n, stride=d), :] = src_ref[r, :, cols].astype(F32)
    return scr[...]


def _qk_prep(z, g_q, g_k, tabs):
    S = z.shape[0]
    tm = _tm(S)
    n_heads = ATTN_W // HEAD_DIM
    ng = len(DILATIONS)

    def body(q_ref, k_ref, v_ref, gq_ref, gk_ref, c_ref, sa_ref, sb_ref, *rest):
        outs, scr = rest[:3 * ng], rest[3 * ng]
        cos_t, sa_t, sb_t = c_ref[...], sa_ref[...], sb_ref[...]
        for wi, (src, gref) in enumerate(((q_ref, gq_ref), (k_ref, gk_ref), (v_ref, None))):
            for h in range(n_heads):
                gi = h // HEADS_PER_GROUP
                cols = slice((h % HEADS_PER_GROUP) * HEAD_DIM, (h % HEADS_PER_GROUP + 1) * HEAD_DIM)
                t = src[:, h * HEAD_DIM:(h + 1) * HEAD_DIM]
                if gref is not None:
                    r = lax.rsqrt(jnp.mean(t * t, axis=-1, keepdims=True) + EPS)
                    t = _rope((t * r) * gref[...], cos_t, sa_t, sb_t)
                _to_dilated(scr, t, outs[wi * ng + gi], DILATIONS[gi], cols)

    tab = pl.BlockSpec((tm, HEAD_DIM), lambda i: (i, 0))
    gsp = pl.BlockSpec((1, HEAD_DIM), lambda i: (0, 0))
    wide = lambda cb: pl.BlockSpec((tm, ATTN_W), lambda i: (i, cb))
    out = pl.pallas_call(
        body, name="qk_prep", grid=(S // tm,),
        in_specs=[wide(0), wide(1), wide(2), gsp, gsp, tab, tab, tab],
        out_specs=[_dil_spec(d, tm) for _ in range(3) for d in DILATIONS],
        out_shape=[SDS((d, S // d, GROUP_W), BF16) for _ in range(3) for d in DILATIONS],
        scratch_shapes=[pltpu.VMEM((tm, HEAD_DIM), F32)],
        compiler_params=_cp("parallel"),
    )(z, z, z, g_q, g_k, *tabs)
    return out[:ng], out[ng:2 * ng], out[2 * ng:]


def _attn_blk(nb, shift):
    if shift < 0:
        return pl.BlockSpec((None, BLOCK, GROUP_W), lambda r, n: (r, jnp.maximum(n - 1, 0), 0))
    if shift > 0:
        return pl.BlockSpec((None, BLOCK, GROUP_W), lambda r, n: (r, jnp.minimum(n + 1, nb - 1), 0))
    return pl.BlockSpec((None, BLOCK, GROUP_W), lambda r, n: (r, n, 0))


def _band_mask(has_prev):
    qi = lax.broadcasted_iota(jnp.int32, (BLOCK, 2 * BLOCK), 0)
    kj = lax.broadcasted_iota(jnp.int32, (BLOCK, 2 * BLOCK), 1)
    band = jnp.logical_and(kj >= qi, kj <= qi + BLOCK)
    return jnp.logical_and(band, jnp.logical_or(has_prev, kj >= BLOCK))


def _attn_fwd(q, k, v, gi):
    d, Sd, _ = q.shape
    nb = Sd // BLOCK
    scale = HEAD_DIM ** -0.5

    def body(q_ref, kc_ref, kp_ref, vc_ref, vp_ref, o_ref, lse_ref, kk, vv):
        n = pl.program_id(1)
        kk[0:BLOCK, :], kk[BLOCK:, :] = kp_ref[...], kc_ref[...]
        vv[0:BLOCK, :], vv[BLOCK:, :] = vp_ref[...], vc_ref[...]
        mask = _band_mask(n > 0)
        for h in range(HEADS_PER_GROUP):
            sl = slice(h * HEAD_DIM, (h + 1) * HEAD_DIM)
            s = lax.dot_general(q_ref[:, sl], kk[:, sl], NT_DIMS, preferred_element_type=F32) * scale
            s = jnp.where(mask, s, NEG)
            m = jnp.max(s, axis=-1, keepdims=True)
            p = jnp.exp(s - m)
            den = jnp.sum(p, axis=-1, keepdims=True)
            acc = jnp.dot(p.astype(BF16), vv[:, sl], preferred_element_type=F32)
            o_ref[:, sl] = acc / den
            lse_ref[:, sl] = jnp.broadcast_to(m + jnp.log(den), (BLOCK, HEAD_DIM))

    cur, prev = _attn_blk(nb, 0), _attn_blk(nb, -1)
    return pl.pallas_call(
        body, name=f"attn_fwd_g{gi}", grid=(d, nb),
        in_specs=[cur, cur, prev, cur, prev], out_specs=[cur] * 2,
        out_shape=[SDS((d, Sd, GROUP_W), F32)] * 2,
        scratch_shapes=[pltpu.VMEM((2 * BLOCK, GROUP_W), BF16)] * 2,
        compiler_params=_cp("parallel", "parallel"),
    )(q, k, k, v, v)


def _group_weights(lses):
    m = jnp.maximum(jnp.maximum(lses[0], lses[1]), lses[2])
    es = [jnp.exp(v - m) for v in lses]
    inv = 1.0 / (es[0] + es[1] + es[2])
    return [e * inv for e in es]


def _attn_combine(outs, lses):
    S = outs[0].shape[0] * outs[0].shape[1]
    tm = _tm(S)
    ng = len(DILATIONS)

    def body(*refs):
        o_refs, l_refs, a_ref, scr = refs[:ng], refs[ng:2 * ng], refs[2 * ng], refs[2 * ng + 1:]
        for h in range(HEADS_PER_GROUP):
            cols = slice(h * HEAD_DIM, (h + 1) * HEAD_DIM)
            ov = [_from_dilated(scr[g], o_refs[g], DILATIONS[g], cols) for g in range(ng)]
            lv = [_from_dilated(scr[ng + g], l_refs[g], DILATIONS[g], cols) for g in range(ng)]
            w = _group_weights(lv)
            a_ref[:, cols] = (w[0] * ov[0] + w[1] * ov[1] + w[2] * ov[2]).astype(BF16)

    dil = [_dil_spec(d, tm) for d in DILATIONS]
    return pl.pallas_call(
        body, name="attn_combine", grid=(S // tm,), in_specs=dil + dil,
        out_specs=pl.BlockSpec((tm, GROUP_W), lambda i: (i, 0)),
        out_shape=SDS((S, GROUP_W), BF16),
        scratch_shapes=[pltpu.VMEM((tm, HEAD_DIM), F32)] * (2 * ng),
        compiler_params=_cp("parallel"),
    )(*outs, *lses)


def _mm_nn(a, w, l, name, res=None, gate=None):
    S, K = a.shape
    N = w.shape[2]
    tm = _tm(S)
    gated = res is not None

    def body(*refs):
        if gated:
            a_ref, w_ref, r_ref, g_ref, t_ref, o_ref = refs
        else:
            a_ref, w_ref, t_ref = refs
        t = jnp.dot(a_ref[...], w_ref[0], preferred_element_type=F32)
        t_ref[...] = t
        if gated:
            o_ref[...] = r_ref[...] + g_ref[...] * t

    row = pl.BlockSpec((tm, N), lambda i: (i, 0))
    in_specs = [pl.BlockSpec((tm, K), lambda i: (i, 0)), pl.BlockSpec((1, K, N), lambda i: (l, 0, 0))]
    args = [a, w]
    if gated:
        in_specs += [row, pl.BlockSpec((1, N), lambda i: (0, 0))]
        args += [res, gate]
    n_out = 2 if gated else 1
    out = pl.pallas_call(
        body, name=name, grid=(S // tm,), in_specs=in_specs, out_specs=[row] * n_out,
        out_shape=[SDS((S, N), F32)] * n_out, compiler_params=_cp("parallel"),
    )(*args)
    return out if gated else out[0]


def _glu(cv, cg):
    return cv * _sigmoid(cg)


STRIP = 16
SHIFTS = 8


def _fill_shifted(sh, ext, rows):
    sh[0] = ext[...]
    for b in range(1, SHIFTS):
        sh[b, 0:rows, :] = ext[pl.ds(b, rows), :]


def _tap(sh, base, off, rows=STRIP):
    return sh[off % SHIFTS, pl.ds(base + SHIFTS * (off // SHIFTS), rows), :]


def _glu_conv(z, w_dw, b_dw, l, D):
    S = z.shape[0]
    tm = _tm(S)
    cc0 = 3 * ATTN_W // CHUNK
    ncc = D // CHUNK
    hb = tm // CONV_HALO
    lead = CONV_HALO - (CONV_K - 1)

    def body(cv_ref, cg_ref, hv_ref, hg_ref, w_ref, b_ref, o_ref, ext, sh):
        i = pl.program_id(1)
        ext[0:CONV_HALO, :] = jnp.where(i > 0, _glu(hv_ref[...], hg_ref[...]), 0.0)
        ext[CONV_HALO:, :] = _glu(cv_ref[...], cg_ref[...])
        _fill_shifted(sh, ext, tm + CONV_HALO - SHIFTS)

        def strip(s, carry):
            base = pl.multiple_of(s * STRIP, STRIP)
            acc = jnp.broadcast_to(b_ref[...], (STRIP, CHUNK))
            for k in range(CONV_K):
                acc = acc + w_ref[0, k:k + 1, :] * _tap(sh, base, lead + k)
            o_ref[pl.ds(base, STRIP), :] = acc
            return carry

        lax.fori_loop(0, tm // STRIP, strip, 0)

    cur = lambda off: pl.BlockSpec((tm, CHUNK), lambda cc, i: (i, cc0 + off + cc))
    halo = lambda off: pl.BlockSpec((CONV_HALO, CHUNK), lambda cc, i: (jnp.maximum(i * hb - 1, 0), cc0 + off + cc))
    return pl.pallas_call(
        body, name="glu_conv", grid=(ncc, S // tm),
        in_specs=[cur(0), cur(ncc), halo(0), halo(ncc),
                  pl.BlockSpec((1, CONV_K, CHUNK), lambda cc, i: (l, 0, cc)),
                  pl.BlockSpec((None, 1, CHUNK), lambda cc, i: (l, 0, cc))],
        out_specs=pl.BlockSpec((tm, CHUNK), lambda cc, i: (i, cc)),
        out_shape=SDS((S, D), F32),
        scratch_shapes=[pltpu.VMEM((tm + CONV_HALO, CHUNK), F32),
                        pltpu.VMEM((SHIFTS, tm + CONV_HALO, CHUNK), F32)],
        compiler_params=_cp("parallel", "parallel"),
    )(z, z, z, z, w_dw, b_dw[:, None, :])


def _ln_stats(u):
    mu = jnp.mean(u, axis=-1, keepdims=True)
    xc = u - mu
    rstd = lax.rsqrt(jnp.mean(xc * xc, axis=-1, keepdims=True) + EPS)
    return xc * rstd, rstd


def _ln_silu(u1, g, b):
    S, D = u1.shape
    tm = _tm(S)

    def body(u_ref, g_ref, b_ref, o_ref):
        xh, _ = _ln_stats(u_ref[...])
        yv = xh * g_ref[...] + b_ref[...]
        o_ref[...] = (yv * _sigmoid(yv)).astype(BF16)

    vec = pl.BlockSpec((1, D), lambda i: (0, 0))
    return pl.pallas_call(
        body, name="ln_silu", grid=(S // tm,),
        in_specs=[pl.BlockSpec((tm, D), lambda i: (i, 0)), vec, vec],
        out_specs=pl.BlockSpec((tm, D), lambda i: (i, 0)),
        out_shape=SDS((S, D), BF16), compiler_params=_cp("parallel"),
    )(u1, g, b)


def _merge(z, y_a, y_b, D):
    S = z.shape[0]
    tm = _tm(S)
    ncc = D // CHUNK
    ga0 = (3 * ATTN_W + 2 * D) // CHUNK

    def body(ga_ref, gb_ref, ya_ref, yb_ref, o_ref):
        o_ref[...] = (_sigmoid(ga_ref[...]) * ya_ref[...] + _sigmoid(gb_ref[...]) * yb_ref[...]).astype(BF16)

    ysp = pl.BlockSpec((tm, CHUNK), lambda i, cc: (i, cc))
    return pl.pallas_call(
        body, name="merge", grid=(S // tm, ncc),
        in_specs=[pl.BlockSpec((tm, CHUNK), lambda i, cc: (i, ga0 + cc)),
                  pl.BlockSpec((tm, CHUNK), lambda i, cc: (i, ga0 + ncc + cc)), ysp, ysp],
        out_specs=ysp, out_shape=SDS((S, D), BF16), compiler_params=_cp("parallel", "parallel"),
    )(z, z, y_a, y_b)


def _ffn_chunk(F):
    best = 128
    for t in range(128, min(F, 1536) + 1, 128):
        if F % t == 0:
            best = t
    return best


def _ffn_conv(ext, w_ref, b_ref, tm):
    gp = jnp.broadcast_to(b_ref[...], (tm, ext.shape[1]))
    for k in range(FFN_K):
        gp = gp + w_ref[0, k:k + 1, :] * ext[pl.ds(FFN_HALO - (FFN_K - 1) + k, tm), :]
    return gp


def _ffn_act(gu, w_dw, b_dw, l):
    S, F2 = gu.shape
    F = F2 // 2
    cw = _ffn_chunk(F)
    ncc = F // cw
    tm = _tm(S)
    hb = tm // FFN_HALO

    def body(g_ref, u_ref, h_ref, w_ref, b_ref, o_ref, ext):
        i = pl.program_id(1)
        ext[0:FFN_HALO, :] = jnp.where(i > 0, h_ref[...], 0.0)
        ext[FFN_HALO:, :] = g_ref[...]
        gp = _ffn_conv(ext, w_ref, b_ref, tm)
        o_ref[...] = (gp * _sigmoid(gp) * u_ref[...]).astype(BF16)

    return pl.pallas_call(
        body, name="ffn_act", grid=(ncc, S // tm),
        in_specs=[pl.BlockSpec((tm, cw), lambda cc, i: (i, cc)),
                  pl.BlockSpec((tm, cw), lambda cc, i: (i, ncc + cc)),
                  pl.BlockSpec((FFN_HALO, cw), lambda cc, i: (jnp.maximum(i * hb - 1, 0), cc)),
                  pl.BlockSpec((1, FFN_K, cw), lambda cc, i: (l, 0, cc)),
                  pl.BlockSpec((None, 1, cw), lambda cc, i: (l, 0, cc))],
        out_specs=pl.BlockSpec((tm, cw), lambda cc, i: (i, cc)),
        out_shape=SDS((S, F), BF16),
        scratch_shapes=[pltpu.VMEM((tm + FFN_HALO, cw), F32)],
        compiler_params=_cp("parallel", "parallel"),
    )(gu, gu, gu, w_dw, b_dw[:, None, :])


def _loss_grad(y, target):
    S, D = y.shape
    tm = _tm(S)

    def body(y_ref, t_ref, dy_ref, l_ref):
        @pl.when(pl.program_id(0) == 0)
        def _():
            l_ref[...] = jnp.zeros_like(l_ref)
        e = y_ref[...] - t_ref[...]
        dy_ref[...] = e * (1.0 / D)
        l_ref[...] += jnp.broadcast_to(0.5 * jnp.sum(jnp.mean(e * e, axis=-1, keepdims=True)), (1, HEAD_DIM))

    row = pl.BlockSpec((tm, D), lambda i: (i, 0))
    dy, lsum = pl.pallas_call(
        body, name="loss_grad", grid=(S // tm,), in_specs=[row, row],
        out_specs=[row, pl.BlockSpec((1, HEAD_DIM), lambda i: (0, 0))],
        out_shape=[SDS((S, D), F32), SDS((1, HEAD_DIM), F32)],
        compiler_params=_cp("arbitrary"),
    )(y, target)
    return dy, lsum[0, 0]


def _gate_bwd(dx, t, gate, name):
    S, D = dx.shape
    tm = _tm(S)

    def body(dx_ref, t_ref, g_ref, e_ref, dg_ref):
        @pl.when(pl.program_id(0) == 0)
        def _():
            dg_ref[...] = jnp.zeros_like(dg_ref)
        dxv = dx_ref[...]
        e_ref[...] = (dxv * g_ref[...]).astype(BF16)
        dg_ref[...] += jnp.sum(dxv * t_ref[...], axis=0, keepdims=True)

    row = pl.BlockSpec((tm, D), lambda i: (i, 0))
    vec = pl.BlockSpec((1, D), lambda i: (0, 0))
    return pl.pallas_call(
        body, name=name, grid=(S // tm,), in_specs=[row, row, vec], out_specs=[row, vec],
        out_shape=[SDS((S, D), BF16), SDS((1, D), F32)], compiler_params=_cp("arbitrary"),
    )(dx, t, gate)


def _mm_nt(dy, w, l, name):
    S, C = dy.shape
    K = w.shape[1]
    tm = _tm(S, 1024 if K <= 1024 else 512)
    tc = _quarter(C) if C > 1024 else C
    nc = C // tc

    def body(dy_ref, w_ref, o_ref, acc):
        kk = pl.program_id(1)

        @pl.when(kk == 0)
        def _():
            acc[...] = jnp.zeros_like(acc)
        acc[...] += lax.dot_general(dy_ref[...], w_ref[0], NT_DIMS, preferred_element_type=F32)

        @pl.when(kk == nc - 1)
        def _():
            o_ref[...] = acc[...]

    return pl.pallas_call(
        body, name=name, grid=(S // tm, nc),
        in_specs=[pl.BlockSpec((tm, tc), lambda i, kk: (i, kk)),
                  pl.BlockSpec((1, K, tc), lambda i, kk: (l, 0, kk))],
        out_specs=pl.BlockSpec((tm, K), lambda i, kk: (i, 0)),
        out_shape=SDS((S, K), F32),
        scratch_shapes=[pltpu.VMEM((tm, K), F32)],
        compiler_params=_cp("parallel", "arbitrary"),
    )(dy, w)


def _mm_tn(a, b, buf, l, n_layers, col_sharded, name):
    S, K = a.shape
    N = b.shape[1]
    ts = _tm(S, 1024)
    ns = S // ts
    if col_sharded:
        cols, r2 = N // 4, K // 2
        grid = (4, ns)

        def body(*refs):
            a_ref, b_ref, o_ref = refs[0], refs[1], refs[-1]
            s = pl.program_id(1)

            @pl.when(s == 0)
            def _():
                o_ref[...] = jnp.zeros_like(o_ref)
            for hh in range(2):
                o_ref[hh] += lax.dot_general(a_ref[:, hh * r2:(hh + 1) * r2], b_ref[...], TN_DIMS,
                                             preferred_element_type=F32)

        in_specs = [pl.BlockSpec((ts, K), lambda jj, s: (s, 0)), pl.BlockSpec((ts, cols), lambda jj, s: (s, jj))]
        out_spec = pl.BlockSpec((None, None, 2, r2, cols), lambda jj, s: (l, jj, 0, 0, 0))
        scratch = []
    else:
        cols, r2 = N, K // 8
        tn = CHUNK if N % CHUNK == 0 else N
        grid = (N // tn, ns)

        def body(*refs):
            a_ref, b_ref, o_ref, acc = refs[0], refs[1], refs[-2], refs[-1]
            s = pl.program_id(1)

            @pl.when(s == 0)
            def _():
                acc[...] = jnp.zeros_like(acc)
            acc[...] += lax.dot_general(a_ref[...], b_ref[...], TN_DIMS, preferred_element_type=F32)

            @pl.when(s == ns - 1)
            def _():
                for jj in range(4):
                    for hh in range(2):
                        o_ref[jj, hh] = acc[(2 * jj + hh) * r2:(2 * jj + hh + 1) * r2, :]

        in_specs = [pl.BlockSpec((ts, K), lambda nb, s: (s, 0)), pl.BlockSpec((ts, tn), lambda nb, s: (s, nb))]
        out_spec = pl.BlockSpec((None, 4, 2, r2, tn), lambda nb, s: (l, 0, 0, 0, nb))
        scratch = [pltpu.VMEM((K, tn), F32)]
    args = [a, b]
    aliases = {}
    if buf is not None:
        in_specs = in_specs + [pl.BlockSpec(memory_space=pl.ANY)]
        args.append(buf)
        aliases = {2: 0}
    return pl.pallas_call(
        body, name=name, grid=grid, in_specs=in_specs, out_specs=out_spec,
        out_shape=SDS((n_layers, 4, 2, r2, cols), F32), scratch_shapes=scratch,
        input_output_aliases=aliases, compiler_params=_cp("parallel", "arbitrary"),
    )(*args)


def _norm_bwd(dh, x, dres, g, sc, name):
    S, D = x.shape
    tm = _tm(S)

    def body(dh_ref, x_ref, dr_ref, g_ref, sc_ref, dx_ref, sums_ref):
        @pl.when(pl.program_id(0) == 0)
        def _():
            sums_ref[...] = jnp.zeros_like(sums_ref)
        xv, dhv = x_ref[...], dh_ref[...]
        r = lax.rsqrt(jnp.mean(xv * xv, axis=-1, keepdims=True) + EPS)
        xn = xv * r
        gv, sv = g_ref[...], 1.0 + sc_ref[...]
        dxn = dhv * gv * sv
        dx_ref[...] = dr_ref[...] + r * (dxn - xn * jnp.mean(dxn * xn, axis=-1, keepdims=True))
        dhxn = dhv * xn
        sums_ref[0:1, :] += jnp.sum(dhv, axis=0, keepdims=True)
        sums_ref[1:2, :] += jnp.sum(dhxn * gv, axis=0, keepdims=True)
        sums_ref[2:3, :] += jnp.sum(dhxn * sv, axis=0, keepdims=True)

    row = pl.BlockSpec((tm, D), lambda i: (i, 0))
    vec = pl.BlockSpec((1, D), lambda i: (0, 0))
    return pl.pallas_call(
        body, name=name, grid=(S // tm,), in_specs=[row, row, row, vec, vec],
        out_specs=[row, pl.BlockSpec((8, D), lambda i: (0, 0))],
        out_shape=[SDS((S, D), F32), SDS((8, D), F32)], compiler_params=_cp("arbitrary"),
    )(dh, x, dres, g, sc)


def _ffn_act_bwd1(da, gu, w_dw, b_dw, l):
    S, F2 = gu.shape
    F = F2 // 2
    cw = _ffn_chunk(F)
    ncc = F // cw
    tm = _tm(S)
    hb = tm // FFN_HALO

    def body(da_ref, g_ref, u_ref, h_ref, w_ref, b_ref, dgp_ref, du_ref, ext):
        i = pl.program_id(1)
        ext[0:FFN_HALO, :] = jnp.where(i > 0, h_ref[...], 0.0)
        ext[FFN_HALO:, :] = g_ref[...]
        gp = _ffn_conv(ext, w_ref, b_ref, tm)
        sg = _sigmoid(gp)
        dav = da_ref[...]
        du_ref[...] = (dav * gp * sg).astype(BF16)
        dgp_ref[...] = dav * u_ref[...] * (sg * (1.0 + gp * (1.0 - sg)))

    return pl.pallas_call(
        body, name="ffn_act_bwd1", grid=(ncc, S // tm),
        in_specs=[pl.BlockSpec((tm, cw), lambda cc, i: (i, cc)),
                  pl.BlockSpec((tm, cw), lambda cc, i: (i, cc)),
                  pl.BlockSpec((tm, cw), lambda cc, i: (i, ncc + cc)),
                  pl.BlockSpec((FFN_HALO, cw), lambda cc, i: (jnp.maximum(i * hb - 1, 0), cc)),
                  pl.BlockSpec((1, FFN_K, cw), lambda cc, i: (l, 0, cc)),
                  pl.BlockSpec((None, 1, cw), lambda cc, i: (l, 0, cc))],
        out_specs=[pl.BlockSpec((tm, cw), lambda cc, i: (i, cc)),
                   pl.BlockSpec((tm, cw), lambda cc, i: (i, ncc + cc))],
        out_shape=[SDS((S, F), F32), SDS((S, F2), BF16)],
        scratch_shapes=[pltpu.VMEM((tm + FFN_HALO, cw), F32)],
        compiler_params=_cp("parallel", "parallel"),
    )(da, gu, gu, gu, w_dw, b_dw[:, None, :])


def _ffn_act_bwd2(dgp, dgu, gu, w_dw, l):
    S, F = dgp.shape
    cw = _ffn_chunk(F)
    ncc = F // cw
    tm = _tm(S)
    ni = S // tm
    hb = tm // FFN_HALO
    last_hb = S // FFN_HALO - 1

    def body(d_ref, dn_ref, g_ref, gh_ref, w_ref, _, dg_ref, sums_ref, extd, extg):
        i = pl.program_id(1)

        @pl.when(i == 0)
        def _():
            sums_ref[...] = jnp.zeros_like(sums_ref)
        dv = d_ref[...]
        extd[0:tm, :] = dv
        extd[tm:, :] = jnp.where(i < ni - 1, dn_ref[...], 0.0)
        extg[0:FFN_HALO, :] = jnp.where(i > 0, gh_ref[...], 0.0)
        extg[FFN_HALO:, :] = g_ref[...]
        acc = jnp.zeros((tm, cw), F32)
        for k in range(FFN_K):
            acc = acc + w_ref[0, k:k + 1, :] * extd[pl.ds(FFN_K - 1 - k, tm), :]
            sums_ref[k:k + 1, :] += jnp.sum(dv * extg[pl.ds(FFN_HALO - (FFN_K - 1) + k, tm), :], axis=0, keepdims=True)
        sums_ref[FFN_K:FFN_K + 1, :] += jnp.sum(dv, axis=0, keepdims=True)
        dg_ref[...] = acc.astype(BF16)

    return pl.pallas_call(
        body, name="ffn_act_bwd2", grid=(ncc, ni),
        in_specs=[pl.BlockSpec((tm, cw), lambda cc, i: (i, cc)),
                  pl.BlockSpec((FFN_HALO, cw), lambda cc, i: (jnp.minimum((i + 1) * hb, last_hb), cc)),
                  pl.BlockSpec((tm, cw), lambda cc, i: (i, cc)),
                  pl.BlockSpec((FFN_HALO, cw), lambda cc, i: (jnp.maximum(i * hb - 1, 0), cc)),
                  pl.BlockSpec((1, FFN_K, cw), lambda cc, i: (l, 0, cc)),
                  pl.BlockSpec(memory_space=pl.ANY)],
        out_specs=[pl.BlockSpec((tm, cw), lambda cc, i: (i, cc)),
                   pl.BlockSpec((8, cw), lambda cc, i: (0, cc))],
        out_shape=[SDS((S, 2 * F), BF16), SDS((8, F), F32)],
        scratch_shapes=[pltpu.VMEM((tm + FFN_HALO, cw), F32), pltpu.VMEM((tm + FFN_HALO, cw), F32)],
        input_output_aliases={5: 0},
        compiler_params=_cp("parallel", "arbitrary"),
    )(dgp, dgp, gu, gu, w_dw, dgu)


def _merge_bwd(dm, z, y_a, y_b, D):
    S = z.shape[0]
    tm = _tm(S)
    ncc = D // CHUNK
    ga0 = (3 * ATTN_W + 2 * D) // CHUNK

    def body(dm_ref, ga_ref, gb_ref, ya_ref, yb_ref, dya_ref, dyb_ref, dga_ref, dgb_ref):
        dmv = dm_ref[...]
        sa, sb = _sigmoid(ga_ref[...]), _sigmoid(gb_ref[...])
        dya_ref[...] = (dmv * sa).astype(BF16)
        dyb_ref[...] = (dmv * sb).astype(BF16)
        dga_ref[...] = (dmv * ya_ref[...] * sa * (1.0 - sa)).astype(BF16)
        dgb_ref[...] = (dmv * yb_ref[...] * sb * (1.0 - sb)).astype(BF16)

    ysp = pl.BlockSpec((tm, CHUNK), lambda i, cc: (i, cc))
    return pl.pallas_call(
        body, name="merge_bwd", grid=(S // tm, ncc),
        in_specs=[ysp, pl.BlockSpec((tm, CHUNK), lambda i, cc: (i, ga0 + cc)),
                  pl.BlockSpec((tm, CHUNK), lambda i, cc: (i, ga0 + ncc + cc)), ysp, ysp],
        out_specs=[ysp] * 4, out_shape=[SDS((S, D), BF16)] * 4,
        compiler_params=_cp("parallel", "parallel"),
    )(dm, z, z, y_a, y_b)


def _ln_silu_bwd(du3, u1, g, b):
    S, D = u1.shape
    tm = _tm(S)

    def body(d_ref, u_ref, g_ref, b_ref, du_ref, sums_ref):
        @pl.when(pl.program_id(0) == 0)
        def _():
            sums_ref[...] = jnp.zeros_like(sums_ref)
        xh, rstd = _ln_stats(u_ref[...])
        gv = g_ref[...]
        yv = xh * gv + b_ref[...]
        sg = _sigmoid(yv)
        dy = d_ref[...] * (sg * (1.0 + yv * (1.0 - sg)))
        dxh = dy * gv
        du_ref[...] = rstd * (dxh - jnp.mean(dxh, axis=-1, keepdims=True)
                              - xh * jnp.mean(dxh * xh, axis=-1, keepdims=True))
        sums_ref[0:1, :] += jnp.sum(dy * xh, axis=0, keepdims=True)
        sums_ref[1:2, :] += jnp.sum(dy, axis=0, keepdims=True)

    row = pl.BlockSpec((tm, D), lambda i: (i, 0))
    vec = pl.BlockSpec((1, D), lambda i: (0, 0))
    return pl.pallas_call(
        body, name="ln_silu_bwd", grid=(S // tm,), in_specs=[row, row, vec, vec],
        out_specs=[row, pl.BlockSpec((8, D), lambda i: (0, 0))],
        out_shape=[SDS((S, D), F32), SDS((8, D), F32)], compiler_params=_cp("arbitrary"),
    )(du3, u1, g, b)


def _glu_conv_bwd(du1, z, w_dw, l, D):
    S = z.shape[0]
    tm = _tm(S)
    ni = S // tm
    cc0 = 3 * ATTN_W // CHUNK
    ncc = D // CHUNK
    hb = tm // CONV_HALO
    last_hb = S // CONV_HALO - 1
    lead = CONV_HALO - (CONV_K - 1)
    group = 8

    def body(d_ref, dn_ref, cv_ref, cg_ref, hv_ref, hg_ref, w_ref, dcv_ref, dcg_ref, sums_ref,
             extd, extu, shd, shu):
        i = pl.program_id(1)

        @pl.when(i == 0)
        def _():
            sums_ref[...] = jnp.zeros_like(sums_ref)
        extd[0:tm, :] = d_ref[...]
        extd[tm:, :] = jnp.where(i < ni - 1, dn_ref[...], 0.0)
        extu[0:CONV_HALO, :] = jnp.where(i > 0, _glu(hv_ref[...], hg_ref[...]), 0.0)
        extu[CONV_HALO:, :] = _glu(cv_ref[...], cg_ref[...])
        _fill_shifted(shd, extd, tm + CONV_HALO - SHIFTS)
        _fill_shifted(shu, extu, tm + CONV_HALO - SHIFTS)

        def strip(s, carry):
            base = pl.multiple_of(s * STRIP, STRIP)
            acc = jnp.zeros((STRIP, CHUNK), F32)
            for k in range(CONV_K):
                acc = acc + w_ref[0, k:k + 1, :] * _tap(shd, base, CONV_K - 1 - k)
            cv, cg = cv_ref[pl.ds(base, STRIP), :], cg_ref[pl.ds(base, STRIP), :]
            sg = _sigmoid(cg)
            dcv_ref[pl.ds(base, STRIP), :] = (acc * sg).astype(BF16)
            dcg_ref[pl.ds(base, STRIP), :] = (acc * cv * sg * (1.0 - sg)).astype(BF16)
            return carry

        lax.fori_loop(0, tm // STRIP, strip, 0)

        for k0 in range(0, CONV_K + 1, group):
            ks = list(range(k0, min(k0 + group, CONV_K + 1)))

            def rows(s, accs, ks=ks):
                base = pl.multiple_of(s * SHIFTS, SHIFTS)
                dv = extd[pl.ds(base, SHIFTS), :]
                return tuple(a + (dv if k == CONV_K else dv * _tap(shu, base, lead + k, SHIFTS))
                             for a, k in zip(accs, ks))

            accs = lax.fori_loop(0, tm // SHIFTS, rows,
                                 tuple(jnp.zeros((SHIFTS, CHUNK), F32) for _ in ks))
            for k, a in zip(ks, accs):
                sums_ref[k:k + 1, :] += jnp.sum(a, axis=0, keepdims=True)

    cur = lambda off: pl.BlockSpec((tm, CHUNK), lambda cc, i: (i, cc0 + off + cc))
    halo = lambda off: pl.BlockSpec((CONV_HALO, CHUNK), lambda cc, i: (jnp.maximum(i * hb - 1, 0), cc0 + off + cc))
    osp = pl.BlockSpec((tm, CHUNK), lambda cc, i: (i, cc))
    big = pltpu.VMEM((tm + CONV_HALO, CHUNK), F32)
    shifted = pltpu.VMEM((SHIFTS, tm + CONV_HALO, CHUNK), F32)
    return pl.pallas_call(
        body, name="glu_conv_bwd", grid=(ncc, ni),
        in_specs=[osp,
                  pl.BlockSpec((CONV_HALO, CHUNK), lambda cc, i: (jnp.minimum((i + 1) * hb, last_hb), cc)),
                  cur(0), cur(ncc), halo(0), halo(ncc),
                  pl.BlockSpec((1, CONV_K, CHUNK), lambda cc, i: (l, 0, cc))],
        out_specs=[osp, osp, pl.BlockSpec((CONV_HALO, CHUNK), lambda cc, i: (0, cc))],
        out_shape=[SDS((S, D), BF16), SDS((S, D), BF16), SDS((CONV_HALO, D), F32)],
        scratch_shapes=[big, big, shifted, shifted],
        compiler_params=_cp("parallel", "arbitrary"),
    )(du1, du1, z, z, z, z, w_dw)


def _attn_bwd_prep(dattn, outs, lses):
    S = dattn.shape[0]
    tm = _tm(S)
    ng = len(DILATIONS)

    def body(*refs):
        da_ref, o_refs, l_refs = refs[0], refs[1:1 + ng], refs[1 + ng:1 + 2 * ng]
        d_refs, t_refs = refs[1 + 2 * ng:1 + 3 * ng], refs[1 + 3 * ng:1 + 4 * ng]
        scr = refs[1 + 4 * ng:]
        for h in range(HEADS_PER_GROUP):
            cols = slice(h * HEAD_DIM, (h + 1) * HEAD_DIM)
            ov = [_from_dilated(scr[g], o_refs[g], DILATIONS[g], cols) for g in range(ng)]
            lv = [_from_dilated(scr[ng + g], l_refs[g], DILATIONS[g], cols) for g in range(ng)]
            w = _group_weights(lv)
            dav = da_ref[:, cols]
            rs = jnp.sum(dav * (w[0] * ov[0] + w[1] * ov[1] + w[2] * ov[2]), axis=-1, keepdims=True)
            rs = jnp.broadcast_to(rs, (tm, HEAD_DIM))
            for g in range(ng):
                _to_dilated(scr[g], w[g] * dav, d_refs[g], DILATIONS[g], cols)
                _to_dilated(scr[ng + g], -w[g] * rs, t_refs[g], DILATIONS[g], cols)

    dil = [_dil_spec(d, tm) for d in DILATIONS]
    out = pl.pallas_call(
        body, name="attn_bwd_prep", grid=(S // tm,),
        in_specs=[pl.BlockSpec((tm, GROUP_W), lambda i: (i, 0))] + dil + dil, out_specs=dil + dil,
        out_shape=[SDS((d, S // d, GROUP_W), BF16) for d in DILATIONS]
        + [SDS((d, S // d, GROUP_W), F32) for d in DILATIONS],
        scratch_shapes=[pltpu.VMEM((tm, HEAD_DIM), F32)] * (2 * ng),
        compiler_params=_cp("parallel"),
    )(dattn, *outs, *lses)
    return out[:ng], out[ng:]


def _attn_bwd(q, k, v, do, lse, dterm, gi):
    d, Sd, _ = q.shape
    nb = Sd // BLOCK
    scale = HEAD_DIM ** -0.5

    def body(q_ref, qx_ref, kc_ref, kp_ref, vc_ref, vp_ref, do_ref, dox_ref, l_ref, lx_ref, t_ref, tx_ref,
             dq_ref, dk_ref, dv_ref, kk, vv, qq, dd, ds2, p2):
        n = pl.program_id(1)
        kk[0:BLOCK, :], kk[BLOCK:, :] = kp_ref[...], kc_ref[...]
        vv[0:BLOCK, :], vv[BLOCK:, :] = vp_ref[...], vc_ref[...]
        qq[0:BLOCK, :], qq[BLOCK:, :] = q_ref[...], qx_ref[...]
        dd[0:BLOCK, :], dd[BLOCK:, :] = do_ref[...], dox_ref[...]
        mask_ab = _band_mask(n > 0)
        qi = lax.broadcasted_iota(jnp.int32, (BLOCK, BLOCK), 0)
        kj = lax.broadcasted_iota(jnp.int32, (BLOCK, BLOCK), 1)
        mask_c = jnp.logical_and(kj >= qi, n < nb - 1)

        def pair(qh, kh, vh, dov, lv, tv, mask):
            s = lax.dot_general(qh, kh, NT_DIMS, preferred_element_type=F32) * scale
            p = jnp.exp(jnp.where(mask, s - lv, NEG))
            dp = lax.dot_general(dov, vh, NT_DIMS, preferred_element_type=F32)
            return p, p * (dp + tv) * scale

        for h in range(HEADS_PER_GROUP):
            sl = slice(h * HEAD_DIM, (h + 1) * HEAD_DIM)
            lv = jnp.concatenate([l_ref[:, sl], l_ref[:, sl]], axis=1)
            tv = jnp.concatenate([t_ref[:, sl], t_ref[:, sl]], axis=1)
            p_ab, ds_ab = pair(q_ref[:, sl], kk[:, sl], vv[:, sl], do_ref[:, sl], lv, tv, mask_ab)
            p_c, ds_c = pair(qx_ref[:, sl], kc_ref[:, sl], vc_ref[:, sl], dox_ref[:, sl],
                             lx_ref[:, sl], tx_ref[:, sl], mask_c)
            dq_ref[:, sl] = jnp.dot(ds_ab.astype(BF16), kk[:, sl], preferred_element_type=F32)
            ds2[0:BLOCK, :], ds2[BLOCK:, :] = ds_ab[:, BLOCK:].astype(BF16), ds_c.astype(BF16)
            p2[0:BLOCK, :], p2[BLOCK:, :] = p_ab[:, BLOCK:].astype(BF16), p_c.astype(BF16)
            dk_ref[:, sl] = lax.dot_general(ds2[...], qq[:, sl], TN_DIMS, preferred_element_type=F32)
            dv_ref[:, sl] = lax.dot_general(p2[...], dd[:, sl], TN_DIMS, preferred_element_type=F32).astype(BF16)

    cur, prev, nxt = _attn_blk(nb, 0), _attn_blk(nb, -1), _attn_blk(nb, 1)
    wide = pltpu.VMEM((2 * BLOCK, GROUP_W), BF16)
    tall = pltpu.VMEM((2 * BLOCK, HEAD_DIM), BF16)
    return pl.pallas_call(
        body, name=f"attn_bwd_g{gi}", grid=(d, nb),
        in_specs=[cur, nxt, cur, prev, cur, prev, cur, nxt, cur, nxt, cur, nxt],
        out_specs=[cur] * 3,
        out_shape=[SDS((d, Sd, GROUP_W), F32)] * 2 + [SDS((d, Sd, GROUP_W), BF16)],
        scratch_shapes=[wide, wide, wide, wide, tall, tall],
        compiler_params=_cp("parallel", "parallel"),
    )(q, q, k, k, v, v, do, do, lse, lse, dterm, dterm)


def _qk_prep_bwd(dqs, dks, dvs, z, g_q, g_k, tabs):
    S = z.shape[0]
    tm = _tm(S)
    n_heads = ATTN_W // HEAD_DIM
    ng = len(DILATIONS)

    def body(*refs):
        dq_refs, dk_refs, dv_refs = refs[:ng], refs[ng:2 * ng], refs[2 * ng:3 * ng]
        q_ref, k_ref, gq_ref, gk_ref, c_ref, sa_ref, sb_ref = refs[3 * ng:3 * ng + 7]
        dqo_ref, dko_ref, dvo_ref, sums_ref, scr = refs[3 * ng + 7:]

        @pl.when(pl.program_id(0) == 0)
        def _():
            sums_ref[...] = jnp.zeros_like(sums_ref)
        cos_t, sa_t, sb_t = c_ref[...], sa_ref[...], sb_ref[...]
        for row, (drefs, src, gref, dst) in enumerate(((dq_refs, q_ref, gq_ref, dqo_ref),
                                                       (dk_refs, k_ref, gk_ref, dko_ref))):
            gv = gref[...]
            gsum = jnp.zeros((1, HEAD_DIM), F32)
            for h in range(n_heads):
                gi = h // HEADS_PER_GROUP
                sl = slice(h * HEAD_DIM, (h + 1) * HEAD_DIM)
                cols = slice((h % HEADS_PER_GROUP) * HEAD_DIM, (h % HEADS_PER_GROUP + 1) * HEAD_DIM)
                dyn = _unrope(_from_dilated(scr, drefs[gi], DILATIONS[gi], cols), cos_t, sa_t, sb_t)
                t = src[:, sl]
                r = lax.rsqrt(jnp.mean(t * t, axis=-1, keepdims=True) + EPS)
                xh = t * r
                gsum = gsum + jnp.sum(dyn * xh, axis=0, keepdims=True)
                gy = dyn * gv
                dst[:, sl] = (r * (gy - xh * jnp.mean(gy * xh, axis=-1, keepdims=True))).astype(BF16)
            sums_ref[row:row + 1, :] += gsum
        for h in range(n_heads):
            gi = h // HEADS_PER_GROUP
            cols = slice((h % HEADS_PER_GROUP) * HEAD_DIM, (h % HEADS_PER_GROUP + 1) * HEAD_DIM)
            dvo_ref[:, h * HEAD_DIM:(h + 1) * HEAD_DIM] = _from_dilated(
                scr, dv_refs[gi], DILATIONS[gi], cols).astype(BF16)

    dil = [_dil_spec(d, tm) for d in DILATIONS]
    tab = pl.BlockSpec((tm, HEAD_DIM), lambda i: (i, 0))
    gsp = pl.BlockSpec((1, HEAD_DIM), lambda i: (0, 0))
    wide = pl.BlockSpec((tm, ATTN_W), lambda i: (i, 0))
    return pl.pallas_call(
        body, name="qk_prep_bwd", grid=(S // tm,),
        in_specs=dil * 3 + [wide, pl.BlockSpec((tm, ATTN_W), lambda i: (i, 1)), gsp, gsp, tab, tab, tab],
        out_specs=[wide, wide, wide, pl.BlockSpec((8, HEAD_DIM), lambda i: (0, 0))],
        out_shape=[SDS((S, ATTN_W), BF16)] * 3 + [SDS((8, HEAD_DIM), F32)],
        scratch_shapes=[pltpu.VMEM((tm, HEAD_DIM), F32)],
        compiler_params=_cp("arbitrary"),
    )(*dqs, *dks, *dvs, z, z, g_q, g_k, *tabs)


def _adamw(w, g, m, v, name):
    R, C = w.shape
    tr = _row_tile(R, C * 2)
    c1 = 1.0 - ADAM_B1 ** ADAM_STEP
    c2 = 1.0 - ADAM_B2 ** ADAM_STEP

    def body(w_ref, g_ref, m_ref, v_ref, d_ref, mo_ref, vo_ref):
        gv = g_ref[...]
        mn = ADAM_B1 * m_ref[...] + (1.0 - ADAM_B1) * gv
        vn = ADAM_B2 * v_ref[...] + (1.0 - ADAM_B2) * (gv * gv)
        mo_ref[...] = mn
        vo_ref[...] = vn
        d_ref[...] = -ADAM_LR * ((mn / c1) / (jnp.sqrt(vn / c2) + ADAM_EPS) + ADAM_WD * w_ref[...])

    sp = pl.BlockSpec((tr, C), lambda i: (i, 0))
    return pl.pallas_call(
        body, name=name, grid=(R // tr,), in_specs=[sp] * 4, out_specs=[sp] * 3,
        out_shape=[SDS((R, C), F32)] * 3, compiler_params=_cp("parallel"),
    )(w, g, m, v)


def _sum_parts(parts):
    n, R, C = parts.shape

    def body(p_ref, o_ref):
        acc = p_ref[0]
        for k in range(1, n):
            acc = acc + p_ref[k]
        o_ref[...] = acc

    return pl.pallas_call(
        body, name="sum_parts", out_shape=SDS((R, C), F32),
        compiler_params=pltpu.CompilerParams(vmem_limit_bytes=VMEM_LIMIT_BYTES),
    )(parts)


def _wada_grad(cact_t, dmod):
    D, n = cact_t.shape
    L, _, cols = dmod.shape

    def body(c_ref, d_ref, o_ref):
        acc = c_ref[:, 0:1] * d_ref[0, 0:1, :]
        for k in range(1, n):
            acc = acc + c_ref[:, k:k + 1] * d_ref[0, k:k + 1, :]
        o_ref[0] = acc

    return pl.pallas_call(
        body, name="wada_grad", grid=(L,),
        in_specs=[pl.BlockSpec((D, n), lambda l: (0, 0)), pl.BlockSpec((1, n, cols), lambda l: (l, 0, 0))],
        out_specs=pl.BlockSpec((1, D, cols), lambda l: (l, 0, 0)),
        out_shape=SDS((L, D, cols), F32), compiler_params=_cp("parallel"),
    )(cact_t, dmod)


def _add_halves(dw, land, c_idx, name):
    L, _, _, r2, cols = dw.shape
    tr = _row_tile(r2, cols, step=16)
    dw4 = dw.reshape(L * 4, 2, r2, cols)
    land3 = land.reshape(L * 4, r2, cols)

    def body(pf, a_ref, b_ref, o_ref):
        o_ref[...] = (a_ref[...] + b_ref[...]).astype(BF16)

    out = pl.pallas_call(
        body, name=name,
        grid_spec=pltpu.PrefetchScalarGridSpec(
            num_scalar_prefetch=1, grid=(L * 4, r2 // tr),
            in_specs=[pl.BlockSpec((None, None, tr, cols), lambda a, i, pf: (a, pf[0], i, 0)),
                      pl.BlockSpec((None, tr, cols), lambda a, i, pf: (a, i, 0))],
            out_specs=pl.BlockSpec((None, tr, cols), lambda a, i, pf: (a, i, 0))),
        out_shape=SDS((L * 4, r2, cols), BF16), compiler_params=_cp("parallel", "parallel"),
    )(c_idx, dw4, land3)
    return out.reshape(L, 4, r2, cols)


def _add_chips(dw, land1, land2, cj_idx, name):
    L, _, _, r2, cols = dw.shape
    tr = _row_tile(r2, cols, step=16)

    def body(pf, a_ref, b_ref, c0, c1, c2, o_ref):
        own = a_ref[...] + b_ref[...]
        o_ref[...] = ((own + c0[...].astype(F32)) + c1[...].astype(F32)) + c2[...].astype(F32)

    lsp = lambda p: pl.BlockSpec((None, None, tr, cols), lambda l, i, pf: (p, l, i, 0))
    return pl.pallas_call(
        body, name=name,
        grid_spec=pltpu.PrefetchScalarGridSpec(
            num_scalar_prefetch=1, grid=(L, r2 // tr),
            in_specs=[pl.BlockSpec((None, None, None, tr, cols), lambda l, i, pf: (l, pf[1], pf[0], i, 0)),
                      pl.BlockSpec((None, None, tr, cols), lambda l, i, pf: (l, pf[1], i, 0)),
                      lsp(0), lsp(1), lsp(2)],
            out_specs=pl.BlockSpec((None, None, tr, cols), lambda l, i, pf: (l, pf[0], i, 0))),
        out_shape=SDS((L, 2, r2, cols), F32), compiler_params=_cp("parallel", "parallel"),
    )(cj_idx, dw, land1, land2, land2, land2)


HBM_SPEC = pl.BlockSpec(memory_space=pltpu.HBM)


def _place():
    x, y, c = lax.axis_index("x"), lax.axis_index("y"), lax.axis_index("c")
    chips = [(x, 1 - y), (1 - x, y), (1 - x, 1 - y)]
    return x, y, c, chips


def _remote(src, dst, send_sem, recv_sem, device):
    return pltpu.make_async_remote_copy(src_ref=src, dst_ref=dst, send_sem=send_sem, recv_sem=recv_sem,
                                        device_id=device, device_id_type=MESH)


def _comm_call(body, name, ins, out_shapes, n_sems):
    return pl.pallas_call(
        body, name=name, in_specs=[HBM_SPEC] * len(ins), out_specs=[HBM_SPEC] * len(out_shapes),
        out_shape=out_shapes,
        scratch_shapes=[pltpu.SemaphoreType.DMA((n,)) for n in n_sems],
    )(*ins)


def _all_gather_weights(shards, small):
    nw = len(shards)
    L = shards[0].shape[0]
    Lh = L // 2

    def body(*refs):
        sh, sm = refs[:nw], refs[nw]
        full, smo = refs[nw + 1:2 * nw + 1], refs[2 * nw + 1]
        ici_s, ici_r, d2d_s, d2d_r, loc = refs[2 * nw + 2:]
        x, y, c, chips = _place()
        j = 2 * x + y
        jps = [2 * px + py for px, py in chips]
        me, sib = (x, y, c), (x, y, 1 - c)
        mine, theirs = pl.ds(c * Lh, Lh), pl.ds((1 - c) * Lh, Lh)

        local = [pltpu.make_async_copy(sh[w], full[w].at[j], loc.at[w]) for w in range(nw)]
        local.append(pltpu.make_async_copy(sm, smo.at[j], loc.at[nw]))
        for cp in local:
            cp.start()
        sends = []
        for w in range(nw):
            for p in range(3):
                sends.append(_remote(sh[w].at[mine], full[w].at[j, mine], ici_s.at[3 * w + p], ici_r.at[3 * w + p],
                                     (*chips[p], c)))
        for p in range(3):
            sends.append(_remote(sm, smo.at[j], ici_s.at[3 * nw + p], ici_r.at[3 * nw + p], (*chips[p], c)))
        for cp in sends:
            cp.start()
        passed = []
        for w in range(nw):
            for p in range(3):
                slot = full[w].at[jps[p], mine]
                _remote(slot, slot, ici_s.at[3 * w + p], ici_r.at[3 * w + p], me).wait_recv()
                cp = _remote(slot, slot, d2d_s.at[3 * w + p], d2d_r.at[3 * w + p], sib)
                cp.start()
                passed.append(cp)
        for p in range(3):
            slot = smo.at[jps[p]]
            _remote(slot, slot, ici_s.at[3 * nw + p], ici_r.at[3 * nw + p], me).wait_recv()
        for w in range(nw):
            for p in range(3):
                slot = full[w].at[jps[p], theirs]
                _remote(slot, slot, d2d_s.at[3 * w + p], d2d_r.at[3 * w + p], me).wait_recv()
        for cp in sends + passed:
            cp.wait_send()
        for cp in local:
            cp.wait()

    outs = [SDS((4,) + s.shape, s.dtype) for s in shards] + [SDS((4,) + small.shape, small.dtype)]
    res = _comm_call(body, "all_gather_weights", list(shards) + [small], outs,
                     [3 * nw + 3, 3 * nw + 3, 3 * nw, 3 * nw, nw + 1])
    return res[:nw], res[nw]


def _all_gather_small(part):
    def body(p_ref, o_ref, send_s, recv_s, loc):
        x, y, c, _ = _place()
        me_id = 4 * x + 2 * y + c
        own = pltpu.make_async_copy(p_ref, o_ref.at[me_id], loc.at[0])
        own.start()
        sends = []
        for k in range(1, 8):
            peer = (x ^ (k >> 2), y ^ ((k >> 1) & 1), c ^ (k & 1))
            sends.append(_remote(p_ref, o_ref.at[me_id], send_s.at[k - 1], recv_s.at[k - 1], peer))
        for cp in sends:
            cp.start()
        for k in range(1, 8):
            peer_id = 4 * (x ^ (k >> 2)) + 2 * (y ^ ((k >> 1) & 1)) + (c ^ (k & 1))
            slot = o_ref.at[peer_id]
            _remote(slot, slot, send_s.at[k - 1], recv_s.at[k - 1], (x, y, c)).wait_recv()
        for cp in sends:
            cp.wait_send()
        own.wait()

    return _comm_call(body, "all_gather_small", [part], [SDS((8,) + part.shape, part.dtype)], [7, 7, 1])[0]


def _rs_to_sibling(dws):
    nw = len(dws)

    def body(*refs):
        src, land = refs[:nw], refs[nw:2 * nw]
        send_s, recv_s = refs[2 * nw:]
        x, y, c, _ = _place()
        cps = [_remote(src[w].at[:, :, 1 - c], land[w], send_s.at[w], recv_s.at[w], (x, y, 1 - c))
               for w in range(nw)]
        for cp in cps:
            cp.start()
        for cp in cps:
            cp.wait_recv()
        for cp in cps:
            cp.wait_send()

    outs = [SDS((a.shape[0], 4, a.shape[3], a.shape[4]), a.dtype) for a in dws]
    return _comm_call(body, "rs_to_sibling", list(dws), outs, [nw, nw])


def _rs_to_chips(s1s):
    nw = len(s1s)

    def body(*refs):
        src, land = refs[:nw], refs[nw:2 * nw]
        send_s, recv_s = refs[2 * nw:]
        x, y, c, chips = _place()
        cps = []
        for w in range(nw):
            for p in range(3):
                jp = 2 * chips[p][0] + chips[p][1]
                cps.append(_remote(src[w].at[:, jp], land[w].at[p], send_s.at[3 * w + p], recv_s.at[3 * w + p],
                                   (*chips[p], c)))
        for cp in cps:
            cp.start()
        for cp in cps:
            cp.wait_recv()
        for cp in cps:
            cp.wait_send()

    outs = [SDS((3, a.shape[0], a.shape[2], a.shape[3]), a.dtype) for a in s1s]
    return _comm_call(body, "rs_to_chips", list(s1s), outs, [3 * nw, 3 * nw])


def _rs_join_halves(fulls):
    nw = len(fulls)

    def body(*refs):
        full = refs[nw:2 * nw]
        send_s, recv_s = refs[2 * nw:]
        x, y, c, _ = _place()
        cps = [_remote(full[w].at[:, c], full[w].at[:, c], send_s.at[w], recv_s.at[w], (x, y, 1 - c))
               for w in range(nw)]
        for cp in cps:
            cp.start()
        for w in range(nw):
            slot = full[w].at[:, 1 - c]
            _remote(slot, slot, send_s.at[w], recv_s.at[w], (x, y, c)).wait_recv()
        for cp in cps:
            cp.wait_send()

    return pl.pallas_call(
        body, name="rs_join_halves", in_specs=[HBM_SPEC] * nw, out_specs=[HBM_SPEC] * nw,
        out_shape=[SDS(a.shape, a.dtype) for a in fulls],
        scratch_shapes=[pltpu.SemaphoreType.DMA((nw,)), pltpu.SemaphoreType.DMA((nw,))],
        input_output_aliases={w: w for w in range(nw)},
    )(*fulls)


BIG = ("w_in", "w_attn_proj", "w_conv_out", "w_o", "w_ffn_in", "w_ffn_down")
COL_SHARDED = {"w_in": True, "w_attn_proj": True, "w_conv_out": False, "w_o": False,
               "w_ffn_in": True, "w_ffn_down": False}
SMALL = ("b_ada", "g_norm1", "g_q", "g_k", "w_conv_dw", "b_conv_dw", "g_conv_ln", "b_conv_ln",
         "g_norm2", "w_ffn_dw", "b_ffn_dw")
WEIGHTS = ("w_ada", "b_ada", "g_norm1", "w_in", "g_q", "g_k", "w_attn_proj", "w_conv_dw", "b_conv_dw",
           "g_conv_ln", "b_conv_ln", "w_conv_out", "w_o", "g_norm2", "w_ffn_in", "w_ffn_dw", "b_ffn_dw",
           "w_ffn_down")


def _pack_rows(arrs):
    flat = jnp.concatenate([a.reshape(-1) for a in arrs])
    pad = (-flat.shape[0]) % (8 * 128)
    return jnp.pad(flat, (0, pad)).reshape(-1, 128)


def _unpack_rows(packed, shapes):
    flat = packed.reshape(-1)
    out, off = [], 0
    for shp in shapes:
        n = 1
        for s in shp:
            n *= s
        out.append(flat[off:off + n].reshape(shp))
        off += n
    return out


def _rope_tables(positions):
    half = ROT_DIM // 2
    inv_freq = ROPE_THETA ** (-jnp.arange(0, ROT_DIM, 2, dtype=F32) / ROT_DIM)
    ang = positions.astype(F32)[..., None] * inv_freq
    cos, sin = jnp.cos(ang), jnp.sin(ang)
    S = cos.shape[0]
    cos_t = jnp.concatenate([cos, cos, jnp.ones((S, HEAD_DIM - ROT_DIM), F32)], axis=1)
    sa_t = jnp.concatenate([-sin, jnp.zeros((S, HEAD_DIM - half), F32)], axis=1)
    sb_t = jnp.concatenate([jnp.zeros((S, half), F32), sin, jnp.zeros((S, HEAD_DIM - ROT_DIM), F32)], axis=1)
    return cos_t, sa_t, sb_t


def _local_step(x, c, positions, loss_target, wf, small_w):
    S, D = x.shape
    L = wf["w_in"].shape[0]
    tabs = _rope_tables(positions)
    mod, cact = _mod_fwd(c, wf["w_ada"], small_w["b_ada"])
    vec = lambda a: a.reshape(1, -1)
    mods = [[vec(mod[l, k * D:(k + 1) * D]) for k in range(6)] for l in range(L)]

    saved = []
    xs = x
    for l in range(L):
        sh1, sc1, gt1, sh2, sc2, gt2 = mods[l]
        g1, g2 = vec(small_w["g_norm1"][l]), vec(small_w["g_norm2"][l])
        gq, gk = vec(small_w["g_q"][l]), vec(small_w["g_k"][l])
        z, h = _normmod_mm(xs, g1, sc1, sh1, wf["w_in"], l, "in_proj")
        qs, ks, vs = _qk_prep(z, gq, gk, tabs)
        outs, lses = [], []
        for gi in range(len(DILATIONS)):
            o_g, lse_g = _attn_fwd(qs[gi], ks[gi], vs[gi], gi)
            outs.append(o_g)
            lses.append(lse_g)
        attn = _attn_combine(outs, lses)
        y_a = _mm_nn(attn, wf["w_attn_proj"], l, "attn_proj")
        u1 = _glu_conv(z, small_w["w_conv_dw"], small_w["b_conv_dw"], l, D)
        u3 = _ln_silu(u1, vec(small_w["g_conv_ln"][l]), vec(small_w["b_conv_ln"][l]))
        y_b = _mm_nn(u3, wf["w_conv_out"], l, "conv_out")
        merged = _merge(z, y_a, y_b, D)
        t_o, x2 = _mm_nn(merged, wf["w_o"], l, "out_proj", res=xs, gate=gt1)
        gu, h2 = _normmod_mm(x2, g2, sc2, sh2, wf["w_ffn_in"], l, "ffn_in")
        a = _ffn_act(gu, small_w["w_ffn_dw"], small_w["b_ffn_dw"], l)
        t_f, x3 = _mm_nn(a, wf["w_ffn_down"], l, "ffn_down", res=x2, gate=gt2)
        saved.append(dict(x=xs, z=z, h=h, qs=qs, ks=ks, vs=vs, outs=outs, lses=lses, attn=attn, y_a=y_a, u1=u1, u3=u3,
                          y_b=y_b, merged=merged, t_o=t_o, x2=x2, gu=gu, h2=h2, a=a, t_f=t_f))
        xs = x3

    dx, loss = _loss_grad(xs, loss_target)

    bufs = {n: None for n in BIG}
    small_g = {n: [None] * L for n in SMALL}
    dmods = [None] * L

    def wgrad(name, a_mat, b_mat, l):
        bufs[name] = _mm_tn(a_mat, b_mat, bufs[name], l, L, COL_SHARDED[name], "grad_" + name)

    for l in reversed(range(L)):
        sv = saved[l]
        sh1, sc1, gt1, sh2, sc2, gt2 = mods[l]
        g1, g2 = vec(small_w["g_norm1"][l]), vec(small_w["g_norm2"][l])
        gq, gk = vec(small_w["g_q"][l]), vec(small_w["g_k"][l])
        e2, dgt2 = _gate_bwd(dx, sv["t_f"], gt2, "gate_bwd_ffn")
        wgrad("w_ffn_down", sv["a"], e2, l)
        da = _mm_nt(e2, wf["w_ffn_down"], l, "ffn_down_bwd")
        dgp, dgu = _ffn_act_bwd1(da, sv["gu"], small_w["w_ffn_dw"], small_w["b_ffn_dw"], l)
        dgu, ffn_sums = _ffn_act_bwd2(dgp, dgu, sv["gu"], small_w["w_ffn_dw"], l)
        wgrad("w_ffn_in", sv["h2"], dgu, l)
        dh2 = _mm_nt(dgu, wf["w_ffn_in"], l, "ffn_in_bwd")
        dx2, n2_sums = _norm_bwd(dh2, sv["x2"], dx, g2, sc2, "norm2_bwd")
        e1, dgt1 = _gate_bwd(dx2, sv["t_o"], gt1, "gate_bwd_mix")
        wgrad("w_o", sv["merged"], e1, l)
        dmerged = _mm_nt(e1, wf["w_o"], l, "out_proj_bwd")
        dya, dyb, dga, dgb = _merge_bwd(dmerged, sv["z"], sv["y_a"], sv["y_b"], D)
        wgrad("w_attn_proj", sv["attn"], dya, l)
        wgrad("w_conv_out", sv["u3"], dyb, l)
        dattn = _mm_nt(dya, wf["w_attn_proj"], l, "attn_proj_bwd")
        du3 = _mm_nt(dyb, wf["w_conv_out"], l, "conv_out_bwd")
        du1, ln_sums = _ln_silu_bwd(du3, sv["u1"], vec(small_w["g_conv_ln"][l]), vec(small_w["b_conv_ln"][l]))
        dcv, dcg, conv_sums = _glu_conv_bwd(du1, sv["z"], small_w["w_conv_dw"], l, D)
        dos, dterms = _attn_bwd_prep(dattn, sv["outs"], sv["lses"])
        dqs, dks, dvs = [], [], []
        for gi in range(len(DILATIONS)):
            dq_g, dk_g, dv_g = _attn_bwd(sv["qs"][gi], sv["ks"][gi], sv["vs"][gi], dos[gi], sv["lses"][gi],
                                         dterms[gi], gi)
            dqs.append(dq_g)
            dks.append(dk_g)
            dvs.append(dv_g)
        dq, dk, dv, qk_sums = _qk_prep_bwd(dqs, dks, dvs, sv["z"], gq, gk, tabs)
        dz = jnp.concatenate([dq, dk, dv, dcv, dcg, dga, dgb], axis=1)
        wgrad("w_in", sv["h"], dz, l)
        dh = _mm_nt(dz, wf["w_in"], l, "in_proj_bwd")
        dx, n1_sums = _norm_bwd(dh, sv["x"], dx2, g1, sc1, "norm1_bwd")

        dmods[l] = jnp.concatenate([n1_sums[0], n1_sums[1], dgt1[0], n2_sums[0], n2_sums[1], dgt2[0]])
        small_g["b_ada"][l] = dmods[l]
        small_g["g_norm1"][l] = n1_sums[2]
        small_g["g_q"][l] = qk_sums[0]
        small_g["g_k"][l] = qk_sums[1]
        small_g["w_conv_dw"][l] = conv_sums[:CONV_K]
        small_g["b_conv_dw"][l] = conv_sums[CONV_K]
        small_g["g_conv_ln"][l] = ln_sums[0]
        small_g["b_conv_ln"][l] = ln_sums[1]
        small_g["g_norm2"][l] = n2_sums[2]
        small_g["w_ffn_dw"][l] = ffn_sums[:FFN_K]
        small_g["b_ffn_dw"][l] = ffn_sums[FFN_K]

    small_g = {n: jnp.stack(v) for n, v in small_g.items()}
    return loss, dx, bufs, small_g, cact


def kernel(x, c, positions, w_ada, b_ada, g_norm1, w_in, g_q, g_k, w_attn_proj, w_conv_dw, b_conv_dw, g_conv_ln, b_conv_ln, w_conv_out, w_o, g_norm2, w_ffn_in, w_ffn_dw, b_ffn_dw, w_ffn_down, loss_target, m_w_ada, m_b_ada, m_g_norm1, m_w_in, m_g_q, m_g_k, m_w_attn_proj, m_w_conv_dw, m_b_conv_dw, m_g_conv_ln, m_b_conv_ln, m_w_conv_out, m_w_o, m_g_norm2, m_w_ffn_in, m_w_ffn_dw, m_b_ffn_dw, m_w_ffn_down, v_w_ada, v_b_ada, v_g_norm1, v_w_in, v_g_q, v_g_k, v_w_attn_proj, v_w_conv_dw, v_b_conv_dw, v_g_conv_ln, v_b_conv_ln, v_w_conv_out, v_w_o, v_g_norm2, v_w_ffn_in, v_w_ffn_dw, v_b_ffn_dw, v_w_ffn_down):
    w = dict(w_ada=w_ada, b_ada=b_ada, g_norm1=g_norm1, w_in=w_in, g_q=g_q, g_k=g_k, w_attn_proj=w_attn_proj,
             w_conv_dw=w_conv_dw, b_conv_dw=b_conv_dw, g_conv_ln=g_conv_ln, b_conv_ln=b_conv_ln,
             w_conv_out=w_conv_out, w_o=w_o, g_norm2=g_norm2, w_ffn_in=w_ffn_in, w_ffn_dw=w_ffn_dw,
             b_ffn_dw=b_ffn_dw, w_ffn_down=w_ffn_down)
    m = dict(w_ada=m_w_ada, b_ada=m_b_ada, g_norm1=m_g_norm1, w_in=m_w_in, g_q=m_g_q, g_k=m_g_k,
             w_attn_proj=m_w_attn_proj, w_conv_dw=m_w_conv_dw, b_conv_dw=m_b_conv_dw, g_conv_ln=m_g_conv_ln,
             b_conv_ln=m_b_conv_ln, w_conv_out=m_w_conv_out, w_o=m_w_o, g_norm2=m_g_norm2, w_ffn_in=m_w_ffn_in,
             w_ffn_dw=m_w_ffn_dw, b_ffn_dw=m_b_ffn_dw, w_ffn_down=m_w_ffn_down)
    v = dict(w_ada=v_w_ada, b_ada=v_b_ada, g_norm1=v_g_norm1, w_in=v_w_in, g_q=v_g_q, g_k=v_g_k,
             w_attn_proj=v_w_attn_proj, w_conv_dw=v_w_conv_dw, b_conv_dw=v_b_conv_dw, g_conv_ln=v_g_conv_ln,
             b_conv_ln=v_b_conv_ln, w_conv_out=v_w_conv_out, w_o=v_w_o, g_norm2=v_g_norm2, w_ffn_in=v_w_ffn_in,
             w_ffn_dw=v_w_ffn_dw, b_ffn_dw=v_b_ffn_dw, w_ffn_down=v_w_ffn_down)
    xi, yi, ci = lax.axis_index("x"), lax.axis_index("y"), lax.axis_index("c")
    j = 2 * xi + yi
    L = w_ada.shape[0]
    D = x.shape[-1]

    mats = ("w_ada",) + BIG
    col = dict(COL_SHARDED, w_ada=True)
    shards = [w[n].astype(BF16) for n in mats]
    conv_shapes = [w["w_conv_dw"].shape, w["w_ffn_dw"].shape]
    gathered, small_g4 = _all_gather_weights(shards, _pack_rows([w["w_conv_dw"], w["w_ffn_dw"]]))
    wf = {}
    for n, g4 in zip(mats, gathered):
        _, _, r, cols = g4.shape
        if col[n]:
            wf[n] = g4.transpose(1, 2, 0, 3).reshape(L, r, 4 * cols)
        else:
            wf[n] = g4.transpose(1, 0, 2, 3).reshape(L, 4 * r, cols)
    conv_parts = [_unpack_rows(small_g4[k], conv_shapes) for k in range(4)]
    small_w = {n: w[n] for n in SMALL}
    small_w["w_conv_dw"] = jnp.concatenate([p[0] for p in conv_parts], axis=2)
    small_w["w_ffn_dw"] = jnp.concatenate([p[1] for p in conv_parts], axis=2)

    loss, dx, bufs, small_g, cact = _local_step(x[0], c, positions[0], loss_target[0], wf, small_w)
    loss = lax.psum(loss, ("x", "y", "c"))

    dws = [bufs[n] for n in BIG]
    land1 = _rs_to_sibling(dws)
    c_idx = jnp.reshape(ci, (1,)).astype(jnp.int32)
    s1s = [_add_halves(a, b, c_idx, "rs_add_halves") for a, b in zip(dws, land1)]
    land2 = _rs_to_chips(s1s)
    cj_idx = jnp.stack([ci, j]).astype(jnp.int32)
    halves = [_add_chips(a, b, e, cj_idx, "rs_add_chips") for a, b, e in zip(dws, land1, land2)]
    joined = _rs_join_halves(halves)
    grads = {n: g.reshape(w[n].shape) for n, g in zip(BIG, joined)}

    part = _pack_rows([small_g[n] for n in SMALL] + [cact])
    parts = _all_gather_small(part)
    full_shapes = [small_g[n].shape for n in SMALL]
    summed = _unpack_rows(_sum_parts(parts), full_shapes)
    for n, g in zip(SMALL, summed):
        if n in ("w_conv_dw", "w_ffn_dw"):
            cols = w[n].shape[2]
            g = lax.dynamic_slice_in_dim(g, j * cols, cols, axis=2)
        grads[n] = g
    flat = parts.reshape(8, -1)
    off, dmod_all = 0, None
    for n, shp in zip(SMALL, full_shapes):
        size = 1
        for s in shp:
            size *= s
        if n == "b_ada":
            dmod_all = flat[:, off:off + size].reshape(8, L, 6 * D)
        off += size
    cact_all = flat[:, off:off + D]
    cols = w_ada.shape[2]
    dmod_cols = lax.dynamic_slice_in_dim(dmod_all, j * cols, cols, axis=2).transpose(1, 0, 2)
    grads["w_ada"] = _wada_grad(cact_all.T, dmod_cols)

    delta, new_m, new_v = {}, {}, {}
    for n in ("w_ada",) + BIG:
        shp = w[n].shape
        two_d = lambda a: a.reshape(shp[0] * shp[1], shp[2])
        dl, mn, vn = _adamw(two_d(w[n]), two_d(grads[n]), two_d(m[n]), two_d(v[n]), "adamw_" + n)
        delta[n], new_m[n], new_v[n] = dl.reshape(shp), mn.reshape(shp), vn.reshape(shp)
    small_shapes = [w[n].shape for n in SMALL]
    packs = [_pack_rows([d[n] for n in SMALL]) for d in (w, grads, m, v)]
    outs = _adamw(*packs, "adamw_small")
    for d, packed in zip((delta, new_m, new_v), outs):
        for n, a in zip(SMALL, _unpack_rows(packed, small_shapes)):
            d[n] = a

    return (loss, dx[None], *[grads[n] for n in WEIGHTS], *[delta[n] for n in WEIGHTS],
            *[new_m[n] for n in WEIGHTS], *[new_v[n] for n in WEIGHTS])
```

```python
import functools

import jax
import jax.numpy as jnp
from jax import lax
from jax.experimental import pallas as pl
from jax.experimental.pallas import tpu as pltpu

F32 = jnp.float32
BF16 = jnp.bfloat16
SDS = jax.ShapeDtypeStruct
MESH = pl.DeviceIdType.MESH

HEAD_DIM = 128
BLOCK = 128
HEADS_PER_GROUP = 4
GROUP_W = HEADS_PER_GROUP * HEAD_DIM
DILATIONS = (1, 4, 16)
ATTN_W = len(DILATIONS) * GROUP_W
ROT_DIM = HEAD_DIM // 4
ROPE_THETA = 500000.0
CONV_K = 31
CONV_HALO = 32
FFN_K = 3
FFN_HALO = 8
EPS = 1e-6
NEG = -1e30
CHUNK = 512

ADAM_LR = 0.001
ADAM_B1 = 0.9
ADAM_B2 = 0.999
ADAM_EPS = 1e-08
ADAM_WD = 0.01
ADAM_STEP = 10

VMEM_LIMIT_BYTES = 48 * 1024 * 1024
TILE_BYTES = 2 * 1024 * 1024

NT_DIMS = (((1,), (1,)), ((), ()))
TN_DIMS = (((0,), (0,)), ((), ()))


def _cp(*sem):
    return pltpu.CompilerParams(dimension_semantics=sem if sem else None,
                                vmem_limit_bytes=VMEM_LIMIT_BYTES)


def _tm(s, cap=512):
    return min(cap, s)


def _row_tile(rows, cols, itemsize=4, step=8):
    best = None
    for t in range(step, rows + 1, step):
        if rows % t == 0 and t * cols * itemsize <= TILE_BYTES:
            best = t
    return best if best is not None else rows


def _quarter(n):
    return n // 4 if n % (4 * HEAD_DIM) == 0 else CHUNK


def _sigmoid(v):
    return jax.nn.sigmoid(v)


def _mod_fwd(c, w_ada, b_ada):
    L, D, N6 = w_ada.shape
    tn = N6 // 4

    def body(c_ref, w_ref, b_ref, mod_ref, cact_ref):
        cv = c_ref[...]
        ca = cv * _sigmoid(cv)
        cact_ref[...] = ca
        a8 = jnp.broadcast_to(ca, (8, D)).astype(BF16)
        acc = jnp.dot(a8, w_ref[0], preferred_element_type=F32)
        mod_ref[0] = acc[0:1, :] + b_ref[0]

    mod, cact = pl.pallas_call(
        body, name="mod_fwd", grid=(L, 4),
        in_specs=[pl.BlockSpec((1, D), lambda l, j: (0, 0)),
                  pl.BlockSpec((1, D, tn), lambda l, j: (l, 0, j)),
                  pl.BlockSpec((1, 1, tn), lambda l, j: (l, 0, j))],
        out_specs=[pl.BlockSpec((1, 1, tn), lambda l, j: (l, 0, j)),
                   pl.BlockSpec((1, D), lambda l, j: (0, 0))],
        out_shape=[SDS((L, 1, N6), F32), SDS((1, D), F32)],
        compiler_params=_cp("arbitrary", "arbitrary"),
    )(c, w_ada, b_ada.reshape(L, 1, N6))
    return mod.reshape(L, N6), cact


def _normmod_mm(x, g, sc, sh, w, l, name):
    S, D = x.shape
    N = w.shape[2]
    tm, tn = _tm(S, 1024), _quarter(N)

    def body(x_ref, g_ref, sc_ref, sh_ref, w_ref, z_ref, h_ref, hs):
        @pl.when(pl.program_id(1) == 0)
        def _():
            xv = x_ref[...]
            r = lax.rsqrt(jnp.mean(xv * xv, axis=-1, keepdims=True) + EPS)
            hv = (xv * r) * g_ref[...] * (1.0 + sc_ref[...]) + sh_ref[...]
            hs[...] = hv.astype(BF16)
            h_ref[...] = hs[...]
        z_ref[...] = jnp.dot(hs[...], w_ref[0], preferred_element_type=F32)

    vec = pl.BlockSpec((1, D), lambda i, j: (0, 0))
    return pl.pallas_call(
        body, name=name, grid=(S // tm, N // tn),
        in_specs=[pl.BlockSpec((tm, D), lambda i, j: (i, 0)), vec, vec, vec,
                  pl.BlockSpec((1, D, tn), lambda i, j: (l, 0, j))],
        out_specs=[pl.BlockSpec((tm, tn), lambda i, j: (i, j)),
                   pl.BlockSpec((tm, D), lambda i, j: (i, 0))],
        out_shape=[SDS((S, N), F32), SDS((S, D), BF16)],
        scratch_shapes=[pltpu.VMEM((tm, D), BF16)],
        compiler_params=_cp("parallel", "arbitrary"),
    )(x, g, sc, sh, w)


def _rope(t, cos_t, sa_t, sb_t):
    return t * cos_t + pltpu.roll(t, HEAD_DIM - ROT_DIM // 2, 1) * sa_t + pltpu.roll(t, ROT_DIM // 2, 1) * sb_t


def _unrope(d, cos_t, sa_t, sb_t):
    return d * cos_t + pltpu.roll(d * sa_t, ROT_DIM // 2, 1) + pltpu.roll(d * sb_t, HEAD_DIM - ROT_DIM // 2, 1)


def _dil_spec(d, tm):
    return pl.BlockSpec((d, tm // d, GROUP_W), lambda i: (0, i, 0))


def _to_dilated(scr, val, dst_ref, d, cols):
    if d == 1:
        dst_ref[0, :, cols] = val.astype(dst_ref.dtype)
        return
    n = val.shape[0] // d
    scr[...] = val
    for r in range(d):
        dst_ref[r, :, cols] = scr[pl.ds(r, n, stride=d), :].astype(dst_ref.dtype)


def _from_dilated(scr, src_ref, d, cols):
    if d == 1:
        return src_ref[0, :, cols].astype(F32)
    n = src_ref.shape[1]
    for r in range(d):
        scr[pl.ds(r, n, stride=d), :] = src_ref[r, :, cols].astype(F32)
    return scr[...]


def _qk_prep(z, g_q, g_k, tabs):
    S = z.shape[0]
    tm = _tm(S)
    n_heads = ATTN_W // HEAD_DIM
    ng = len(DILATIONS)

    def body(q_ref, k_ref, v_ref, gq_ref, gk_ref, c_ref, sa_ref, sb_ref, *rest):
        outs, scr = rest[:3 * ng], rest[3 * ng]
        cos_t, sa_t, sb_t = c_ref[...], sa_ref[...], sb_ref[...]
        for wi, (src, gref) in enumerate(((q_ref, gq_ref), (k_ref, gk_ref), (v_ref, None))):
            for h in range(n_heads):
                gi = h // HEADS_PER_GROUP
                cols = slice((h % HEADS_PER_GROUP) * HEAD_DIM, (h % HEADS_PER_GROUP + 1) * HEAD_DIM)
                t = src[:, h * HEAD_DIM:(h + 1) * HEAD_DIM]
                if gref is not None:
                    r = lax.rsqrt(jnp.mean(t * t, axis=-1, keepdims=True) + EPS)
                    t = _rope((t * r) * gref[...], cos_t, sa_t, sb_t)
                _to_dilated(scr, t, outs[wi * ng + gi], DILATIONS[gi], cols)

    tab = pl.BlockSpec((tm, HEAD_DIM), lambda i: (i, 0))
    gsp = pl.BlockSpec((1, HEAD_DIM), lambda i: (0, 0))
    wide = lambda cb: pl.BlockSpec((tm, ATTN_W), lambda i: (i, cb))
    out = pl.pallas_call(
        body, name="qk_prep", grid=(S // tm,),
        in_specs=[wide(0), wide(1), wide(2), gsp, gsp, tab, tab, tab],
        out_specs=[_dil_spec(d, tm) for _ in range(3) for d in DILATIONS],
        out_shape=[SDS((d, S // d, GROUP_W), BF16) for _ in range(3) for d in DILATIONS],
        scratch_shapes=[pltpu.VMEM((tm, HEAD_DIM), F32)],
        compiler_params=_cp("parallel"),
    )(z, z, z, g_q, g_k, *tabs)
    return out[:ng], out[ng:2 * ng], out[2 * ng:]


def _attn_blk(nb, shift):
    if shift < 0:
        return pl.BlockSpec((None, BLOCK, GROUP_W), lambda r, n: (r, jnp.maximum(n - 1, 0), 0))
    if shift > 0:
        return pl.BlockSpec((None, BLOCK, GROUP_W), lambda r, n: (r, jnp.minimum(n + 1, nb - 1), 0))
    return pl.BlockSpec((None, BLOCK, GROUP_W), lambda r, n: (r, n, 0))


def _band_mask(has_prev):
    qi = lax.broadcasted_iota(jnp.int32, (BLOCK, 2 * BLOCK), 0)
    kj = lax.broadcasted_iota(jnp.int32, (BLOCK, 2 * BLOCK), 1)
    band = jnp.logical_and(kj >= qi, kj <= qi + BLOCK)
    return jnp.logical_and(band, jnp.logical_or(has_prev, kj >= BLOCK))


def _attn_fwd(q, k, v, gi):
    d, Sd, _ = q.shape
    nb = Sd // BLOCK
    scale = HEAD_DIM ** -0.5

    def body(q_ref, kc_ref, kp_ref, vc_ref, vp_ref, o_ref, lse_ref, kk, vv):
        n = pl.program_id(1)
        kk[0:BLOCK, :], kk[BLOCK:, :] = kp_ref[...], kc_ref[...]
        vv[0:BLOCK, :], vv[BLOCK:, :] = vp_ref[...], vc_ref[...]
        mask = _band_mask(n > 0)
        for h in range(HEADS_PER_GROUP):
            sl = slice(h * HEAD_DIM, (h + 1) * HEAD_DIM)
            s = lax.dot_general(q_ref[:, sl], kk[:, sl], NT_DIMS, preferred_element_type=F32) * scale
            s = jnp.where(mask, s, NEG)
            m = jnp.max(s, axis=-1, keepdims=True)
            p = jnp.exp(s - m)
            den = jnp.sum(p, axis=-1, keepdims=True)
            acc = jnp.dot(p.astype(BF16), vv[:, sl], preferred_element_type=F32)
            o_ref[:, sl] = acc / den
            lse_ref[:, sl] = jnp.broadcast_to(m + jnp.log(den), (BLOCK, HEAD_DIM))

    cur, prev = _attn_blk(nb, 0), _attn_blk(nb, -1)
    return pl.pallas_call(
        body, name=f"attn_fwd_g{gi}", grid=(d, nb),
        in_specs=[cur, cur, prev, cur, prev], out_specs=[cur] * 2,
        out_shape=[SDS((d, Sd, GROUP_W), F32)] * 2,
        scratch_shapes=[pltpu.VMEM((2 * BLOCK, GROUP_W), BF16)] * 2,
        compiler_params=_cp("parallel", "parallel"),
    )(q, k, k, v, v)


def _group_weights(lses):
    m = jnp.maximum(jnp.maximum(lses[0], lses[1]), lses[2])
    es = [jnp.exp(v - m) for v in lses]
    inv = 1.0 / (es[0] + es[1] + es[2])
    return [e * inv for e in es]


def _attn_combine(outs, lses):
    S = outs[0].shape[0] * outs[0].shape[1]
    tm = _tm(S)
    ng = len(DILATIONS)

    def body(*refs):
        o_refs, l_refs, a_ref, scr = refs[:ng], refs[ng:2 * ng], refs[2 * ng], refs[2 * ng + 1:]
        for h in range(HEADS_PER_GROUP):
            cols = slice(h * HEAD_DIM, (h + 1) * HEAD_DIM)
            ov = [_from_dilated(scr[g], o_refs[g], DILATIONS[g], cols) for g in range(ng)]
            lv = [_from_dilated(scr[ng + g], l_refs[g], DILATIONS[g], cols) for g in range(ng)]
            w = _group_weights(lv)
            a_ref[:, cols] = (w[0] * ov[0] + w[1] * ov[1] + w[2] * ov[2]).astype(BF16)

    dil = [_dil_spec(d, tm) for d in DILATIONS]
    return pl.pallas_call(
        body, name="attn_combine", grid=(S // tm,), in_specs=dil + dil,
        out_specs=pl.BlockSpec((tm, GROUP_W), lambda i: (i, 0)),
        out_shape=SDS((S, GROUP_W), BF16),
        scratch_shapes=[pltpu.VMEM((tm, HEAD_DIM), F32)] * (2 * ng),
        compiler_params=_cp("parallel"),
    )(*outs, *lses)


def _mm_nn(a, w, l, name, res=None, gate=None):
    S, K = a.shape
    N = w.shape[2]
    tm = _tm(S)
    gated = res is not None

    def body(*refs):
        if gated:
            a_ref, w_ref, r_ref, g_ref, t_ref, o_ref = refs
        else:
            a_ref, w_ref, t_ref = refs
        t = jnp.dot(a_ref[...], w_ref[0], preferred_element_type=F32)
        t_ref[...] = t
        if gated:
            o_ref[...] = r_ref[...] + g_ref[...] * t

    row = pl.BlockSpec((tm, N), lambda i: (i, 0))
    in_specs = [pl.BlockSpec((tm, K), lambda i: (i, 0)), pl.BlockSpec((1, K, N), lambda i: (l, 0, 0))]
    args = [a, w]
    if gated:
        in_specs += [row, pl.BlockSpec((1, N), lambda i: (0, 0))]
        args += [res, gate]
    n_out = 2 if gated else 1
    out = pl.pallas_call(
        body, name=name, grid=(S // tm,), in_specs=in_specs, out_specs=[row] * n_out,
        out_shape=[SDS((S, N), F32)] * n_out, compiler_params=_cp("parallel"),
    )(*args)
    return out if gated else out[0]


def _glu(cv, cg):
    return cv * _sigmoid(cg)


STRIP = 16
SHIFTS = 8


def _fill_shifted(sh, ext, rows):
    sh[0] = ext[...]
    for b in range(1, SHIFTS):
        sh[b, 0:rows, :] = ext[pl.ds(b, rows), :]


def _tap(sh, base, off, rows=STRIP):
    return sh[off % SHIFTS, pl.ds(base + SHIFTS * (off // SHIFTS), rows), :]


def _glu_conv(z, w_dw, b_dw, l, D):
    S = z.shape[0]
    tm = _tm(S)
    cc0 = 3 * ATTN_W // CHUNK
    ncc = D // CHUNK
    hb = tm // CONV_HALO
    lead = CONV_HALO - (CONV_K - 1)

    def body(cv_ref, cg_ref, hv_ref, hg_ref, w_ref, b_ref, o_ref, ext, sh):
        i = pl.program_id(1)
        ext[0:CONV_HALO, :] = jnp.where(i > 0, _glu(hv_ref[...], hg_ref[...]), 0.0)
        ext[CONV_HALO:, :] = _glu(cv_ref[...], cg_ref[...])
        _fill_shifted(sh, ext, tm + CONV_HALO - SHIFTS)

        def strip(s, carry):
            base = pl.multiple_of(s * STRIP, STRIP)
            acc = jnp.broadcast_to(b_ref[...], (STRIP, CHUNK))
            for k in range(CONV_K):
                acc = acc + w_ref[0, k:k + 1, :] * _tap(sh, base, lead + k)
            o_ref[pl.ds(base, STRIP), :] = acc
            return carry

        lax.fori_loop(0, tm // STRIP, strip, 0)

    cur = lambda off: pl.BlockSpec((tm, CHUNK), lambda cc, i: (i, cc0 + off + cc))
    halo = lambda off: pl.BlockSpec((CONV_HALO, CHUNK), lambda cc, i: (jnp.maximum(i * hb - 1, 0), cc0 + off + cc))
    return pl.pallas_call(
        body, name="glu_conv", grid=(ncc, S // tm),
        in_specs=[cur(0), cur(ncc), halo(0), halo(ncc),
                  pl.BlockSpec((1, CONV_K, CHUNK), lambda cc, i: (l, 0, cc)),
                  pl.BlockSpec((None, 1, CHUNK), lambda cc, i: (l, 0, cc))],
        out_specs=pl.BlockSpec((tm, CHUNK), lambda cc, i: (i, cc)),
        out_shape=SDS((S, D), F32),
        scratch_shapes=[pltpu.VMEM((tm + CONV_HALO, CHUNK), F32),
                        pltpu.VMEM((SHIFTS, tm + CONV_HALO, CHUNK), F32)],
        compiler_params=_cp("parallel", "parallel"),
    )(z, z, z, z, w_dw, b_dw[:, None, :])


def _ln_stats(u):
    mu = jnp.mean(u, axis=-1, keepdims=True)
    xc = u - mu
    rstd = lax.rsqrt(jnp.mean(xc * xc, axis=-1, keepdims=True) + EPS)
    return xc * rstd, rstd


def _ln_silu(u1, g, b):
    S, D = u1.shape
    tm = _tm(S)

    def body(u_ref, g_ref, b_ref, o_ref):
        xh, _ = _ln_stats(u_ref[...])
        yv = xh * g_ref[...] + b_ref[...]
        o_ref[...] = (yv * _sigmoid(yv)).astype(BF16)

    vec = pl.BlockSpec((1, D), lambda i: (0, 0))
    return pl.pallas_call(
        body, name="ln_silu", grid=(S // tm,),
        in_specs=[pl.BlockSpec((tm, D), lambda i: (i, 0)), vec, vec],
        out_specs=pl.BlockSpec((tm, D), lambda i: (i, 0)),
        out_shape=SDS((S, D), BF16), compiler_params=_cp("parallel"),
    )(u1, g, b)


def _merge(z, y_a, y_b, D):
    S = z.shape[0]
    tm = _tm(S)
    ncc = D // CHUNK
    ga0 = (3 * ATTN_W + 2 * D) // CHUNK

    def body(ga_ref, gb_ref, ya_ref, yb_ref, o_ref):
        o_ref[...] = (_sigmoid(ga_ref[...]) * ya_ref[...] + _sigmoid(gb_ref[...]) * yb_ref[...]).astype(BF16)

    ysp = pl.BlockSpec((tm, CHUNK), lambda i, cc: (i, cc))
    return pl.pallas_call(
        body, name="merge", grid=(S // tm, ncc),
        in_specs=[pl.BlockSpec((tm, CHUNK), lambda i, cc: (i, ga0 + cc)),
                  pl.BlockSpec((tm, CHUNK), lambda i, cc: (i, ga0 + ncc + cc)), ysp, ysp],
        out_specs=ysp, out_shape=SDS((S, D), BF16), compiler_params=_cp("parallel", "parallel"),
    )(z, z, y_a, y_b)


def _ffn_chunk(F):
    best = 128
    for t in range(128, min(F, 1536) + 1, 128):
        if F % t == 0:
            best = t
    return best


FFN_ROWS = 32


def _ffn_shift(sh, ext, rows):
    for k in range(FFN_K - 1):
        sh[k, 0:rows, :] = ext[pl.ds(FFN_HALO - (FFN_K - 1) + k, rows), :]


def _ffn_taps(sh, ext, base, ls):
    rows = pl.ds(base, FFN_ROWS)
    return [sh[k, rows, ls] for k in range(FFN_K - 1)] + [ext[pl.ds(base + FFN_HALO, FFN_ROWS), ls]]


def _ffn_act(gu, w_dw, b_dw, l):
    S, F2 = gu.shape
    F = F2 // 2
    cw = _ffn_chunk(F)
    ncc = F // cw
    tm = _tm(S)
    hb = tm // FFN_HALO

    def body(g_ref, u_ref, h_ref, w_ref, b_ref, o_ref, ext, sh):
        i = pl.program_id(1)
        ext[0:FFN_HALO, :] = jnp.where(i > 0, h_ref[...], 0.0)
        ext[FFN_HALO:, :] = g_ref[...]
        _ffn_shift(sh, ext, tm)
        for cg in range(cw // HEAD_DIM):
            ls = slice(cg * HEAD_DIM, (cg + 1) * HEAD_DIM)
            wv = [w_ref[0, k:k + 1, ls] for k in range(FFN_K)]
            bv = b_ref[:, ls]

            def strip(s, carry, ls=ls, wv=wv, bv=bv):
                base = pl.multiple_of(s * FFN_ROWS, FFN_ROWS)
                rows = pl.ds(base, FFN_ROWS)
                taps = _ffn_taps(sh, ext, base, ls)
                gp = bv + wv[0] * taps[0] + wv[1] * taps[1] + wv[2] * taps[2]
                o_ref[rows, ls] = (gp * _sigmoid(gp) * u_ref[rows, ls]).astype(BF16)
                return carry

            lax.fori_loop(0, tm // FFN_ROWS, strip, 0)

    return pl.pallas_call(
        body, name="ffn_act", grid=(ncc, S // tm),
        in_specs=[pl.BlockSpec((tm, cw), lambda cc, i: (i, cc)),
                  pl.BlockSpec((tm, cw), lambda cc, i: (i, ncc + cc)),
                  pl.BlockSpec((FFN_HALO, cw), lambda cc, i: (jnp.maximum(i * hb - 1, 0), cc)),
                  pl.BlockSpec((1, FFN_K, cw), lambda cc, i: (l, 0, cc)),
                  pl.BlockSpec((None, 1, cw), lambda cc, i: (l, 0, cc))],
        out_specs=pl.BlockSpec((tm, cw), lambda cc, i: (i, cc)),
        out_shape=SDS((S, F), BF16),
        scratch_shapes=[pltpu.VMEM((tm + FFN_HALO, cw), F32), pltpu.VMEM((FFN_K - 1, tm, cw), F32)],
        compiler_params=_cp("parallel", "parallel"),
    )(gu, gu, gu, w_dw, b_dw[:, None, :])


def _silu_grad(gp, sg):
    return sg * (1.0 + gp * (1.0 - sg))


def _ffn_act_bwd(da, gu, w_dw, b_dw, l):
    S, F2 = gu.shape
    F = F2 // 2
    cw = _ffn_chunk(F)
    ncc = F // cw
    tm = _tm(S, 256)
    ni = S // tm
    hb = tm // FFN_HALO
    last_hb = S // FFN_HALO - 1

    def body(da_ref, dan_ref, g_ref, gh_ref, gn_ref, u_ref, un_ref, w_ref, b_ref, dg_ref, du_ref, sums_ref,
             ext, sh, dgp, dsh):
        i = pl.program_id(1)

        @pl.when(i == 0)
        def _():
            sums_ref[...] = jnp.zeros_like(sums_ref)
        ext[0:FFN_HALO, :] = jnp.where(i > 0, gh_ref[...], 0.0)
        ext[FFN_HALO:FFN_HALO + tm, :] = g_ref[...]
        ext[FFN_HALO + tm:, :] = gn_ref[...]
        _ffn_shift(sh, ext, tm + FFN_HALO)
        tail = pl.ds(tm, FFN_HALO)
        gp_t = b_ref[...]
        for k, tap in enumerate((sh[0, tail, :], sh[1, tail, :], ext[pl.ds(tm + FFN_HALO, FFN_HALO), :])):
            gp_t = gp_t + w_ref[0, k:k + 1, :] * tap
        d_t = dan_ref[...] * un_ref[...] * _silu_grad(gp_t, _sigmoid(gp_t))
        dgp[tail, :] = jnp.where(i < ni - 1, d_t, 0.0)
        for cg in range(cw // HEAD_DIM):
            ls = slice(cg * HEAD_DIM, (cg + 1) * HEAD_DIM)
            wv = [w_ref[0, k:k + 1, ls] for k in range(FFN_K)]
            bv = b_ref[:, ls]

            def strip(s, accs, ls=ls, wv=wv, bv=bv):
                base = pl.multiple_of(s * FFN_ROWS, FFN_ROWS)
                rows = pl.ds(base, FFN_ROWS)
                taps = _ffn_taps(sh, ext, base, ls)
                gp = bv + wv[0] * taps[0] + wv[1] * taps[1] + wv[2] * taps[2]
                sg = _sigmoid(gp)
                dav = da_ref[rows, ls]
                du_ref[rows, ls] = (dav * gp * sg).astype(BF16)
                dv = dav * u_ref[rows, ls] * _silu_grad(gp, sg)
                dgp[rows, ls] = dv
                return tuple(a + dv * t for a, t in zip(accs[:FFN_K], taps)) + (accs[FFN_K] + dv,)

            accs = lax.fori_loop(0, tm // FFN_ROWS, strip,
                                 tuple(jnp.zeros((FFN_ROWS, HEAD_DIM), F32) for _ in range(FFN_K + 1)))
            for k, a in enumerate(accs):
                sums_ref[k:k + 1, ls] += jnp.sum(a, axis=0, keepdims=True)
        for b in range(1, FFN_K):
            dsh[b - 1] = dgp[pl.ds(b, tm), :]
        for cg in range(cw // HEAD_DIM):
            ls = slice(cg * HEAD_DIM, (cg + 1) * HEAD_DIM)
            wv = [w_ref[0, k:k + 1, ls] for k in range(FFN_K)]

            def strip2(s, carry, ls=ls, wv=wv):
                rows = pl.ds(pl.multiple_of(s * FFN_ROWS, FFN_ROWS), FFN_ROWS)
                dg_ref[rows, ls] = (wv[0] * dsh[1, rows, ls] + wv[1] * dsh[0, rows, ls]
                                    + wv[2] * dgp[rows, ls]).astype(BF16)
                return carry

            lax.fori_loop(0, tm // FFN_ROWS, strip2, 0)

    cur = lambda off: pl.BlockSpec((tm, cw), lambda cc, i: (i, off + cc))
    nxt = lambda off: pl.BlockSpec((FFN_HALO, cw), lambda cc, i: (jnp.minimum((i + 1) * hb, last_hb), off + cc))
    return pl.pallas_call(
        body, name="ffn_act_bwd", grid=(ncc, ni),
        in_specs=[cur(0), nxt(0), cur(0),
                  pl.BlockSpec((FFN_HALO, cw), lambda cc, i: (jnp.maximum(i * hb - 1, 0), cc)), nxt(0),
                  cur(ncc), nxt(ncc),
                  pl.BlockSpec((1, FFN_K, cw), lambda cc, i: (l, 0, cc)),
                  pl.BlockSpec((None, 1, cw), lambda cc, i: (l, 0, cc))],
        out_specs=[cur(0), cur(0), pl.BlockSpec((8, cw), lambda cc, i: (0, cc))],
        out_shape=[SDS((S, F), BF16), SDS((S, F), BF16), SDS((8, F), F32)],
        scratch_shapes=[pltpu.VMEM((tm + 2 * FFN_HALO, cw), F32), pltpu.VMEM((FFN_K - 1, tm + FFN_HALO, cw), F32),
                        pltpu.VMEM((tm + FFN_HALO, cw), F32), pltpu.VMEM((FFN_K - 1, tm, cw), F32)],
        compiler_params=_cp("parallel", "arbitrary"),
    )(da, da, gu, gu, gu, gu, gu, w_dw, b_dw[:, None, :])


def _loss_grad(y, target):
    S, D = y.shape
    tm = _tm(S)

    def body(y_ref, t_ref, dy_ref, l_ref):
        @pl.when(pl.program_id(0) == 0)
        def _():
            l_ref[...] = jnp.zeros_like(l_ref)
        e = y_ref[...] - t_ref[...]
        dy_ref[...] = e * (1.0 / D)
        l_ref[...] += jnp.broadcast_to(0.5 * jnp.sum(jnp.mean(e * e, axis=-1, keepdims=True)), (1, HEAD_DIM))

    row = pl.BlockSpec((tm, D), lambda i: (i, 0))
    dy, lsum = pl.pallas_call(
        body, name="loss_grad", grid=(S // tm,), in_specs=[row, row],
        out_specs=[row, pl.BlockSpec((1, HEAD_DIM), lambda i: (0, 0))],
        out_shape=[SDS((S, D), F32), SDS((1, HEAD_DIM), F32)],
        compiler_params=_cp("arbitrary"),
    )(y, target)
    return dy, lsum[0, 0]


def _gate_bwd(dx, t, gate, name):
    S, D = dx.shape
    tm = _tm(S)

    def body(dx_ref, t_ref, g_ref, e_ref, dg_ref):
        @pl.when(pl.program_id(0) == 0)
        def _():
            dg_ref[...] = jnp.zeros_like(dg_ref)
        dxv = dx_ref[...]
        e_ref[...] = (dxv * g_ref[...]).astype(BF16)
        dg_ref[...] += jnp.sum(dxv * t_ref[...], axis=0, keepdims=True)

    row = pl.BlockSpec((tm, D), lambda i: (i, 0))
    vec = pl.BlockSpec((1, D), lambda i: (0, 0))
    return pl.pallas_call(
        body, name=name, grid=(S // tm,), in_specs=[row, row, vec], out_specs=[row, vec],
        out_shape=[SDS((S, D), BF16), SDS((1, D), F32)], compiler_params=_cp("arbitrary"),
    )(dx, t, gate)


def _mm_nt(dy, w, l, name):
    parts = dy if isinstance(dy, tuple) else (dy,)
    S, cp = parts[0].shape
    C = cp * len(parts)
    K = w.shape[1]
    tm = _tm(S, 1024 if K <= 1024 else 512)
    tc = _quarter(C) if C > 1024 else C
    nc = C // tc
    per = cp // tc

    def body(*refs):
        p_refs, (w_ref, o_ref, acc) = refs[:len(parts)], refs[len(parts):]
        kk = pl.program_id(1)

        @pl.when(kk == 0)
        def _():
            acc[...] = jnp.zeros_like(acc)
        for p, p_ref in enumerate(p_refs):
            @pl.when(jnp.logical_and(kk >= p * per, kk < (p + 1) * per))
            def _(p_ref=p_ref):
                acc[...] += lax.dot_general(p_ref[...], w_ref[0], NT_DIMS, preferred_element_type=F32)

        @pl.when(kk == nc - 1)
        def _():
            o_ref[...] = acc[...]

    part_spec = lambda p: pl.BlockSpec((tm, tc), lambda i, kk: (i, jnp.clip(kk - p * per, 0, per - 1)))
    return pl.pallas_call(
        body, name=name, grid=(S // tm, nc),
        in_specs=[part_spec(p) for p in range(len(parts))]
        + [pl.BlockSpec((1, K, tc), lambda i, kk: (l, 0, kk))],
        out_specs=pl.BlockSpec((tm, K), lambda i, kk: (i, 0)),
        out_shape=SDS((S, K), F32),
        scratch_shapes=[pltpu.VMEM((tm, K), F32)],
        compiler_params=_cp("parallel", "arbitrary"),
    )(*parts, w)


def _mm_tn(a, b, buf, l, n_layers, col_sharded, name):
    parts = b if isinstance(b, tuple) else (b,)
    S, K = a.shape
    N = parts[0].shape[1] * len(parts)
    ts = _tm(S, 1024)
    ns = S // ts
    if col_sharded:
        cols, r2 = N // 4, K // 2
        grid = (4, ns)
        spp = 4 // len(parts)

        def body(*refs):
            a_ref, b_refs, o_ref = refs[0], refs[1:1 + len(parts)], refs[-1]
            jj, s = pl.program_id(0), pl.program_id(1)

            @pl.when(s == 0)
            def _():
                o_ref[...] = jnp.zeros_like(o_ref)
            for p, b_ref in enumerate(b_refs):
                @pl.when(jj // spp == p)
                def _(b_ref=b_ref):
                    for hh in range(2):
                        o_ref[hh] += lax.dot_general(a_ref[:, hh * r2:(hh + 1) * r2], b_ref[...], TN_DIMS,
                                                     preferred_element_type=F32)

        part_spec = lambda p: pl.BlockSpec(
            (ts, cols), lambda jj, s: (jnp.where(jj // spp == p, s, 0), jnp.clip(jj - p * spp, 0, spp - 1)))
        in_specs = [pl.BlockSpec((ts, K), lambda jj, s: (s, 0))] + [part_spec(p) for p in range(len(parts))]
        out_spec = pl.BlockSpec((None, None, 2, r2, cols), lambda jj, s: (l, jj, 0, 0, 0))
        scratch = []
    else:
        cols, r2 = N, K // 8
        tn = CHUNK if N % CHUNK == 0 else N
        grid = (N // tn, ns)

        def body(*refs):
            a_ref, b_ref, o_ref, acc = refs[0], refs[1], refs[-2], refs[-1]
            s = pl.program_id(1)

            @pl.when(s == 0)
            def _():
                acc[...] = jnp.zeros_like(acc)
            acc[...] += lax.dot_general(a_ref[...], b_ref[...], TN_DIMS, preferred_element_type=F32)

            @pl.when(s == ns - 1)
            def _():
                for jj in range(4):
                    for hh in range(2):
                        o_ref[jj, hh] = acc[(2 * jj + hh) * r2:(2 * jj + hh + 1) * r2, :]

        in_specs = [pl.BlockSpec((ts, K), lambda nb, s: (s, 0)), pl.BlockSpec((ts, tn), lambda nb, s: (s, nb))]
        out_spec = pl.BlockSpec((None, 4, 2, r2, tn), lambda nb, s: (l, 0, 0, 0, nb))
        scratch = [pltpu.VMEM((K, tn), F32)]
    args = [a, *parts]
    aliases = {}
    if buf is not None:
        in_specs = in_specs + [pl.BlockSpec(memory_space=pl.ANY)]
        args.append(buf)
        aliases = {len(args) - 1: 0}
    return pl.pallas_call(
        body, name=name, grid=grid, in_specs=in_specs, out_specs=out_spec,
        out_shape=SDS((n_layers, 4, 2, r2, cols), F32), scratch_shapes=scratch,
        input_output_aliases=aliases, compiler_params=_cp("parallel", "arbitrary"),
    )(*args)


def _norm_bwd(dh, x, dres, g, sc, name):
    S, D = x.shape
    tm = _tm(S)

    def body(dh_ref, x_ref, dr_ref, g_ref, sc_ref, dx_ref, sums_ref):
        @pl.when(pl.program_id(0) == 0)
        def _():
            sums_ref[...] = jnp.zeros_like(sums_ref)
        xv, dhv = x_ref[...], dh_ref[...]
        r = lax.rsqrt(jnp.mean(xv * xv, axis=-1, keepdims=True) + EPS)
        xn = xv * r
        gv, sv = g_ref[...], 1.0 + sc_ref[...]
        dxn = dhv * gv * sv
        dx_ref[...] = dr_ref[...] + r * (dxn - xn * jnp.mean(dxn * xn, axis=-1, keepdims=True))
        dhxn = dhv * xn
        sums_ref[0:1, :] += jnp.sum(dhv, axis=0, keepdims=True)
        sums_ref[1:2, :] += jnp.sum(dhxn * gv, axis=0, keepdims=True)
        sums_ref[2:3, :] += jnp.sum(dhxn * sv, axis=0, keepdims=True)

    row = pl.BlockSpec((tm, D), lambda i: (i, 0))
    vec = pl.BlockSpec((1, D), lambda i: (0, 0))
    return pl.pallas_call(
        body, name=name, grid=(S // tm,), in_specs=[row, row, row, vec, vec],
        out_specs=[row, pl.BlockSpec((8, D), lambda i: (0, 0))],
        out_shape=[SDS((S, D), F32), SDS((8, D), F32)], compiler_params=_cp("arbitrary"),
    )(dh, x, dres, g, sc)


def _merge_bwd(dm, z, y_a, y_b, D):
    S = z.shape[0]
    tm = _tm(S)
    ncc = D // CHUNK
    ga0 = (3 * ATTN_W + 2 * D) // CHUNK

    def body(dm_ref, ga_ref, gb_ref, ya_ref, yb_ref, dya_ref, dyb_ref, dga_ref, dgb_ref):
        dmv = dm_ref[...]
        sa, sb = _sigmoid(ga_ref[...]), _sigmoid(gb_ref[...])
        dya_ref[...] = (dmv * sa).astype(BF16)
        dyb_ref[...] = (dmv * sb).astype(BF16)
        dga_ref[...] = (dmv * ya_ref[...] * sa * (1.0 - sa)).astype(BF16)
        dgb_ref[...] = (dmv * yb_ref[...] * sb * (1.0 - sb)).astype(BF16)

    ysp = pl.BlockSpec((tm, CHUNK), lambda i, cc: (i, cc))
    return pl.pallas_call(
        body, name="merge_bwd", grid=(S // tm, ncc),
        in_specs=[ysp, pl.BlockSpec((tm, CHUNK), lambda i, cc: (i, ga0 + cc)),
                  pl.BlockSpec((tm, CHUNK), lambda i, cc: (i, ga0 + ncc + cc)), ysp, ysp],
        out_specs=[ysp] * 4, out_shape=[SDS((S, D), BF16)] * 4,
        compiler_params=_cp("parallel", "parallel"),
    )(dm, z, z, y_a, y_b)


def _ln_silu_bwd(du3, u1, g, b):
    S, D = u1.shape
    tm = _tm(S)

    def body(d_ref, u_ref, g_ref, b_ref, du_ref, sums_ref):
        @pl.when(pl.program_id(0) == 0)
        def _():
            sums_ref[...] = jnp.zeros_like(sums_ref)
        xh, rstd = _ln_stats(u_ref[...])
        gv = g_ref[...]
        yv = xh * gv + b_ref[...]
        sg = _sigmoid(yv)
        dy = d_ref[...] * (sg * (1.0 + yv * (1.0 - sg)))
        dxh = dy * gv
        du_ref[...] = rstd * (dxh - jnp.mean(dxh, axis=-1, keepdims=True)
                              - xh * jnp.mean(dxh * xh, axis=-1, keepdims=True))
        sums_ref[0:1, :] += jnp.sum(dy * xh, axis=0, keepdims=True)
        sums_ref[1:2, :] += jnp.sum(dy, axis=0, keepdims=True)

    row = pl.BlockSpec((tm, D), lambda i: (i, 0))
    vec = pl.BlockSpec((1, D), lambda i: (0, 0))
    return pl.pallas_call(
        body, name="ln_silu_bwd", grid=(S // tm,), in_specs=[row, row, vec, vec],
        out_specs=[row, pl.BlockSpec((8, D), lambda i: (0, 0))],
        out_shape=[SDS((S, D), F32), SDS((8, D), F32)], compiler_params=_cp("arbitrary"),
    )(du3, u1, g, b)


def _glu_conv_bwd(du1, z, w_dw, l, D):
    S = z.shape[0]
    tm = _tm(S)
    ni = S // tm
    cc0 = 3 * ATTN_W // CHUNK
    ncc = D // CHUNK
    hb = tm // CONV_HALO
    last_hb = S // CONV_HALO - 1
    lead = CONV_HALO - (CONV_K - 1)
    group = 8

    def body(d_ref, dn_ref, cv_ref, cg_ref, hv_ref, hg_ref, w_ref, dcv_ref, dcg_ref, sums_ref,
             extd, extu, shd, shu):
        i = pl.program_id(1)

        @pl.when(i == 0)
        def _():
            sums_ref[...] = jnp.zeros_like(sums_ref)
        extd[0:tm, :] = d_ref[...]
        extd[tm:, :] = jnp.where(i < ni - 1, dn_ref[...], 0.0)
        extu[0:CONV_HALO, :] = jnp.where(i > 0, _glu(hv_ref[...], hg_ref[...]), 0.0)
        extu[CONV_HALO:, :] = _glu(cv_ref[...], cg_ref[...])
        _fill_shifted(shd, extd, tm + CONV_HALO - SHIFTS)
        _fill_shifted(shu, extu, tm + CONV_HALO - SHIFTS)

        def strip(s, carry):
            base = pl.multiple_of(s * STRIP, STRIP)
            acc = jnp.zeros((STRIP, CHUNK), F32)
            for k in range(CONV_K):
                acc = acc + w_ref[0, k:k + 1, :] * _tap(shd, base, CONV_K - 1 - k)
            cv, cg = cv_ref[pl.ds(base, STRIP), :], cg_ref[pl.ds(base, STRIP), :]
            sg = _sigmoid(cg)
            dcv_ref[pl.ds(base, STRIP), :] = (acc * sg).astype(BF16)
            dcg_ref[pl.ds(base, STRIP), :] = (acc * cv * sg * (1.0 - sg)).astype(BF16)
            return carry

        lax.fori_loop(0, tm // STRIP, strip, 0)

        for k0 in range(0, CONV_K + 1, group):
            ks = list(range(k0, min(k0 + group, CONV_K + 1)))

            def rows(s, accs, ks=ks):
                base = pl.multiple_of(s * SHIFTS, SHIFTS)
                dv = extd[pl.ds(base, SHIFTS), :]
                return tuple(a + (dv if k == CONV_K else dv * _tap(shu, base, lead + k, SHIFTS))
                             for a, k in zip(accs, ks))

            accs = lax.fori_loop(0, tm // SHIFTS, rows,
                                 tuple(jnp.zeros((SHIFTS, CHUNK), F32) for _ in ks))
            for k, a in zip(ks, accs):
                sums_ref[k:k + 1, :] += jnp.sum(a, axis=0, keepdims=True)

    cur = lambda off: pl.BlockSpec((tm, CHUNK), lambda cc, i: (i, cc0 + off + cc))
    halo = lambda off: pl.BlockSpec((CONV_HALO, CHUNK), lambda cc, i: (jnp.maximum(i * hb - 1, 0), cc0 + off + cc))
    osp = pl.BlockSpec((tm, CHUNK), lambda cc, i: (i, cc))
    big = pltpu.VMEM((tm + CONV_HALO, CHUNK), F32)
    shifted = pltpu.VMEM((SHIFTS, tm + CONV_HALO, CHUNK), F32)
    return pl.pallas_call(
        body, name="glu_conv_bwd", grid=(ncc, ni),
        in_specs=[osp,
                  pl.BlockSpec((CONV_HALO, CHUNK), lambda cc, i: (jnp.minimum((i + 1) * hb, last_hb), cc)),
                  cur(0), cur(ncc), halo(0), halo(ncc),
                  pl.BlockSpec((1, CONV_K, CHUNK), lambda cc, i: (l, 0, cc))],
        out_specs=[osp, osp, pl.BlockSpec((CONV_HALO, CHUNK), lambda cc, i: (0, cc))],
        out_shape=[SDS((S, D), BF16), SDS((S, D), BF16), SDS((CONV_HALO, D), F32)],
        scratch_shapes=[big, big, shifted, shifted],
        compiler_params=_cp("parallel", "arbitrary"),
    )(du1, du1, z, z, z, z, w_dw)


def _attn_bwd_prep(dattn, outs, lses):
    S = dattn.shape[0]
    tm = _tm(S)
    ng = len(DILATIONS)

    def body(*refs):
        da_ref, o_refs, l_refs = refs[0], refs[1:1 + ng], refs[1 + ng:1 + 2 * ng]
        d_refs, t_refs = refs[1 + 2 * ng:1 + 3 * ng], refs[1 + 3 * ng:1 + 4 * ng]
        scr = refs[1 + 4 * ng:]
        for h in range(HEADS_PER_GROUP):
            cols = slice(h * HEAD_DIM, (h + 1) * HEAD_DIM)
            ov = [_from_dilated(scr[g], o_refs[g], DILATIONS[g], cols) for g in range(ng)]
            lv = [_from_dilated(scr[ng + g], l_refs[g], DILATIONS[g], cols) for g in range(ng)]
            w = _group_weights(lv)
            dav = da_ref[:, cols]
            rs = jnp.sum(dav * (w[0] * ov[0] + w[1] * ov[1] + w[2] * ov[2]), axis=-1, keepdims=True)
            rs = jnp.broadcast_to(rs, (tm, HEAD_DIM))
            for g in range(ng):
                _to_dilated(scr[g], w[g] * dav, d_refs[g], DILATIONS[g], cols)
                _to_dilated(scr[ng + g], -w[g] * rs, t_refs[g], DILATIONS[g], cols)

    dil = [_dil_spec(d, tm) for d in DILATIONS]
    out = pl.pallas_call(
        body, name="attn_bwd_prep", grid=(S // tm,),
        in_specs=[pl.BlockSpec((tm, GROUP_W), lambda i: (i, 0))] + dil + dil, out_specs=dil + dil,
        out_shape=[SDS((d, S // d, GROUP_W), BF16) for d in DILATIONS]
        + [SDS((d, S // d, GROUP_W), F32) for d in DILATIONS],
        scratch_shapes=[pltpu.VMEM((tm, HEAD_DIM), F32)] * (2 * ng),
        compiler_params=_cp("parallel"),
    )(dattn, *outs, *lses)
    return out[:ng], out[ng:]


def _attn_bwd(q, k, v, do, lse, dterm, gi):
    d, Sd, _ = q.shape
    nb = Sd // BLOCK
    scale = HEAD_DIM ** -0.5

    def body(q_ref, qx_ref, kc_ref, kp_ref, vc_ref, vp_ref, do_ref, dox_ref, l_ref, lx_ref, t_ref, tx_ref,
             dq_ref, dk_ref, dv_ref, kk, vv, qq, dd, ds2, p2):
        n = pl.program_id(1)
        kk[0:BLOCK, :], kk[BLOCK:, :] = kp_ref[...], kc_ref[...]
        vv[0:BLOCK, :], vv[BLOCK:, :] = vp_ref[...], vc_ref[...]
        qq[0:BLOCK, :], qq[BLOCK:, :] = q_ref[...], qx_ref[...]
        dd[0:BLOCK, :], dd[BLOCK:, :] = do_ref[...], dox_ref[...]
        mask_ab = _band_mask(n > 0)
        qi = lax.broadcasted_iota(jnp.int32, (BLOCK, BLOCK), 0)
        kj = lax.broadcasted_iota(jnp.int32, (BLOCK, BLOCK), 1)
        mask_c = jnp.logical_and(kj >= qi, n < nb - 1)

        def pair(qh, kh, vh, dov, lv, tv, mask):
            s = lax.dot_general(qh, kh, NT_DIMS, preferred_element_type=F32) * scale
            p = jnp.exp(jnp.where(mask, s - lv, NEG))
            dp = lax.dot_general(dov, vh, NT_DIMS, preferred_element_type=F32)
            return p, p * (dp + tv) * scale

        for h in range(HEADS_PER_GROUP):
            sl = slice(h * HEAD_DIM, (h + 1) * HEAD_DIM)
            lv = jnp.concatenate([l_ref[:, sl], l_ref[:, sl]], axis=1)
            tv = jnp.concatenate([t_ref[:, sl], t_ref[:, sl]], axis=1)
            p_ab, ds_ab = pair(q_ref[:, sl], kk[:, sl], vv[:, sl], do_ref[:, sl], lv, tv, mask_ab)
            p_c, ds_c = pair(qx_ref[:, sl], kc_ref[:, sl], vc_ref[:, sl], dox_ref[:, sl],
                             lx_ref[:, sl], tx_ref[:, sl], mask_c)
            dq_ref[:, sl] = jnp.dot(ds_ab.astype(BF16), kk[:, sl], preferred_element_type=F32)
            ds2[0:BLOCK, :], ds2[BLOCK:, :] = ds_ab[:, BLOCK:].astype(BF16), ds_c.astype(BF16)
            p2[0:BLOCK, :], p2[BLOCK:, :] = p_ab[:, BLOCK:].astype(BF16), p_c.astype(BF16)
            dk_ref[:, sl] = lax.dot_general(ds2[...], qq[:, sl], TN_DIMS, preferred_element_type=F32)
            dv_ref[:, sl] = lax.dot_general(p2[...], dd[:, sl], TN_DIMS, preferred_element_type=F32).astype(BF16)

    cur, prev, nxt = _attn_blk(nb, 0), _attn_blk(nb, -1), _attn_blk(nb, 1)
    wide = pltpu.VMEM((2 * BLOCK, GROUP_W), BF16)
    tall = pltpu.VMEM((2 * BLOCK, HEAD_DIM), BF16)
    return pl.pallas_call(
        body, name=f"attn_bwd_g{gi}", grid=(d, nb),
        in_specs=[cur, nxt, cur, prev, cur, prev, cur, nxt, cur, nxt, cur, nxt],
        out_specs=[cur] * 3,
        out_shape=[SDS((d, Sd, GROUP_W), F32)] * 2 + [SDS((d, Sd, GROUP_W), BF16)],
        scratch_shapes=[wide, wide, wide, wide, tall, tall],
        compiler_params=_cp("parallel", "parallel"),
    )(q, q, k, k, v, v, do, do, lse, lse, dterm, dterm)


def _qk_prep_bwd(dqs, dks, dvs, z, g_q, g_k, tabs):
    S = z.shape[0]
    tm = _tm(S)
    n_heads = ATTN_W // HEAD_DIM
    ng = len(DILATIONS)

    def body(*refs):
        dq_refs, dk_refs, dv_refs = refs[:ng], refs[ng:2 * ng], refs[2 * ng:3 * ng]
        q_ref, k_ref, gq_ref, gk_ref, c_ref, sa_ref, sb_ref = refs[3 * ng:3 * ng + 7]
        dqo_ref, dko_ref, dvo_ref, sums_ref, scr = refs[3 * ng + 7:]

        @pl.when(pl.program_id(0) == 0)
        def _():
            sums_ref[...] = jnp.zeros_like(sums_ref)
        cos_t, sa_t, sb_t = c_ref[...], sa_ref[...], sb_ref[...]
        for row, (drefs, src, gref, dst) in enumerate(((dq_refs, q_ref, gq_ref, dqo_ref),
                                                       (dk_refs, k_ref, gk_ref, dko_ref))):
            gv = gref[...]
            gsum = jnp.zeros((1, HEAD_DIM), F32)
            for h in range(n_heads):
                gi = h // HEADS_PER_GROUP
                sl = slice(h * HEAD_DIM, (h + 1) * HEAD_DIM)
                cols = slice((h % HEADS_PER_GROUP) * HEAD_DIM, (h % HEADS_PER_GROUP + 1) * HEAD_DIM)
                dyn = _unrope(_from_dilated(scr, drefs[gi], DILATIONS[gi], cols), cos_t, sa_t, sb_t)
                t = src[:, sl]
                r = lax.rsqrt(jnp.mean(t * t, axis=-1, keepdims=True) + EPS)
                xh = t * r
                gsum = gsum + jnp.sum(dyn * xh, axis=0, keepdims=True)
                gy = dyn * gv
                dst[:, sl] = (r * (gy - xh * jnp.mean(gy * xh, axis=-1, keepdims=True))).astype(BF16)
            sums_ref[row:row + 1, :] += gsum
        for h in range(n_heads):
            gi = h // HEADS_PER_GROUP
            cols = slice((h % HEADS_PER_GROUP) * HEAD_DIM, (h % HEADS_PER_GROUP + 1) * HEAD_DIM)
            dvo_ref[:, h * HEAD_DIM:(h + 1) * HEAD_DIM] = _from_dilated(
                scr, dv_refs[gi], DILATIONS[gi], cols).astype(BF16)

    dil = [_dil_spec(d, tm) for d in DILATIONS]
    tab = pl.BlockSpec((tm, HEAD_DIM), lambda i: (i, 0))
    gsp = pl.BlockSpec((1, HEAD_DIM), lambda i: (0, 0))
    wide = pl.BlockSpec((tm, ATTN_W), lambda i: (i, 0))
    return pl.pallas_call(
        body, name="qk_prep_bwd", grid=(S // tm,),
        in_specs=dil * 3 + [wide, pl.BlockSpec((tm, ATTN_W), lambda i: (i, 1)), gsp, gsp, tab, tab, tab],
        out_specs=[wide, wide, wide, pl.BlockSpec((8, HEAD_DIM), lambda i: (0, 0))],
        out_shape=[SDS((S, ATTN_W), BF16)] * 3 + [SDS((8, HEAD_DIM), F32)],
        scratch_shapes=[pltpu.VMEM((tm, HEAD_DIM), F32)],
        compiler_params=_cp("arbitrary"),
    )(*dqs, *dks, *dvs, z, z, g_q, g_k, *tabs)


def _adamw(w, g, m, v, name):
    R, C = w.shape
    tr = _row_tile(R, C * 2)
    c1 = 1.0 - ADAM_B1 ** ADAM_STEP
    c2 = 1.0 - ADAM_B2 ** ADAM_STEP

    def body(w_ref, g_ref, m_ref, v_ref, d_ref, mo_ref, vo_ref):
        gv = g_ref[...]
        mn = ADAM_B1 * m_ref[...] + (1.0 - ADAM_B1) * gv
        vn = ADAM_B2 * v_ref[...] + (1.0 - ADAM_B2) * (gv * gv)
        mo_ref[...] = mn
        vo_ref[...] = vn
        d_ref[...] = -ADAM_LR * ((mn / c1) / (jnp.sqrt(vn / c2) + ADAM_EPS) + ADAM_WD * w_ref[...])

    sp = pl.BlockSpec((tr, C), lambda i: (i, 0))
    return pl.pallas_call(
        body, name=name, grid=(R // tr,), in_specs=[sp] * 4, out_specs=[sp] * 3,
        out_shape=[SDS((R, C), F32)] * 3, compiler_params=_cp("parallel"),
    )(w, g, m, v)


def _sum_parts(parts):
    n, R, C = parts.shape

    def body(p_ref, o_ref):
        acc = p_ref[0]
        for k in range(1, n):
            acc = acc + p_ref[k]
        o_ref[...] = acc

    return pl.pallas_call(
        body, name="sum_parts", out_shape=SDS((R, C), F32),
        compiler_params=pltpu.CompilerParams(vmem_limit_bytes=VMEM_LIMIT_BYTES),
    )(parts)


def _wada_grad(cact_t, dmod):
    D, n = cact_t.shape
    L, _, cols = dmod.shape

    def body(c_ref, d_ref, o_ref):
        acc = c_ref[:, 0:1] * d_ref[0, 0:1, :]
        for k in range(1, n):
            acc = acc + c_ref[:, k:k + 1] * d_ref[0, k:k + 1, :]
        o_ref[0] = acc

    return pl.pallas_call(
        body, name="wada_grad", grid=(L,),
        in_specs=[pl.BlockSpec((D, n), lambda l: (0, 0)), pl.BlockSpec((1, n, cols), lambda l: (l, 0, 0))],
        out_specs=pl.BlockSpec((1, D, cols), lambda l: (l, 0, 0)),
        out_shape=SDS((L, D, cols), F32), compiler_params=_cp("parallel"),
    )(cact_t, dmod)


def _add_halves(dw, land, c_idx, name):
    L, _, _, r2, cols = dw.shape
    tr = _row_tile(r2, cols, step=16)
    dw4 = dw.reshape(L * 4, 2, r2, cols)
    land3 = land.reshape(L * 4, r2, cols)

    def body(pf, a_ref, b_ref, o_ref):
        o_ref[...] = (a_ref[...] + b_ref[...]).astype(BF16)

    out = pl.pallas_call(
        body, name=name,
        grid_spec=pltpu.PrefetchScalarGridSpec(
            num_scalar_prefetch=1, grid=(L * 4, r2 // tr),
            in_specs=[pl.BlockSpec((None, None, tr, cols), lambda a, i, pf: (a, pf[0], i, 0)),
                      pl.BlockSpec((None, tr, cols), lambda a, i, pf: (a, i, 0))],
            out_specs=pl.BlockSpec((None, tr, cols), lambda a, i, pf: (a, i, 0))),
        out_shape=SDS((L * 4, r2, cols), BF16), compiler_params=_cp("parallel", "parallel"),
    )(c_idx, dw4, land3)
    return out.reshape(L, 4, r2, cols)


def _add_chips(dw, land1, land2, cj_idx, name):
    L, _, _, r2, cols = dw.shape
    tr = _row_tile(r2, cols, step=16)

    def body(pf, a_ref, b_ref, c0, c1, c2, o_ref):
        own = a_ref[...] + b_ref[...]
        o_ref[...] = ((own + c0[...].astype(F32)) + c1[...].astype(F32)) + c2[...].astype(F32)

    lsp = lambda p: pl.BlockSpec((None, None, tr, cols), lambda l, i, pf: (p, l, i, 0))
    return pl.pallas_call(
        body, name=name,
        grid_spec=pltpu.PrefetchScalarGridSpec(
            num_scalar_prefetch=1, grid=(L, r2 // tr),
            in_specs=[pl.BlockSpec((None, None, None, tr, cols), lambda l, i, pf: (l, pf[1], pf[0], i, 0)),
                      pl.BlockSpec((None, None, tr, cols), lambda l, i, pf: (l, pf[1], i, 0)),
                      lsp(0), lsp(1), lsp(2)],
            out_specs=pl.BlockSpec((None, None, tr, cols), lambda l, i, pf: (l, pf[0], i, 0))),
        out_shape=SDS((L, 2, r2, cols), F32), compiler_params=_cp("parallel", "parallel"),
    )(cj_idx, dw, land1, land2, land2, land2)


HBM_SPEC = pl.BlockSpec(memory_space=pltpu.HBM)


def _place():
    x, y, c = lax.axis_index("x"), lax.axis_index("y"), lax.axis_index("c")
    chips = [(x, 1 - y), (1 - x, y), (1 - x, 1 - y)]
    return x, y, c, chips


def _remote(src, dst, send_sem, recv_sem, device):
    return pltpu.make_async_remote_copy(src_ref=src, dst_ref=dst, send_sem=send_sem, recv_sem=recv_sem,
                                        device_id=device, device_id_type=MESH)


def _comm_call(body, name, ins, out_shapes, n_sems):
    return pl.pallas_call(
        body, name=name, in_specs=[HBM_SPEC] * len(ins), out_specs=[HBM_SPEC] * len(out_shapes),
        out_shape=out_shapes,
        scratch_shapes=[pltpu.SemaphoreType.DMA((n,)) for n in n_sems],
    )(*ins)


def _all_gather_weights(shards, small):
    nw = len(shards)
    L = shards[0].shape[0]
    Lh = L // 2

    def body(*refs):
        sh, sm = refs[:nw], refs[nw]
        full, smo = refs[nw + 1:2 * nw + 1], refs[2 * nw + 1]
        ici_s, ici_r, d2d_s, d2d_r, loc = refs[2 * nw + 2:]
        x, y, c, chips = _place()
        j = 2 * x + y
        jps = [2 * px + py for px, py in chips]
        me, sib = (x, y, c), (x, y, 1 - c)
        mine, theirs = pl.ds(c * Lh, Lh), pl.ds((1 - c) * Lh, Lh)

        local = [pltpu.make_async_copy(sh[w], full[w].at[j], loc.at[w]) for w in range(nw)]
        local.append(pltpu.make_async_copy(sm, smo.at[j], loc.at[nw]))
        for cp in local:
            cp.start()
        sends = []
        for w in range(nw):
            for p in range(3):
                sends.append(_remote(sh[w].at[mine], full[w].at[j, mine], ici_s.at[3 * w + p], ici_r.at[3 * w + p],
                                     (*chips[p], c)))
        for p in range(3):
            sends.append(_remote(sm, smo.at[j], ici_s.at[3 * nw + p], ici_r.at[3 * nw + p], (*chips[p], c)))
        for cp in sends:
            cp.start()
        passed = []
        for w in range(nw):
            for p in range(3):
                slot = full[w].at[jps[p], mine]
                _remote(slot, slot, ici_s.at[3 * w + p], ici_r.at[3 * w + p], me).wait_recv()
                cp = _remote(slot, slot, d2d_s.at[3 * w + p], d2d_r.at[3 * w + p], sib)
                cp.start()
                passed.append(cp)
        for p in range(3):
            slot = smo.at[jps[p]]
            _remote(slot, slot, ici_s.at[3 * nw + p], ici_r.at[3 * nw + p], me).wait_recv()
        for w in range(nw):
            for p in range(3):
                slot = full[w].at[jps[p], theirs]
                _remote(slot, slot, d2d_s.at[3 * w + p], d2d_r.at[3 * w + p], me).wait_recv()
        for cp in sends + passed:
            cp.wait_send()
        for cp in local:
            cp.wait()

    outs = [SDS((4,) + s.shape, s.dtype) for s in shards] + [SDS((4,) + small.shape, small.dtype)]
    res = _comm_call(body, "all_gather_weights", list(shards) + [small], outs,
                     [3 * nw + 3, 3 * nw + 3, 3 * nw, 3 * nw, nw + 1])
    return res[:nw], res[nw]


def _all_gather_small(part):
    def body(p_ref, o_ref, send_s, recv_s, loc):
        x, y, c, _ = _place()
        me_id = 4 * x + 2 * y + c
        own = pltpu.make_async_copy(p_ref, o_ref.at[me_id], loc.at[0])
        own.start()
        sends = []
        for k in range(1, 8):
            peer = (x ^ (k >> 2), y ^ ((k >> 1) & 1), c ^ (k & 1))
            sends.append(_remote(p_ref, o_ref.at[me_id], send_s.at[k - 1], recv_s.at[k - 1], peer))
        for cp in sends:
            cp.start()
        for k in range(1, 8):
            peer_id = 4 * (x ^ (k >> 2)) + 2 * (y ^ ((k >> 1) & 1)) + (c ^ (k & 1))
            slot = o_ref.at[peer_id]
            _remote(slot, slot, send_s.at[k - 1], recv_s.at[k - 1], (x, y, c)).wait_recv()
        for cp in sends:
            cp.wait_send()
        own.wait()

    return _comm_call(body, "all_gather_small", [part], [SDS((8,) + part.shape, part.dtype)], [7, 7, 1])[0]


def _rs_to_sibling(dws):
    nw = len(dws)

    def body(*refs):
        src, land = refs[:nw], refs[nw:2 * nw]
        send_s, recv_s = refs[2 * nw:]
        x, y, c, _ = _place()
        cps = [_remote(src[w].at[:, :, 1 - c], land[w], send_s.at[w], recv_s.at[w], (x, y, 1 - c))
               for w in range(nw)]
        for cp in cps:
            cp.start()
        for cp in cps:
            cp.wait_recv()
        for cp in cps:
            cp.wait_send()

    outs = [SDS((a.shape[0], 4, a.shape[3], a.shape[4]), a.dtype) for a in dws]
    return _comm_call(body, "rs_to_sibling", list(dws), outs, [nw, nw])


def _rs_to_chips(s1s):
    nw = len(s1s)

    def body(*refs):
        src, land = refs[:nw], refs[nw:2 * nw]
        send_s, recv_s = refs[2 * nw:]
        x, y, c, chips = _place()
        cps = []
        for w in range(nw):
            for p in range(3):
                jp = 2 * chips[p][0] + chips[p][1]
                cps.append(_remote(src[w].at[:, jp], land[w].at[p], send_s.at[3 * w + p], recv_s.at[3 * w + p],
                                   (*chips[p], c)))
        for cp in cps:
            cp.start()
        for cp in cps:
            cp.wait_recv()
        for cp in cps:
            cp.wait_send()

    outs = [SDS((3, a.shape[0], a.shape[2], a.shape[3]), a.dtype) for a in s1s]
    return _comm_call(body, "rs_to_chips", list(s1s), outs, [3 * nw, 3 * nw])


def _rs_join_halves(fulls):
    nw = len(fulls)

    def body(*refs):
        full = refs[nw:2 * nw]
        send_s, recv_s = refs[2 * nw:]
        x, y, c, _ = _place()
        cps = [_remote(full[w].at[:, c], full[w].at[:, c], send_s.at[w], recv_s.at[w], (x, y, 1 - c))
               for w in range(nw)]
        for cp in cps:
            cp.start()
        for w in range(nw):
            slot = full[w].at[:, 1 - c]
            _remote(slot, slot, send_s.at[w], recv_s.at[w], (x, y, c)).wait_recv()
        for cp in cps:
            cp.wait_send()

    return pl.pallas_call(
        body, name="rs_join_halves", in_specs=[HBM_SPEC] * nw, out_specs=[HBM_SPEC] * nw,
        out_shape=[SDS(a.shape, a.dtype) for a in fulls],
        scratch_shapes=[pltpu.SemaphoreType.DMA((nw,)), pltpu.SemaphoreType.DMA((nw,))],
        input_output_aliases={w: w for w in range(nw)},
    )(*fulls)


BIG = ("w_in", "w_attn_proj", "w_conv_out", "w_o", "w_ffn_in", "w_ffn_down")
COL_SHARDED = {"w_in": True, "w_attn_proj": True, "w_conv_out": False, "w_o": False,
               "w_ffn_in": True, "w_ffn_down": False}
SMALL = ("b_ada", "g_norm1", "g_q", "g_k", "w_conv_dw", "b_conv_dw", "g_conv_ln", "b_conv_ln",
         "g_norm2", "w_ffn_dw", "b_ffn_dw")
WEIGHTS = ("w_ada", "b_ada", "g_norm1", "w_in", "g_q", "g_k", "w_attn_proj", "w_conv_dw", "b_conv_dw",
           "g_conv_ln", "b_conv_ln", "w_conv_out", "w_o", "g_norm2", "w_ffn_in", "w_ffn_dw", "b_ffn_dw",
           "w_ffn_down")


def _pack_rows(arrs):
    flat = jnp.concatenate([a.reshape(-1) for a in arrs])
    pad = (-flat.shape[0]) % (8 * 128)
    return jnp.pad(flat, (0, pad)).reshape(-1, 128)


def _unpack_rows(packed, shapes):
    flat = packed.reshape(-1)
    out, off = [], 0
    for shp in shapes:
        n = 1
        for s in shp:
            n *= s
        out.append(flat[off:off + n].reshape(shp))
        off += n
    return out


def _rope_tables(positions):
    half = ROT_DIM // 2
    inv_freq = ROPE_THETA ** (-jnp.arange(0, ROT_DIM, 2, dtype=F32) / ROT_DIM)
    ang = positions.astype(F32)[..., None] * inv_freq
    cos, sin = jnp.cos(ang), jnp.sin(ang)
    S = cos.shape[0]
    cos_t = jnp.concatenate([cos, cos, jnp.ones((S, HEAD_DIM - ROT_DIM), F32)], axis=1)
    sa_t = jnp.concatenate([-sin, jnp.zeros((S, HEAD_DIM - half), F32)], axis=1)
    sb_t = jnp.concatenate([jnp.zeros((S, half), F32), sin, jnp.zeros((S, HEAD_DIM - ROT_DIM), F32)], axis=1)
    return cos_t, sa_t, sb_t


def _local_step(x, c, positions, loss_target, wf, small_w):
    S, D = x.shape
    L = wf["w_in"].shape[0]
    tabs = _rope_tables(positions)
    mod, cact = _mod_fwd(c, wf["w_ada"], small_w["b_ada"])
    vec = lambda a: a.reshape(1, -1)
    mods = [[vec(mod[l, k * D:(k + 1) * D]) for k in range(6)] for l in range(L)]

    saved = []
    xs = x
    for l in range(L):
        sh1, sc1, gt1, sh2, sc2, gt2 = mods[l]
        g1, g2 = vec(small_w["g_norm1"][l]), vec(small_w["g_norm2"][l])
        gq, gk = vec(small_w["g_q"][l]), vec(small_w["g_k"][l])
        z, h = _normmod_mm(xs, g1, sc1, sh1, wf["w_in"], l, "in_proj")
        qs, ks, vs = _qk_prep(z, gq, gk, tabs)
        outs, lses = [], []
        for gi in range(len(DILATIONS)):
            o_g, lse_g = _attn_fwd(qs[gi], ks[gi], vs[gi], gi)
            outs.append(o_g)
            lses.append(lse_g)
        attn = _attn_combine(outs, lses)
        y_a = _mm_nn(attn, wf["w_attn_proj"], l, "attn_proj")
        u1 = _glu_conv(z, small_w["w_conv_dw"], small_w["b_conv_dw"], l, D)
        u3 = _ln_silu(u1, vec(small_w["g_conv_ln"][l]), vec(small_w["b_conv_ln"][l]))
        y_b = _mm_nn(u3, wf["w_conv_out"], l, "conv_out")
        merged = _merge(z, y_a, y_b, D)
        t_o, x2 = _mm_nn(merged, wf["w_o"], l, "out_proj", res=xs, gate=gt1)
        gu, h2 = _normmod_mm(x2, g2, sc2, sh2, wf["w_ffn_in"], l, "ffn_in")
        a = _ffn_act(gu, small_w["w_ffn_dw"], small_w["b_ffn_dw"], l)
        t_f, x3 = _mm_nn(a, wf["w_ffn_down"], l, "ffn_down", res=x2, gate=gt2)
        saved.append(dict(x=xs, z=z, h=h, qs=qs, ks=ks, vs=vs, outs=outs, lses=lses, attn=attn, y_a=y_a, u1=u1, u3=u3,
                          y_b=y_b, merged=merged, t_o=t_o, x2=x2, gu=gu, h2=h2, a=a, t_f=t_f))
        xs = x3

    dx, loss = _loss_grad(xs, loss_target)

    bufs = {n: None for n in BIG}
    small_g = {n: [None] * L for n in SMALL}
    dmods = [None] * L

    def wgrad(name, a_mat, b_mat, l):
        bufs[name] = _mm_tn(a_mat, b_mat, bufs[name], l, L, COL_SHARDED[name], "grad_" + name)

    for l in reversed(range(L)):
        sv = saved[l]
        sh1, sc1, gt1, sh2, sc2, gt2 = mods[l]
        g1, g2 = vec(small_w["g_norm1"][l]), vec(small_w["g_norm2"][l])
        gq, gk = vec(small_w["g_q"][l]), vec(small_w["g_k"][l])
        e2, dgt2 = _gate_bwd(dx, sv["t_f"], gt2, "gate_bwd_ffn")
        wgrad("w_ffn_down", sv["a"], e2, l)
        da = _mm_nt(e2, wf["w_ffn_down"], l, "ffn_down_bwd")
        dg, du, ffn_sums = _ffn_act_bwd(da, sv["gu"], small_w["w_ffn_dw"], small_w["b_ffn_dw"], l)
        dgu = (dg, du)
        wgrad("w_ffn_in", sv["h2"], dgu, l)
        dh2 = _mm_nt(dgu, wf["w_ffn_in"], l, "ffn_in_bwd")
        dx2, n2_sums = _norm_bwd(dh2, sv["x2"], dx, g2, sc2, "norm2_bwd")
        e1, dgt1 = _gate_bwd(dx2, sv["t_o"], gt1, "gate_bwd_mix")
        wgrad("w_o", sv["merged"], e1, l)
        dmerged = _mm_nt(e1, wf["w_o"], l, "out_proj_bwd")
        dya, dyb, dga, dgb = _merge_bwd(dmerged, sv["z"], sv["y_a"], sv["y_b"], D)
        wgrad("w_attn_proj", sv["attn"], dya, l)
        wgrad("w_conv_out", sv["u3"], dyb, l)
        dattn = _mm_nt(dya, wf["w_attn_proj"], l, "attn_proj_bwd")
        du3 = _mm_nt(dyb, wf["w_conv_out"], l, "conv_out_bwd")
        du1, ln_sums = _ln_silu_bwd(du3, sv["u1"], vec(small_w["g_conv_ln"][l]), vec(small_w["b_conv_ln"][l]))
        dcv, dcg, conv_sums = _glu_conv_bwd(du1, sv["z"], small_w["w_conv_dw"], l, D)
        dos, dterms = _attn_bwd_prep(dattn, sv["outs"], sv["lses"])
        dqs, dks, dvs = [], [], []
        for gi in range(len(DILATIONS)):
            dq_g, dk_g, dv_g = _attn_bwd(sv["qs"][gi], sv["ks"][gi], sv["vs"][gi], dos[gi], sv["lses"][gi],
                                         dterms[gi], gi)
            dqs.append(dq_g)
            dks.append(dk_g)
            dvs.append(dv_g)
        dq, dk, dv, qk_sums = _qk_prep_bwd(dqs, dks, dvs, sv["z"], gq, gk, tabs)
        dz = jnp.concatenate([dq, dk, dv, dcv, dcg, dga, dgb], axis=1)
        wgrad("w_in", sv["h"], dz, l)
        dh = _mm_nt(dz, wf["w_in"], l, "in_proj_bwd")
        dx, n1_sums = _norm_bwd(dh, sv["x"], dx2, g1, sc1, "norm1_bwd")

        dmods[l] = jnp.concatenate([n1_sums[0], n1_sums[1], dgt1[0], n2_sums[0], n2_sums[1], dgt2[0]])
        small_g["b_ada"][l] = dmods[l]
        small_g["g_norm1"][l] = n1_sums[2]
        small_g["g_q"][l] = qk_sums[0]
        small_g["g_k"][l] = qk_sums[1]
        small_g["w_conv_dw"][l] = conv_sums[:CONV_K]
        small_g["b_conv_dw"][l] = conv_sums[CONV_K]
        small_g["g_conv_ln"][l] = ln_sums[0]
        small_g["b_conv_ln"][l] = ln_sums[1]
        small_g["g_norm2"][l] = n2_sums[2]
        small_g["w_ffn_dw"][l] = ffn_sums[:FFN_K]
        small_g["b_ffn_dw"][l] = ffn_sums[FFN_K]

    small_g = {n: jnp.stack(v) for n, v in small_g.items()}
    return loss, dx, bufs, small_g, cact


def kernel(x, c, positions, w_ada, b_ada, g_norm1, w_in, g_q, g_k, w_attn_proj, w_conv_dw, b_conv_dw, g_conv_ln, b_conv_ln, w_conv_out, w_o, g_norm2, w_ffn_in, w_ffn_dw, b_ffn_dw, w_ffn_down, loss_target, m_w_ada, m_b_ada, m_g_norm1, m_w_in, m_g_q, m_g_k, m_w_attn_proj, m_w_conv_dw, m_b_conv_dw, m_g_conv_ln, m_b_conv_ln, m_w_conv_out, m_w_o, m_g_norm2, m_w_ffn_in, m_w_ffn_dw, m_b_ffn_dw, m_w_ffn_down, v_w_ada, v_b_ada, v_g_norm1, v_w_in, v_g_q, v_g_k, v_w_attn_proj, v_w_conv_dw, v_b_conv_dw, v_g_conv_ln, v_b_conv_ln, v_w_conv_out, v_w_o, v_g_norm2, v_w_ffn_in, v_w_ffn_dw, v_b_ffn_dw, v_w_ffn_down):
    w = dict(w_ada=w_ada, b_ada=b_ada, g_norm1=g_norm1, w_in=w_in, g_q=g_q, g_k=g_k, w_attn_proj=w_attn_proj,
             w_conv_dw=w_conv_dw, b_conv_dw=b_conv_dw, g_conv_ln=g_conv_ln, b_conv_ln=b_conv_ln,
             w_conv_out=w_conv_out, w_o=w_o, g_norm2=g_norm2, w_ffn_in=w_ffn_in, w_ffn_dw=w_ffn_dw,
             b_ffn_dw=b_ffn_dw, w_ffn_down=w_ffn_down)
    m = dict(w_ada=m_w_ada, b_ada=m_b_ada, g_norm1=m_g_norm1, w_in=m_w_in, g_q=m_g_q, g_k=m_g_k,
             w_attn_proj=m_w_attn_proj, w_conv_dw=m_w_conv_dw, b_conv_dw=m_b_conv_dw, g_conv_ln=m_g_conv_ln,
             b_conv_ln=m_b_conv_ln, w_conv_out=m_w_conv_out, w_o=m_w_o, g_norm2=m_g_norm2, w_ffn_in=m_w_ffn_in,
             w_ffn_dw=m_w_ffn_dw, b_ffn_dw=m_b_ffn_dw, w_ffn_down=m_w_ffn_down)
    v = dict(w_ada=v_w_ada, b_ada=v_b_ada, g_norm1=v_g_norm1, w_in=v_w_in, g_q=v_g_q, g_k=v_g_k,
             w_attn_proj=v_w_attn_proj, w_conv_dw=v_w_conv_dw, b_conv_dw=v_b_conv_dw, g_conv_ln=v_g_conv_ln,
             b_conv_ln=v_b_conv_ln, w_conv_out=v_w_conv_out, w_o=v_w_o, g_norm2=v_g_norm2, w_ffn_in=v_w_ffn_in,
             w_ffn_dw=v_w_ffn_dw, b_ffn_dw=v_b_ffn_dw, w_ffn_down=v_w_ffn_down)
    xi, yi, ci = lax.axis_index("x"), lax.axis_index("y"), lax.axis_index("c")
    j = 2 * xi + yi
    L = w_ada.shape[0]
    D = x.shape[-1]

    mats = ("w_ada",) + BIG
    col = dict(COL_SHARDED, w_ada=True)
    shards = [w[n].astype(BF16) for n in mats]
    conv_shapes = [w["w_conv_dw"].shape, w["w_ffn_dw"].shape]
    gathered, small_g4 = _all_gather_weights(shards, _pack_rows([w["w_conv_dw"], w["w_ffn_dw"]]))
    wf = {}
    for n, g4 in zip(mats, gathered):
        _, _, r, cols = g4.shape
        if col[n]:
            wf[n] = g4.transpose(1, 2, 0, 3).reshape(L, r, 4 * cols)
        else:
            wf[n] = g4.transpose(1, 0, 2, 3).reshape(L, 4 * r, cols)
    conv_parts = [_unpack_rows(small_g4[k], conv_shapes) for k in range(4)]
    small_w = {n: w[n] for n in SMALL}
    small_w["w_conv_dw"] = jnp.concatenate([p[0] for p in conv_parts], axis=2)
    small_w["w_ffn_dw"] = jnp.concatenate([p[1] for p in conv_parts], axis=2)

    loss, dx, bufs, small_g, cact = _local_step(x[0], c, positions[0], loss_target[0], wf, small_w)
    loss = lax.psum(loss, ("x", "y", "c"))

    dws = [bufs[n] for n in BIG]
    land1 = _rs_to_sibling(dws)
    c_idx = jnp.reshape(ci, (1,)).astype(jnp.int32)
    s1s = [_add_halves(a, b, c_idx, "rs_add_halves") for a, b in zip(dws, land1)]
    land2 = _rs_to_chips(s1s)
    cj_idx = jnp.stack([ci, j]).astype(jnp.int32)
    halves = [_add_chips(a, b, e, cj_idx, "rs_add_chips") for a, b, e in zip(dws, land1, land2)]
    joined = _rs_join_halves(halves)
    grads = {n: g.reshape(w[n].shape) for n, g in zip(BIG, joined)}

    part = _pack_rows([small_g[n] for n in SMALL] + [cact])
    parts = _all_gather_small(part)
    full_shapes = [small_g[n].shape for n in SMALL]
    summed = _unpack_rows(_sum_parts(parts), full_shapes)
    for n, g in zip(SMALL, summed):
        if n in ("w_conv_dw", "w_ffn_dw"):
            cols = w[n].shape[2]
            g = lax.dynamic_slice_in_dim(g, j * cols, cols, axis=2)
        grads[n] = g
    flat = parts.reshape(8, -1)
    off, dmod_all = 0, None
    for n, shp in zip(SMALL, full_shapes):
        size = 1
        for s in shp:
            size *= s
        if n == "b_ada":
            dmod_all = flat[:, off:off + size].reshape(8, L, 6 * D)
        off += size
    cact_all = flat[:, off:off + D]
    cols = w_ada.shape[2]
    dmod_cols = lax.dynamic_slice_in_dim(dmod_all, j * cols, cols, axis=2).transpose(1, 0, 2)
    grads["w_ada"] = _wada_grad(cact_all.T, dmod_cols)

    delta, new_m, new_v = {}, {}, {}
    for n in ("w_ada",) + BIG:
        shp = w[n].shape
        two_d = lambda a: a.reshape(shp[0] * shp[1], shp[2])
        dl, mn, vn = _adamw(two_d(w[n]), two_d(grads[n]), two_d(m[n]), two_d(v[n]), "adamw_" + n)
        delta[n], new_m[n], new_v[n] = dl.reshape(shp), mn.reshape(shp), vn.reshape(shp)
    small_shapes = [w[n].shape for n in SMALL]
    packs = [_pack_rows([d[n] for n in SMALL]) for d in (w, grads, m, v)]
    outs = _adamw(*packs, "adamw_small")
    for d, packed in zip((delta, new_m, new_v), outs):
        for n, a in zip(SMALL, _unpack_rows(packed, small_shapes)):
            d[n] = a

    return (loss, dx[None], *[grads[n] for n in WEIGHTS], *[delta[n] for n in WEIGHTS],
            *[new_m[n] for n in WEIGHTS], *[new_v[n] for n in WEIGHTS])
```

```python
import functools

import jax
import jax.numpy as jnp
from jax import lax
from jax.experimental import pallas as pl
from jax.experimental.pallas import tpu as pltpu

F32 = jnp.float32
BF16 = jnp.bfloat16
SDS = jax.ShapeDtypeStruct
MESH = pl.DeviceIdType.MESH

HEAD_DIM = 128
BLOCK = 128
HEADS_PER_GROUP = 4
GROUP_W = HEADS_PER_GROUP * HEAD_DIM
DILATIONS = (1, 4, 16)
ATTN_W = len(DILATIONS) * GROUP_W
ROT_DIM = HEAD_DIM // 4
ROPE_THETA = 500000.0
CONV_K = 31
CONV_HALO = 32
FFN_K = 3
FFN_HALO = 8
EPS = 1e-6
NEG = -1e30
CHUNK = 512

ADAM_LR = 0.001
ADAM_B1 = 0.9
ADAM_B2 = 0.999
ADAM_EPS = 1e-08
ADAM_WD = 0.01
ADAM_STEP = 10

VMEM_LIMIT_BYTES = 48 * 1024 * 1024
TILE_BYTES = 2 * 1024 * 1024

NT_DIMS = (((1,), (1,)), ((), ()))
TN_DIMS = (((0,), (0,)), ((), ()))


def _cp(*sem):
    return pltpu.CompilerParams(dimension_semantics=sem if sem else None,
                                vmem_limit_bytes=VMEM_LIMIT_BYTES)


def _tm(s, cap=512):
    return min(cap, s)


def _row_tile(rows, cols, itemsize=4, step=8):
    best = None
    for t in range(step, rows + 1, step):
        if rows % t == 0 and t * cols * itemsize <= TILE_BYTES:
            best = t
    return best if best is not None else rows


def _quarter(n):
    return n // 4 if n % (4 * HEAD_DIM) == 0 else CHUNK


def _sigmoid(v):
    return jax.nn.sigmoid(v)


def _mod_fwd(c, w_ada, b_ada):
    L, D, N6 = w_ada.shape
    tn = N6 // 4

    def body(c_ref, w_ref, b_ref, mod_ref, cact_ref):
        cv = c_ref[...]
        ca = cv * _sigmoid(cv)
        cact_ref[...] = ca
        a8 = jnp.broadcast_to(ca, (8, D)).astype(BF16)
        acc = jnp.dot(a8, w_ref[0], preferred_element_type=F32)
        mod_ref[0] = acc[0:1, :] + b_ref[0]

    mod, cact = pl.pallas_call(
        body, name="mod_fwd", grid=(L, 4),
        in_specs=[pl.BlockSpec((1, D), lambda l, j: (0, 0)),
                  pl.BlockSpec((1, D, tn), lambda l, j: (l, 0, j)),
                  pl.BlockSpec((1, 1, tn), lambda l, j: (l, 0, j))],
        out_specs=[pl.BlockSpec((1, 1, tn), lambda l, j: (l, 0, j)),
                   pl.BlockSpec((1, D), lambda l, j: (0, 0))],
        out_shape=[SDS((L, 1, N6), F32), SDS((1, D), F32)],
        compiler_params=_cp("arbitrary", "arbitrary"),
    )(c, w_ada, b_ada.reshape(L, 1, N6))
    return mod.reshape(L, N6), cact


def _normmod_mm(x, g, sc, sh, w, l, name):
    S, D = x.shape
    N = w.shape[2]
    tm, tn = _tm(S, 1024), _quarter(N)

    def body(x_ref, g_ref, sc_ref, sh_ref, w_ref, z_ref, h_ref, hs):
        @pl.when(pl.program_id(1) == 0)
        def _():
            xv = x_ref[...]
            r = lax.rsqrt(jnp.mean(xv * xv, axis=-1, keepdims=True) + EPS)
            hv = (xv * r) * g_ref[...] * (1.0 + sc_ref[...]) + sh_ref[...]
            hs[...] = hv.astype(BF16)
            h_ref[...] = hs[...]
        z_ref[...] = jnp.dot(hs[...], w_ref[0], preferred_element_type=F32)

    vec = pl.BlockSpec((1, D), lambda i, j: (0, 0))
    return pl.pallas_call(
        body, name=name, grid=(S // tm, N // tn),
        in_specs=[pl.BlockSpec((tm, D), lambda i, j: (i, 0)), vec, vec, vec,
                  pl.BlockSpec((1, D, tn), lambda i, j: (l, 0, j))],
        out_specs=[pl.BlockSpec((tm, tn), lambda i, j: (i, j)),
                   pl.BlockSpec((tm, D), lambda i, j: (i, 0))],
        out_shape=[SDS((S, N), F32), SDS((S, D), BF16)],
        scratch_shapes=[pltpu.VMEM((tm, D), BF16)],
        compiler_params=_cp("parallel", "arbitrary"),
    )(x, g, sc, sh, w)


def _rope(t, cos_t, sa_t, sb_t):
    return t * cos_t + pltpu.roll(t, HEAD_DIM - ROT_DIM // 2, 1) * sa_t + pltpu.roll(t, ROT_DIM // 2, 1) * sb_t


def _unrope(d, cos_t, sa_t, sb_t):
    return d * cos_t + pltpu.roll(d * sa_t, ROT_DIM // 2, 1) + pltpu.roll(d * sb_t, HEAD_DIM - ROT_DIM // 2, 1)


def _dil_spec(d, tm):
    return pl.BlockSpec((d, tm // d, GROUP_W), lambda i: (0, i, 0))


def _to_dilated(scr, val, dst_ref, d, cols):
    if d == 1:
        dst_ref[0, :, cols] = val.astype(dst_ref.dtype)
        return
    n = val.shape[0] // d
    scr[...] = val
    for r in range(d):
        dst_ref[r, :, cols] = scr[pl.ds(r, n, stride=d), :].astype(dst_ref.dtype)


def _from_dilated(scr, src_ref, d, cols):
    if d == 1:
        return src_ref[0, :, cols].astype(F32)
    n = src_ref.shape[1]
    for r in range(d):
        scr[pl.ds(r, n, stride=d), :] = src_ref[r, :, cols].astype(F32)
    return scr[...]


def _qk_prep(z, g_q, g_k, tabs):
    S = z.shape[0]
    tm = _tm(S)
    n_heads = ATTN_W // HEAD_DIM
    ng = len(DILATIONS)

    def body(q_ref, k_ref, v_ref, gq_ref, gk_ref, c_ref, sa_ref, sb_ref, *rest):
        outs, scr = rest[:3 * ng], rest[3 * ng]
        cos_t, sa_t, sb_t = c_ref[...], sa_ref[...], sb_ref[...]
        for wi, (src, gref) in enumerate(((q_ref, gq_ref), (k_ref, gk_ref), (v_ref, None))):
            for h in range(n_heads):
                gi = h // HEADS_PER_GROUP
                cols = slice((h % HEADS_PER_GROUP) * HEAD_DIM, (h % HEADS_PER_GROUP + 1) * HEAD_DIM)
                t = src[:, h * HEAD_DIM:(h + 1) * HEAD_DIM]
                if gref is not None:
                    r = lax.rsqrt(jnp.mean(t * t, axis=-1, keepdims=True) + EPS)
                    t = _rope((t * r) * gref[...], cos_t, sa_t, sb_t)
                _to_dilated(scr, t, outs[wi * ng + gi], DILATIONS[gi], cols)

    tab = pl.BlockSpec((tm, HEAD_DIM), lambda i: (i, 0))
    gsp = pl.BlockSpec((1, HEAD_DIM), lambda i: (0, 0))
    wide = lambda cb: pl.BlockSpec((tm, ATTN_W), lambda i: (i, cb))
    out = pl.pallas_call(
        body, name="qk_prep", grid=(S // tm,),
        in_specs=[wide(0), wide(1), wide(2), gsp, gsp, tab, tab, tab],
        out_specs=[_dil_spec(d, tm) for _ in range(3) for d in DILATIONS],
        out_shape=[SDS((d, S // d, GROUP_W), BF16) for _ in range(3) for d in DILATIONS],
        scratch_shapes=[pltpu.VMEM((tm, HEAD_DIM), F32)],
        compiler_params=_cp("parallel"),
    )(z, z, z, g_q, g_k, *tabs)
    return out[:ng], out[ng:2 * ng], out[2 * ng:]


def _attn_blk(nb, shift):
    if shift < 0:
        return pl.BlockSpec((None, BLOCK, GROUP_W), lambda r, n: (r, jnp.maximum(n - 1, 0), 0))
    if shift > 0:
        return pl.BlockSpec((None, BLOCK, GROUP_W), lambda r, n: (r, jnp.minimum(n + 1, nb - 1), 0))
    return pl.BlockSpec((None, BLOCK, GROUP_W), lambda r, n: (r, n, 0))


def _band_mask(has_prev):
    qi = lax.broadcasted_iota(jnp.int32, (BLOCK, 2 * BLOCK), 0)
    kj = lax.broadcasted_iota(jnp.int32, (BLOCK, 2 * BLOCK), 1)
    band = jnp.logical_and(kj >= qi, kj <= qi + BLOCK)
    return jnp.logical_and(band, jnp.logical_or(has_prev, kj >= BLOCK))


def _attn_fwd(q, k, v, gi):
    d, Sd, _ = q.shape
    nb = Sd // BLOCK
    scale = HEAD_DIM ** -0.5

    def body(q_ref, kc_ref, kp_ref, vc_ref, vp_ref, o_ref, lse_ref, kk, vv):
        n = pl.program_id(1)
        kk[0:BLOCK, :], kk[BLOCK:, :] = kp_ref[...], kc_ref[...]
        vv[0:BLOCK, :], vv[BLOCK:, :] = vp_ref[...], vc_ref[...]
        mask = _band_mask(n > 0)
        for h in range(HEADS_PER_GROUP):
            sl = slice(h * HEAD_DIM, (h + 1) * HEAD_DIM)
            s = lax.dot_general(q_ref[:, sl], kk[:, sl], NT_DIMS, preferred_element_type=F32) * scale
            s = jnp.where(mask, s, NEG)
            m = jnp.max(s, axis=-1, keepdims=True)
            p = jnp.exp(s - m)
            den = jnp.sum(p, axis=-1, keepdims=True)
            acc = jnp.dot(p.astype(BF16), vv[:, sl], preferred_element_type=F32)
            o_ref[:, sl] = acc / den
            lse_ref[:, sl] = jnp.broadcast_to(m + jnp.log(den), (BLOCK, HEAD_DIM))

    cur, prev = _attn_blk(nb, 0), _attn_blk(nb, -1)
    return pl.pallas_call(
        body, name=f"attn_fwd_g{gi}", grid=(d, nb),
        in_specs=[cur, cur, prev, cur, prev], out_specs=[cur] * 2,
        out_shape=[SDS((d, Sd, GROUP_W), F32)] * 2,
        scratch_shapes=[pltpu.VMEM((2 * BLOCK, GROUP_W), BF16)] * 2,
        compiler_params=_cp("parallel", "parallel"),
    )(q, k, k, v, v)


def _group_weights(lses):
    m = jnp.maximum(jnp.maximum(lses[0], lses[1]), lses[2])
    es = [jnp.exp(v - m) for v in lses]
    inv = 1.0 / (es[0] + es[1] + es[2])
    return [e * inv for e in es]


def _attn_combine(outs, lses):
    S = outs[0].shape[0] * outs[0].shape[1]
    tm = _tm(S)
    ng = len(DILATIONS)

    def body(*refs):
        o_refs, l_refs, a_ref, scr = refs[:ng], refs[ng:2 * ng], refs[2 * ng], refs[2 * ng + 1:]
        for h in range(HEADS_PER_GROUP):
            cols = slice(h * HEAD_DIM, (h + 1) * HEAD_DIM)
            ov = [_from_dilated(scr[g], o_refs[g], DILATIONS[g], cols) for g in range(ng)]
            lv = [_from_dilated(scr[ng + g], l_refs[g], DILATIONS[g], cols) for g in range(ng)]
            w = _group_weights(lv)
            a_ref[:, cols] = (w[0] * ov[0] + w[1] * ov[1] + w[2] * ov[2]).astype(BF16)

    dil = [_dil_spec(d, tm) for d in DILATIONS]
    return pl.pallas_call(
        body, name="attn_combine", grid=(S // tm,), in_specs=dil + dil,
        out_specs=pl.BlockSpec((tm, GROUP_W), lambda i: (i, 0)),
        out_shape=SDS((S, GROUP_W), BF16),
        scratch_shapes=[pltpu.VMEM((tm, HEAD_DIM), F32)] * (2 * ng),
        compiler_params=_cp("parallel"),
    )(*outs, *lses)


def _mm_nn(a, w, l, name, res=None, gate=None):
    S, K = a.shape
    N = w.shape[2]
    tm = _tm(S)
    gated = res is not None

    def body(*refs):
        if gated:
            a_ref, w_ref, r_ref, g_ref, t_ref, o_ref = refs
        else:
            a_ref, w_ref, t_ref = refs
        t = jnp.dot(a_ref[...], w_ref[0], preferred_element_type=F32)
        t_ref[...] = t
        if gated:
            o_ref[...] = r_ref[...] + g_ref[...] * t

    row = pl.BlockSpec((tm, N), lambda i: (i, 0))
    in_specs = [pl.BlockSpec((tm, K), lambda i: (i, 0)), pl.BlockSpec((1, K, N), lambda i: (l, 0, 0))]
    args = [a, w]
    if gated:
        in_specs += [row, pl.BlockSpec((1, N), lambda i: (0, 0))]
        args += [res, gate]
    n_out = 2 if gated else 1
    out = pl.pallas_call(
        body, name=name, grid=(S // tm,), in_specs=in_specs, out_specs=[row] * n_out,
        out_shape=[SDS((S, N), F32)] * n_out, compiler_params=_cp("parallel"),
    )(*args)
    return out if gated else out[0]


def _glu(cv, cg):
    return cv * _sigmoid(cg)


STRIP = 16
SHIFTS = 8


def _fill_shifted(sh, ext, rows):
    sh[0] = ext[...]
    for b in range(1, SHIFTS):
        sh[b, 0:rows, :] = ext[pl.ds(b, rows), :]


def _tap(sh, base, off, rows=STRIP):
    return sh[off % SHIFTS, pl.ds(base + SHIFTS * (off // SHIFTS), rows), :]


def _glu_conv(z, w_dw, b_dw, l, D):
    S = z.shape[0]
    tm = _tm(S)
    cc0 = 3 * ATTN_W // CHUNK
    ncc = D // CHUNK
    hb = tm // CONV_HALO
    lead = CONV_HALO - (CONV_K - 1)

    def body(cv_ref, cg_ref, hv_ref, hg_ref, w_ref, b_ref, o_ref, ext, sh):
        i = pl.program_id(1)
        ext[0:CONV_HALO, :] = jnp.where(i > 0, _glu(hv_ref[...], hg_ref[...]), 0.0)
        ext[CONV_HALO:, :] = _glu(cv_ref[...], cg_ref[...])
        _fill_shifted(sh, ext, tm + CONV_HALO - SHIFTS)

        def strip(s, carry):
            base = s * STRIP
            acc = jnp.broadcast_to(b_ref[...], (STRIP, CHUNK))
            for k in range(CONV_K):
                acc = acc + w_ref[0, k:k + 1, :] * _tap(sh, base, lead + k)
            o_ref[pl.ds(base, STRIP), :] = acc
            return carry

        _unrolled(tm // STRIP, strip, 0)

    cur = lambda off: pl.BlockSpec((tm, CHUNK), lambda cc, i: (i, cc0 + off + cc))
    halo = lambda off: pl.BlockSpec((CONV_HALO, CHUNK), lambda cc, i: (jnp.maximum(i * hb - 1, 0), cc0 + off + cc))
    return pl.pallas_call(
        body, name="glu_conv", grid=(ncc, S // tm),
        in_specs=[cur(0), cur(ncc), halo(0), halo(ncc),
                  pl.BlockSpec((1, CONV_K, CHUNK), lambda cc, i: (l, 0, cc)),
                  pl.BlockSpec((None, 1, CHUNK), lambda cc, i: (l, 0, cc))],
        out_specs=pl.BlockSpec((tm, CHUNK), lambda cc, i: (i, cc)),
        out_shape=SDS((S, D), F32),
        scratch_shapes=[pltpu.VMEM((tm + CONV_HALO, CHUNK), F32),
                        pltpu.VMEM((SHIFTS, tm + CONV_HALO, CHUNK), F32)],
        compiler_params=_cp("parallel", "parallel"),
    )(z, z, z, z, w_dw, b_dw[:, None, :])


def _ln_stats(u):
    mu = jnp.mean(u, axis=-1, keepdims=True)
    xc = u - mu
    rstd = lax.rsqrt(jnp.mean(xc * xc, axis=-1, keepdims=True) + EPS)
    return xc * rstd, rstd


def _ln_silu(u1, g, b):
    S, D = u1.shape
    tm = _tm(S)

    def body(u_ref, g_ref, b_ref, o_ref):
        xh, _ = _ln_stats(u_ref[...])
        yv = xh * g_ref[...] + b_ref[...]
        o_ref[...] = (yv * _sigmoid(yv)).astype(BF16)

    vec = pl.BlockSpec((1, D), lambda i: (0, 0))
    return pl.pallas_call(
        body, name="ln_silu", grid=(S // tm,),
        in_specs=[pl.BlockSpec((tm, D), lambda i: (i, 0)), vec, vec],
        out_specs=pl.BlockSpec((tm, D), lambda i: (i, 0)),
        out_shape=SDS((S, D), BF16), compiler_params=_cp("parallel"),
    )(u1, g, b)


def _merge(z, y_a, y_b, D):
    S = z.shape[0]
    tm = _tm(S)
    ncc = D // CHUNK
    ga0 = (3 * ATTN_W + 2 * D) // CHUNK

    def body(ga_ref, gb_ref, ya_ref, yb_ref, o_ref):
        o_ref[...] = (_sigmoid(ga_ref[...]) * ya_ref[...] + _sigmoid(gb_ref[...]) * yb_ref[...]).astype(BF16)

    ysp = pl.BlockSpec((tm, CHUNK), lambda i, cc: (i, cc))
    return pl.pallas_call(
        body, name="merge", grid=(S // tm, ncc),
        in_specs=[pl.BlockSpec((tm, CHUNK), lambda i, cc: (i, ga0 + cc)),
                  pl.BlockSpec((tm, CHUNK), lambda i, cc: (i, ga0 + ncc + cc)), ysp, ysp],
        out_specs=ysp, out_shape=SDS((S, D), BF16), compiler_params=_cp("parallel", "parallel"),
    )(z, z, y_a, y_b)


def _ffn_chunk(F):
    best = 128
    for t in range(128, min(F, 1536) + 1, 128):
        if F % t == 0:
            best = t
    return best


FFN_ROWS = 32


def _unrolled(n, step, carry):
    for s in range(n):
        carry = step(s, carry)
    return carry


def _ffn_shift(sh, ext, rows):
    for k in range(FFN_K - 1):
        sh[k, 0:rows, :] = ext[pl.ds(FFN_HALO - (FFN_K - 1) + k, rows), :]


def _ffn_taps(sh, ext, base, ls):
    rows = pl.ds(base, FFN_ROWS)
    return [sh[k, rows, ls] for k in range(FFN_K - 1)] + [ext[pl.ds(base + FFN_HALO, FFN_ROWS), ls]]


def _ffn_act(gu, w_dw, b_dw, l):
    S, F2 = gu.shape
    F = F2 // 2
    cw = _ffn_chunk(F)
    ncc = F // cw
    tm = _tm(S)
    hb = tm // FFN_HALO

    def body(g_ref, u_ref, h_ref, w_ref, b_ref, o_ref, ext, sh):
        i = pl.program_id(1)
        ext[0:FFN_HALO, :] = jnp.where(i > 0, h_ref[...], 0.0)
        ext[FFN_HALO:, :] = g_ref[...]
        _ffn_shift(sh, ext, tm)
        for cg in range(cw // HEAD_DIM):
            ls = slice(cg * HEAD_DIM, (cg + 1) * HEAD_DIM)
            wv = [w_ref[0, k:k + 1, ls] for k in range(FFN_K)]
            bv = b_ref[:, ls]

            def strip(s, carry, ls=ls, wv=wv, bv=bv):
                base = s * FFN_ROWS
                rows = pl.ds(base, FFN_ROWS)
                taps = _ffn_taps(sh, ext, base, ls)
                gp = bv + wv[0] * taps[0] + wv[1] * taps[1] + wv[2] * taps[2]
                o_ref[rows, ls] = (gp * _sigmoid(gp) * u_ref[rows, ls]).astype(BF16)
                return carry

            _unrolled(tm // FFN_ROWS, strip, 0)

    return pl.pallas_call(
        body, name="ffn_act", grid=(ncc, S // tm),
        in_specs=[pl.BlockSpec((tm, cw), lambda cc, i: (i, cc)),
                  pl.BlockSpec((tm, cw), lambda cc, i: (i, ncc + cc)),
                  pl.BlockSpec((FFN_HALO, cw), lambda cc, i: (jnp.maximum(i * hb - 1, 0), cc)),
                  pl.BlockSpec((1, FFN_K, cw), lambda cc, i: (l, 0, cc)),
                  pl.BlockSpec((None, 1, cw), lambda cc, i: (l, 0, cc))],
        out_specs=pl.BlockSpec((tm, cw), lambda cc, i: (i, cc)),
        out_shape=SDS((S, F), BF16),
        scratch_shapes=[pltpu.VMEM((tm + FFN_HALO, cw), F32), pltpu.VMEM((FFN_K - 1, tm, cw), F32)],
        compiler_params=_cp("parallel", "parallel"),
    )(gu, gu, gu, w_dw, b_dw[:, None, :])


def _silu_grad(gp, sg):
    return sg * (1.0 + gp * (1.0 - sg))


def _ffn_act_bwd(da, gu, w_dw, b_dw, l):
    S, F2 = gu.shape
    F = F2 // 2
    cw = _ffn_chunk(F)
    ncc = F // cw
    tm = _tm(S, 256)
    ni = S // tm
    hb = tm // FFN_HALO
    last_hb = S // FFN_HALO - 1

    def body(da_ref, dan_ref, g_ref, gh_ref, gn_ref, u_ref, un_ref, w_ref, b_ref, dg_ref, du_ref, sums_ref,
             ext, sh, dgp, dsh):
        i = pl.program_id(1)

        @pl.when(i == 0)
        def _():
            sums_ref[...] = jnp.zeros_like(sums_ref)
        ext[0:FFN_HALO, :] = jnp.where(i > 0, gh_ref[...], 0.0)
        ext[FFN_HALO:FFN_HALO + tm, :] = g_ref[...]
        ext[FFN_HALO + tm:, :] = gn_ref[...]
        _ffn_shift(sh, ext, tm + FFN_HALO)
        tail = pl.ds(tm, FFN_HALO)
        gp_t = b_ref[...]
        for k, tap in enumerate((sh[0, tail, :], sh[1, tail, :], ext[pl.ds(tm + FFN_HALO, FFN_HALO), :])):
            gp_t = gp_t + w_ref[0, k:k + 1, :] * tap
        d_t = dan_ref[...] * un_ref[...] * _silu_grad(gp_t, _sigmoid(gp_t))
        dgp[tail, :] = jnp.where(i < ni - 1, d_t, 0.0)
        for cg in range(cw // HEAD_DIM):
            ls = slice(cg * HEAD_DIM, (cg + 1) * HEAD_DIM)
            wv = [w_ref[0, k:k + 1, ls] for k in range(FFN_K)]
            bv = b_ref[:, ls]

            def strip(s, accs, ls=ls, wv=wv, bv=bv):
                base = s * FFN_ROWS
                rows = pl.ds(base, FFN_ROWS)
                taps = _ffn_taps(sh, ext, base, ls)
                gp = bv + wv[0] * taps[0] + wv[1] * taps[1] + wv[2] * taps[2]
                sg = _sigmoid(gp)
                dav = da_ref[rows, ls]
                du_ref[rows, ls] = (dav * gp * sg).astype(BF16)
                dv = dav * u_ref[rows, ls] * _silu_grad(gp, sg)
                dgp[rows, ls] = dv
                return tuple(a + dv * t for a, t in zip(accs[:FFN_K], taps)) + (accs[FFN_K] + dv,)

            accs = _unrolled(tm // FFN_ROWS, strip,
                             tuple(jnp.zeros((FFN_ROWS, HEAD_DIM), F32) for _ in range(FFN_K + 1)))
            for k, a in enumerate(accs):
                sums_ref[k:k + 1, ls] += jnp.sum(a, axis=0, keepdims=True)
        for b in range(1, FFN_K):
            dsh[b - 1] = dgp[pl.ds(b, tm), :]
        for cg in range(cw // HEAD_DIM):
            ls = slice(cg * HEAD_DIM, (cg + 1) * HEAD_DIM)
            wv = [w_ref[0, k:k + 1, ls] for k in range(FFN_K)]

            def strip2(s, carry, ls=ls, wv=wv):
                rows = pl.ds(s * FFN_ROWS, FFN_ROWS)
                dg_ref[rows, ls] = (wv[0] * dsh[1, rows, ls] + wv[1] * dsh[0, rows, ls]
                                    + wv[2] * dgp[rows, ls]).astype(BF16)
                return carry

            _unrolled(tm // FFN_ROWS, strip2, 0)

    cur = lambda off: pl.BlockSpec((tm, cw), lambda cc, i: (i, off + cc))
    nxt = lambda off: pl.BlockSpec((FFN_HALO, cw), lambda cc, i: (jnp.minimum((i + 1) * hb, last_hb), off + cc))
    return pl.pallas_call(
        body, name="ffn_act_bwd", grid=(ncc, ni),
        in_specs=[cur(0), nxt(0), cur(0),
                  pl.BlockSpec((FFN_HALO, cw), lambda cc, i: (jnp.maximum(i * hb - 1, 0), cc)), nxt(0),
                  cur(ncc), nxt(ncc),
                  pl.BlockSpec((1, FFN_K, cw), lambda cc, i: (l, 0, cc)),
                  pl.BlockSpec((None, 1, cw), lambda cc, i: (l, 0, cc))],
        out_specs=[cur(0), cur(0), pl.BlockSpec((8, cw), lambda cc, i: (0, cc))],
        out_shape=[SDS((S, F), BF16), SDS((S, F), BF16), SDS((8, F), F32)],
        scratch_shapes=[pltpu.VMEM((tm + 2 * FFN_HALO, cw), F32), pltpu.VMEM((FFN_K - 1, tm + FFN_HALO, cw), F32),
                        pltpu.VMEM((tm + FFN_HALO, cw), F32), pltpu.VMEM((FFN_K - 1, tm, cw), F32)],
        compiler_params=_cp("parallel", "arbitrary"),
    )(da, da, gu, gu, gu, gu, gu, w_dw, b_dw[:, None, :])


def _loss_grad(y, target):
    S, D = y.shape
    tm = _tm(S)

    def body(y_ref, t_ref, dy_ref, l_ref):
        @pl.when(pl.program_id(0) == 0)
        def _():
            l_ref[...] = jnp.zeros_like(l_ref)
        e = y_ref[...] - t_ref[...]
        dy_ref[...] = e * (1.0 / D)
        l_ref[...] += jnp.broadcast_to(0.5 * jnp.sum(jnp.mean(e * e, axis=-1, keepdims=True)), (1, HEAD_DIM))

    row = pl.BlockSpec((tm, D), lambda i: (i, 0))
    dy, lsum = pl.pallas_call(
        body, name="loss_grad", grid=(S // tm,), in_specs=[row, row],
        out_specs=[row, pl.BlockSpec((1, HEAD_DIM), lambda i: (0, 0))],
        out_shape=[SDS((S, D), F32), SDS((1, HEAD_DIM), F32)],
        compiler_params=_cp("arbitrary"),
    )(y, target)
    return dy, lsum[0, 0]


def _gate_bwd(dx, t, gate, name):
    S, D = dx.shape
    tm = _tm(S)

    def body(dx_ref, t_ref, g_ref, e_ref, dg_ref):
        @pl.when(pl.program_id(0) == 0)
        def _():
            dg_ref[...] = jnp.zeros_like(dg_ref)
        dxv = dx_ref[...]
        e_ref[...] = (dxv * g_ref[...]).astype(BF16)
        dg_ref[...] += jnp.sum(dxv * t_ref[...], axis=0, keepdims=True)

    row = pl.BlockSpec((tm, D), lambda i: (i, 0))
    vec = pl.BlockSpec((1, D), lambda i: (0, 0))
    return pl.pallas_call(
        body, name=name, grid=(S // tm,), in_specs=[row, row, vec], out_specs=[row, vec],
        out_shape=[SDS((S, D), BF16), SDS((1, D), F32)], compiler_params=_cp("arbitrary"),
    )(dx, t, gate)


def _mm_nt(dy, w, l, name):
    parts = dy if isinstance(dy, tuple) else (dy,)
    S, cp = parts[0].shape
    C = cp * len(parts)
    K = w.shape[1]
    tm = _tm(S, 1024 if K <= 1024 else 512)
    tc = _quarter(C) if C > 1024 else C
    nc = C // tc
    per = cp // tc

    def body(*refs):
        p_refs, (w_ref, o_ref, acc) = refs[:len(parts)], refs[len(parts):]
        kk = pl.program_id(1)

        @pl.when(kk == 0)
        def _():
            acc[...] = jnp.zeros_like(acc)
        for p, p_ref in enumerate(p_refs):
            @pl.when(jnp.logical_and(kk >= p * per, kk < (p + 1) * per))
            def _(p_ref=p_ref):
                acc[...] += lax.dot_general(p_ref[...], w_ref[0], NT_DIMS, preferred_element_type=F32)

        @pl.when(kk == nc - 1)
        def _():
            o_ref[...] = acc[...]

    part_spec = lambda p: pl.BlockSpec((tm, tc), lambda i, kk: (i, jnp.clip(kk - p * per, 0, per - 1)))
    return pl.pallas_call(
        body, name=name, grid=(S // tm, nc),
        in_specs=[part_spec(p) for p in range(len(parts))]
        + [pl.BlockSpec((1, K, tc), lambda i, kk: (l, 0, kk))],
        out_specs=pl.BlockSpec((tm, K), lambda i, kk: (i, 0)),
        out_shape=SDS((S, K), F32),
        scratch_shapes=[pltpu.VMEM((tm, K), F32)],
        compiler_params=_cp("parallel", "arbitrary"),
    )(*parts, w)


def _mm_tn(a, b, buf, l, n_layers, col_sharded, name):
    parts = b if isinstance(b, tuple) else (b,)
    S, K = a.shape
    N = parts[0].shape[1] * len(parts)
    ts = _tm(S, 1024)
    ns = S // ts
    if col_sharded:
        cols, r2 = N // 4, K // 2
        grid = (4, ns)
        spp = 4 // len(parts)

        def body(*refs):
            a_ref, b_refs, o_ref = refs[0], refs[1:1 + len(parts)], refs[-1]
            jj, s = pl.program_id(0), pl.program_id(1)

            @pl.when(s == 0)
            def _():
                o_ref[...] = jnp.zeros_like(o_ref)
            for p, b_ref in enumerate(b_refs):
                @pl.when(jj // spp == p)
                def _(b_ref=b_ref):
                    for hh in range(2):
                        o_ref[hh] += lax.dot_general(a_ref[:, hh * r2:(hh + 1) * r2], b_ref[...], TN_DIMS,
                                                     preferred_element_type=F32)

        part_spec = lambda p: pl.BlockSpec(
            (ts, cols), lambda jj, s: (jnp.where(jj // spp == p, s, 0), jnp.clip(jj - p * spp, 0, spp - 1)))
        in_specs = [pl.BlockSpec((ts, K), lambda jj, s: (s, 0))] + [part_spec(p) for p in range(len(parts))]
        out_spec = pl.BlockSpec((None, None, 2, r2, cols), lambda jj, s: (l, jj, 0, 0, 0))
        scratch = []
    else:
        cols, r2 = N, K // 8
        tn = CHUNK if N % CHUNK == 0 else N
        grid = (N // tn, ns)

        def body(*refs):
            a_ref, b_ref, o_ref, acc = refs[0], refs[1], refs[-2], refs[-1]
            s = pl.program_id(1)

            @pl.when(s == 0)
            def _():
                acc[...] = jnp.zeros_like(acc)
            acc[...] += lax.dot_general(a_ref[...], b_ref[...], TN_DIMS, preferred_element_type=F32)

            @pl.when(s == ns - 1)
            def _():
                for jj in range(4):
                    for hh in range(2):
                        o_ref[jj, hh] = acc[(2 * jj + hh) * r2:(2 * jj + hh + 1) * r2, :]

        in_specs = [pl.BlockSpec((ts, K), lambda nb, s: (s, 0)), pl.BlockSpec((ts, tn), lambda nb, s: (s, nb))]
        out_spec = pl.BlockSpec((None, 4, 2, r2, tn), lambda nb, s: (l, 0, 0, 0, nb))
        scratch = [pltpu.VMEM((K, tn), F32)]
    args = [a, *parts]
    aliases = {}
    if buf is not None:
        in_specs = in_specs + [pl.BlockSpec(memory_space=pl.ANY)]
        args.append(buf)
        aliases = {len(args) - 1: 0}
    return pl.pallas_call(
        body, name=name, grid=grid, in_specs=in_specs, out_specs=out_spec,
        out_shape=SDS((n_layers, 4, 2, r2, cols), F32), scratch_shapes=scratch,
        input_output_aliases=aliases, compiler_params=_cp("parallel", "arbitrary"),
    )(*args)


def _norm_bwd(dh, x, dres, g, sc, name):
    S, D = x.shape
    tm = _tm(S)

    def body(dh_ref, x_ref, dr_ref, g_ref, sc_ref, dx_ref, sums_ref):
        @pl.when(pl.program_id(0) == 0)
        def _():
            sums_ref[...] = jnp.zeros_like(sums_ref)
        xv, dhv = x_ref[...], dh_ref[...]
        r = lax.rsqrt(jnp.mean(xv * xv, axis=-1, keepdims=True) + EPS)
        xn = xv * r
        gv, sv = g_ref[...], 1.0 + sc_ref[...]
        dxn = dhv * gv * sv
        dx_ref[...] = dr_ref[...] + r * (dxn - xn * jnp.mean(dxn * xn, axis=-1, keepdims=True))
        dhxn = dhv * xn
        sums_ref[0:1, :] += jnp.sum(dhv, axis=0, keepdims=True)
        sums_ref[1:2, :] += jnp.sum(dhxn * gv, axis=0, keepdims=True)
        sums_ref[2:3, :] += jnp.sum(dhxn * sv, axis=0, keepdims=True)

    row = pl.BlockSpec((tm, D), lambda i: (i, 0))
    vec = pl.BlockSpec((1, D), lambda i: (0, 0))
    return pl.pallas_call(
        body, name=name, grid=(S // tm,), in_specs=[row, row, row, vec, vec],
        out_specs=[row, pl.BlockSpec((8, D), lambda i: (0, 0))],
        out_shape=[SDS((S, D), F32), SDS((8, D), F32)], compiler_params=_cp("arbitrary"),
    )(dh, x, dres, g, sc)


def _merge_bwd(dm, z, y_a, y_b, D):
    S = z.shape[0]
    tm = _tm(S)
    ncc = D // CHUNK
    ga0 = (3 * ATTN_W + 2 * D) // CHUNK

    def body(dm_ref, ga_ref, gb_ref, ya_ref, yb_ref, dya_ref, dyb_ref, dga_ref, dgb_ref):
        dmv = dm_ref[...]
        sa, sb = _sigmoid(ga_ref[...]), _sigmoid(gb_ref[...])
        dya_ref[...] = (dmv * sa).astype(BF16)
        dyb_ref[...] = (dmv * sb).astype(BF16)
        dga_ref[...] = (dmv * ya_ref[...] * sa * (1.0 - sa)).astype(BF16)
        dgb_ref[...] = (dmv * yb_ref[...] * sb * (1.0 - sb)).astype(BF16)

    ysp = pl.BlockSpec((tm, CHUNK), lambda i, cc: (i, cc))
    return pl.pallas_call(
        body, name="merge_bwd", grid=(S // tm, ncc),
        in_specs=[ysp, pl.BlockSpec((tm, CHUNK), lambda i, cc: (i, ga0 + cc)),
                  pl.BlockSpec((tm, CHUNK), lambda i, cc: (i, ga0 + ncc + cc)), ysp, ysp],
        out_specs=[ysp] * 4, out_shape=[SDS((S, D), BF16)] * 4,
        compiler_params=_cp("parallel", "parallel"),
    )(dm, z, z, y_a, y_b)


def _ln_silu_bwd(du3, u1, g, b):
    S, D = u1.shape
    tm = _tm(S)

    def body(d_ref, u_ref, g_ref, b_ref, du_ref, sums_ref):
        @pl.when(pl.program_id(0) == 0)
        def _():
            sums_ref[...] = jnp.zeros_like(sums_ref)
        xh, rstd = _ln_stats(u_ref[...])
        gv = g_ref[...]
        yv = xh * gv + b_ref[...]
        sg = _sigmoid(yv)
        dy = d_ref[...] * (sg * (1.0 + yv * (1.0 - sg)))
        dxh = dy * gv
        du_ref[...] = rstd * (dxh - jnp.mean(dxh, axis=-1, keepdims=True)
                              - xh * jnp.mean(dxh * xh, axis=-1, keepdims=True))
        sums_ref[0:1, :] += jnp.sum(dy * xh, axis=0, keepdims=True)
        sums_ref[1:2, :] += jnp.sum(dy, axis=0, keepdims=True)

    row = pl.BlockSpec((tm, D), lambda i: (i, 0))
    vec = pl.BlockSpec((1, D), lambda i: (0, 0))
    return pl.pallas_call(
        body, name="ln_silu_bwd", grid=(S // tm,), in_specs=[row, row, vec, vec],
        out_specs=[row, pl.BlockSpec((8, D), lambda i: (0, 0))],
        out_shape=[SDS((S, D), F32), SDS((8, D), F32)], compiler_params=_cp("arbitrary"),
    )(du3, u1, g, b)


def _glu_conv_bwd(du1, z, w_dw, l, D):
    S = z.shape[0]
    tm = _tm(S)
    ni = S // tm
    cc0 = 3 * ATTN_W // CHUNK
    ncc = D // CHUNK
    hb = tm // CONV_HALO
    last_hb = S // CONV_HALO - 1
    lead = CONV_HALO - (CONV_K - 1)
    group = 8

    def body(d_ref, dn_ref, cv_ref, cg_ref, hv_ref, hg_ref, w_ref, dcv_ref, dcg_ref, sums_ref,
             extd, extu, shd, shu):
        i = pl.program_id(1)

        @pl.when(i == 0)
        def _():
            sums_ref[...] = jnp.zeros_like(sums_ref)
        extd[0:tm, :] = d_ref[...]
        extd[tm:, :] = jnp.where(i < ni - 1, dn_ref[...], 0.0)
        extu[0:CONV_HALO, :] = jnp.where(i > 0, _glu(hv_ref[...], hg_ref[...]), 0.0)
        extu[CONV_HALO:, :] = _glu(cv_ref[...], cg_ref[...])
        _fill_shifted(shd, extd, tm + CONV_HALO - SHIFTS)
        _fill_shifted(shu, extu, tm + CONV_HALO - SHIFTS)

        def strip(s, carry):
            base = s * STRIP
            acc = jnp.zeros((STRIP, CHUNK), F32)
            for k in range(CONV_K):
                acc = acc + w_ref[0, k:k + 1, :] * _tap(shd, base, CONV_K - 1 - k)
            cv, cg = cv_ref[pl.ds(base, STRIP), :], cg_ref[pl.ds(base, STRIP), :]
            sg = _sigmoid(cg)
            dcv_ref[pl.ds(base, STRIP), :] = (acc * sg).astype(BF16)
            dcg_ref[pl.ds(base, STRIP), :] = (acc * cv * sg * (1.0 - sg)).astype(BF16)
            return carry

        _unrolled(tm // STRIP, strip, 0)

        for k0 in range(0, CONV_K + 1, group):
            ks = list(range(k0, min(k0 + group, CONV_K + 1)))

            def rows(s, accs, ks=ks):
                base = s * SHIFTS
                dv = extd[pl.ds(base, SHIFTS), :]
                return tuple(a + (dv if k == CONV_K else dv * _tap(shu, base, lead + k, SHIFTS))
                             for a, k in zip(accs, ks))

            accs = _unrolled(tm // SHIFTS, rows,
                             tuple(jnp.zeros((SHIFTS, CHUNK), F32) for _ in ks))
            for k, a in zip(ks, accs):
                sums_ref[k:k + 1, :] += jnp.sum(a, axis=0, keepdims=True)

    cur = lambda off: pl.BlockSpec((tm, CHUNK), lambda cc, i: (i, cc0 + off + cc))
    halo = lambda off: pl.BlockSpec((CONV_HALO, CHUNK), lambda cc, i: (jnp.maximum(i * hb - 1, 0), cc0 + off + cc))
    osp = pl.BlockSpec((tm, CHUNK), lambda cc, i: (i, cc))
    big = pltpu.VMEM((tm + CONV_HALO, CHUNK), F32)
    shifted = pltpu.VMEM((SHIFTS, tm + CONV_HALO, CHUNK), F32)
    return pl.pallas_call(
        body, name="glu_conv_bwd", grid=(ncc, ni),
        in_specs=[osp,
                  pl.BlockSpec((CONV_HALO, CHUNK), lambda cc, i: (jnp.minimum((i + 1) * hb, last_hb), cc)),
                  cur(0), cur(ncc), halo(0), halo(ncc),
                  pl.BlockSpec((1, CONV_K, CHUNK), lambda cc, i: (l, 0, cc))],
        out_specs=[osp, osp, pl.BlockSpec((CONV_HALO, CHUNK), lambda cc, i: (0, cc))],
        out_shape=[SDS((S, D), BF16), SDS((S, D), BF16), SDS((CONV_HALO, D), F32)],
        scratch_shapes=[big, big, shifted, shifted],
        compiler_params=_cp("parallel", "arbitrary"),
    )(du1, du1, z, z, z, z, w_dw)


def _attn_bwd_prep(dattn, outs, lses):
    S = dattn.shape[0]
    tm = _tm(S)
    ng = len(DILATIONS)

    def body(*refs):
        da_ref, o_refs, l_refs = refs[0], refs[1:1 + ng], refs[1 + ng:1 + 2 * ng]
        d_refs, t_refs = refs[1 + 2 * ng:1 + 3 * ng], refs[1 + 3 * ng:1 + 4 * ng]
        scr = refs[1 + 4 * ng:]
        for h in range(HEADS_PER_GROUP):
            cols = slice(h * HEAD_DIM, (h + 1) * HEAD_DIM)
            ov = [_from_dilated(scr[g], o_refs[g], DILATIONS[g], cols) for g in range(ng)]
            lv = [_from_dilated(scr[ng + g], l_refs[g], DILATIONS[g], cols) for g in range(ng)]
            w = _group_weights(lv)
            dav = da_ref[:, cols]
            rs = jnp.sum(dav * (w[0] * ov[0] + w[1] * ov[1] + w[2] * ov[2]), axis=-1, keepdims=True)
            rs = jnp.broadcast_to(rs, (tm, HEAD_DIM))
            for g in range(ng):
                _to_dilated(scr[g], w[g] * dav, d_refs[g], DILATIONS[g], cols)
                _to_dilated(scr[ng + g], -w[g] * rs, t_refs[g], DILATIONS[g], cols)

    dil = [_dil_spec(d, tm) for d in DILATIONS]
    out = pl.pallas_call(
        body, name="attn_bwd_prep", grid=(S // tm,),
        in_specs=[pl.BlockSpec((tm, GROUP_W), lambda i: (i, 0))] + dil + dil, out_specs=dil + dil,
        out_shape=[SDS((d, S // d, GROUP_W), BF16) for d in DILATIONS]
        + [SDS((d, S // d, GROUP_W), F32) for d in DILATIONS],
        scratch_shapes=[pltpu.VMEM((tm, HEAD_DIM), F32)] * (2 * ng),
        compiler_params=_cp("parallel"),
    )(dattn, *outs, *lses)
    return out[:ng], out[ng:]


def _attn_bwd(q, k, v, do, lse, dterm, gi):
    d, Sd, _ = q.shape
    nb = Sd // BLOCK
    scale = HEAD_DIM ** -0.5

    def body(q_ref, qx_ref, kc_ref, kp_ref, vc_ref, vp_ref, do_ref, dox_ref, l_ref, lx_ref, t_ref, tx_ref,
             dq_ref, dk_ref, dv_ref, kk, vv, qq, dd, ds2, p2):
        n = pl.program_id(1)
        kk[0:BLOCK, :], kk[BLOCK:, :] = kp_ref[...], kc_ref[...]
        vv[0:BLOCK, :], vv[BLOCK:, :] = vp_ref[...], vc_ref[...]
        qq[0:BLOCK, :], qq[BLOCK:, :] = q_ref[...], qx_ref[...]
        dd[0:BLOCK, :], dd[BLOCK:, :] = do_ref[...], dox_ref[...]
        mask_ab = _band_mask(n > 0)
        qi = lax.broadcasted_iota(jnp.int32, (BLOCK, BLOCK), 0)
        kj = lax.broadcasted_iota(jnp.int32, (BLOCK, BLOCK), 1)
        mask_c = jnp.logical_and(kj >= qi, n < nb - 1)

        def pair(qh, kh, vh, dov, lv, tv, mask):
            s = lax.dot_general(qh, kh, NT_DIMS, preferred_element_type=F32) * scale
            p = jnp.exp(jnp.where(mask, s - lv, NEG))
            dp = lax.dot_general(dov, vh, NT_DIMS, preferred_element_type=F32)
            return p, p * (dp + tv) * scale

        for h in range(HEADS_PER_GROUP):
            sl = slice(h * HEAD_DIM, (h + 1) * HEAD_DIM)
            lv = jnp.concatenate([l_ref[:, sl], l_ref[:, sl]], axis=1)
            tv = jnp.concatenate([t_ref[:, sl], t_ref[:, sl]], axis=1)
            p_ab, ds_ab = pair(q_ref[:, sl], kk[:, sl], vv[:, sl], do_ref[:, sl], lv, tv, mask_ab)
            p_c, ds_c = pair(qx_ref[:, sl], kc_ref[:, sl], vc_ref[:, sl], dox_ref[:, sl],
                             lx_ref[:, sl], tx_ref[:, sl], mask_c)
            dq_ref[:, sl] = jnp.dot(ds_ab.astype(BF16), kk[:, sl], preferred_element_type=F32)
            ds2[0:BLOCK, :], ds2[BLOCK:, :] = ds_ab[:, BLOCK:].astype(BF16), ds_c.astype(BF16)
            p2[0:BLOCK, :], p2[BLOCK:, :] = p_ab[:, BLOCK:].astype(BF16), p_c.astype(BF16)
            dk_ref[:, sl] = lax.dot_general(ds2[...], qq[:, sl], TN_DIMS, preferred_element_type=F32)
            dv_ref[:, sl] = lax.dot_general(p2[...], dd[:, sl], TN_DIMS, preferred_element_type=F32).astype(BF16)

    cur, prev, nxt = _attn_blk(nb, 0), _attn_blk(nb, -1), _attn_blk(nb, 1)
    wide = pltpu.VMEM((2 * BLOCK, GROUP_W), BF16)
    tall = pltpu.VMEM((2 * BLOCK, HEAD_DIM), BF16)
    return pl.pallas_call(
        body, name=f"attn_bwd_g{gi}", grid=(d, nb),
        in_specs=[cur, nxt, cur, prev, cur, prev, cur, nxt, cur, nxt, cur, nxt],
        out_specs=[cur] * 3,
        out_shape=[SDS((d, Sd, GROUP_W), F32)] * 2 + [SDS((d, Sd, GROUP_W), BF16)],
        scratch_shapes=[wide, wide, wide, wide, tall, tall],
        compiler_params=_cp("parallel", "parallel"),
    )(q, q, k, k, v, v, do, do, lse, lse, dterm, dterm)


def _qk_prep_bwd(dqs, dks, dvs, z, g_q, g_k, tabs):
    S = z.shape[0]
    tm = _tm(S)
    n_heads = ATTN_W // HEAD_DIM
    ng = len(DILATIONS)

    def body(*refs):
        dq_refs, dk_refs, dv_refs = refs[:ng], refs[ng:2 * ng], refs[2 * ng:3 * ng]
        q_ref, k_ref, gq_ref, gk_ref, c_ref, sa_ref, sb_ref = refs[3 * ng:3 * ng + 7]
        dqo_ref, dko_ref, dvo_ref, sums_ref, scr = refs[3 * ng + 7:]

        @pl.when(pl.program_id(0) == 0)
        def _():
            sums_ref[...] = jnp.zeros_like(sums_ref)
        cos_t, sa_t, sb_t = c_ref[...], sa_ref[...], sb_ref[...]
        for row, (drefs, src, gref, dst) in enumerate(((dq_refs, q_ref, gq_ref, dqo_ref),
                                                       (dk_refs, k_ref, gk_ref, dko_ref))):
            gv = gref[...]
            gsum = jnp.zeros((1, HEAD_DIM), F32)
            for h in range(n_heads):
                gi = h // HEADS_PER_GROUP
                sl = slice(h * HEAD_DIM, (h + 1) * HEAD_DIM)
                cols = slice((h % HEADS_PER_GROUP) * HEAD_DIM, (h % HEADS_PER_GROUP + 1) * HEAD_DIM)
                dyn = _unrope(_from_dilated(scr, drefs[gi], DILATIONS[gi], cols), cos_t, sa_t, sb_t)
                t = src[:, sl]
                r = lax.rsqrt(jnp.mean(t * t, axis=-1, keepdims=True) + EPS)
                xh = t * r
                gsum = gsum + jnp.sum(dyn * xh, axis=0, keepdims=True)
                gy = dyn * gv
                dst[:, sl] = (r * (gy - xh * jnp.mean(gy * xh, axis=-1, keepdims=True))).astype(BF16)
            sums_ref[row:row + 1, :] += gsum
        for h in range(n_heads):
            gi = h // HEADS_PER_GROUP
            cols = slice((h % HEADS_PER_GROUP) * HEAD_DIM, (h % HEADS_PER_GROUP + 1) * HEAD_DIM)
            dvo_ref[:, h * HEAD_DIM:(h + 1) * HEAD_DIM] = _from_dilated(
                scr, dv_refs[gi], DILATIONS[gi], cols).astype(BF16)

    dil = [_dil_spec(d, tm) for d in DILATIONS]
    tab = pl.BlockSpec((tm, HEAD_DIM), lambda i: (i, 0))
    gsp = pl.BlockSpec((1, HEAD_DIM), lambda i: (0, 0))
    wide = pl.BlockSpec((tm, ATTN_W), lambda i: (i, 0))
    return pl.pallas_call(
        body, name="qk_prep_bwd", grid=(S // tm,),
        in_specs=dil * 3 + [wide, pl.BlockSpec((tm, ATTN_W), lambda i: (i, 1)), gsp, gsp, tab, tab, tab],
        out_specs=[wide, wide, wide, pl.BlockSpec((8, HEAD_DIM), lambda i: (0, 0))],
        out_shape=[SDS((S, ATTN_W), BF16)] * 3 + [SDS((8, HEAD_DIM), F32)],
        scratch_shapes=[pltpu.VMEM((tm, HEAD_DIM), F32)],
        compiler_params=_cp("arbitrary"),
    )(*dqs, *dks, *dvs, z, z, g_q, g_k, *tabs)


def _adamw(w, g, m, v, name):
    R, C = w.shape
    tr = _row_tile(R, C * 2)
    c1 = 1.0 - ADAM_B1 ** ADAM_STEP
    c2 = 1.0 - ADAM_B2 ** ADAM_STEP

    def body(w_ref, g_ref, m_ref, v_ref, d_ref, mo_ref, vo_ref):
        gv = g_ref[...]
        mn = ADAM_B1 * m_ref[...] + (1.0 - ADAM_B1) * gv
        vn = ADAM_B2 * v_ref[...] + (1.0 - ADAM_B2) * (gv * gv)
        mo_ref[...] = mn
        vo_ref[...] = vn
        d_ref[...] = -ADAM_LR * ((mn / c1) / (jnp.sqrt(vn / c2) + ADAM_EPS) + ADAM_WD * w_ref[...])

    sp = pl.BlockSpec((tr, C), lambda i: (i, 0))
    return pl.pallas_call(
        body, name=name, grid=(R // tr,), in_specs=[sp] * 4, out_specs=[sp] * 3,
        out_shape=[SDS((R, C), F32)] * 3, compiler_params=_cp("parallel"),
    )(w, g, m, v)


def _sum_parts(parts):
    n, R, C = parts.shape

    def body(p_ref, o_ref):
        acc = p_ref[0]
        for k in range(1, n):
            acc = acc + p_ref[k]
        o_ref[...] = acc

    return pl.pallas_call(
        body, name="sum_parts", out_shape=SDS((R, C), F32),
        compiler_params=pltpu.CompilerParams(vmem_limit_bytes=VMEM_LIMIT_BYTES),
    )(parts)


def _wada_grad(cact_t, dmod):
    D, n = cact_t.shape
    L, _, cols = dmod.shape

    def body(c_ref, d_ref, o_ref):
        acc = c_ref[:, 0:1] * d_ref[0, 0:1, :]
        for k in range(1, n):
            acc = acc + c_ref[:, k:k + 1] * d_ref[0, k:k + 1, :]
        o_ref[0] = acc

    return pl.pallas_call(
        body, name="wada_grad", grid=(L,),
        in_specs=[pl.BlockSpec((D, n), lambda l: (0, 0)), pl.BlockSpec((1, n, cols), lambda l: (l, 0, 0))],
        out_specs=pl.BlockSpec((1, D, cols), lambda l: (l, 0, 0)),
        out_shape=SDS((L, D, cols), F32), compiler_params=_cp("parallel"),
    )(cact_t, dmod)


def _add_halves(dw, land, c_idx, name):
    L, _, _, r2, cols = dw.shape
    tr = _row_tile(r2, cols, step=16)
    dw4 = dw.reshape(L * 4, 2, r2, cols)
    land3 = land.reshape(L * 4, r2, cols)

    def body(pf, a_ref, b_ref, o_ref):
        o_ref[...] = (a_ref[...] + b_ref[...]).astype(BF16)

    out = pl.pallas_call(
        body, name=name,
        grid_spec=pltpu.PrefetchScalarGridSpec(
            num_scalar_prefetch=1, grid=(L * 4, r2 // tr),
            in_specs=[pl.BlockSpec((None, None, tr, cols), lambda a, i, pf: (a, pf[0], i, 0)),
                      pl.BlockSpec((None, tr, cols), lambda a, i, pf: (a, i, 0))],
            out_specs=pl.BlockSpec((None, tr, cols), lambda a, i, pf: (a, i, 0))),
        out_shape=SDS((L * 4, r2, cols), BF16), compiler_params=_cp("parallel", "parallel"),
    )(c_idx, dw4, land3)
    return out.reshape(L, 4, r2, cols)


def _add_chips(dw, land1, land2, cj_idx, name):
    L, _, _, r2, cols = dw.shape
    tr = _row_tile(r2, cols, step=16)

    def body(pf, a_ref, b_ref, c0, c1, c2, o_ref):
        own = a_ref[...] + b_ref[...]
        o_ref[...] = ((own + c0[...].astype(F32)) + c1[...].astype(F32)) + c2[...].astype(F32)

    lsp = lambda p: pl.BlockSpec((None, None, tr, cols), lambda l, i, pf: (p, l, i, 0))
    return pl.pallas_call(
        body, name=name,
        grid_spec=pltpu.PrefetchScalarGridSpec(
            num_scalar_prefetch=1, grid=(L, r2 // tr),
            in_specs=[pl.BlockSpec((None, None, None, tr, cols), lambda l, i, pf: (l, pf[1], pf[0], i, 0)),
                      pl.BlockSpec((None, None, tr, cols), lambda l, i, pf: (l, pf[1], i, 0)),
                      lsp(0), lsp(1), lsp(2)],
            out_specs=pl.BlockSpec((None, None, tr, cols), lambda l, i, pf: (l, pf[0], i, 0))),
        out_shape=SDS((L, 2, r2, cols), F32), compiler_params=_cp("parallel", "parallel"),
    )(cj_idx, dw, land1, land2, land2, land2)


HBM_SPEC = pl.BlockSpec(memory_space=pltpu.HBM)


def _place():
    x, y, c = lax.axis_index("x"), lax.axis_index("y"), lax.axis_index("c")
    chips = [(x, 1 - y), (1 - x, y), (1 - x, 1 - y)]
    return x, y, c, chips


def _remote(src, dst, send_sem, recv_sem, device):
    return pltpu.make_async_remote_copy(src_ref=src, dst_ref=dst, send_sem=send_sem, recv_sem=recv_sem,
                                        device_id=device, device_id_type=MESH)


def _comm_call(body, name, ins, out_shapes, n_sems):
    return pl.pallas_call(
        body, name=name, in_specs=[HBM_SPEC] * len(ins), out_specs=[HBM_SPEC] * len(out_shapes),
        out_shape=out_shapes,
        scratch_shapes=[pltpu.SemaphoreType.DMA((n,)) for n in n_sems],
    )(*ins)


def _all_gather_weights(shards, small):
    nw = len(shards)
    L = shards[0].shape[0]
    Lh = L // 2

    def body(*refs):
        sh, sm = refs[:nw], refs[nw]
        full, smo = refs[nw + 1:2 * nw + 1], refs[2 * nw + 1]
        ici_s, ici_r, d2d_s, d2d_r, loc = refs[2 * nw + 2:]
        x, y, c, chips = _place()
        j = 2 * x + y
        jps = [2 * px + py for px, py in chips]
        me, sib = (x, y, c), (x, y, 1 - c)
        mine, theirs = pl.ds(c * Lh, Lh), pl.ds((1 - c) * Lh, Lh)

        local = [pltpu.make_async_copy(sh[w], full[w].at[j], loc.at[w]) for w in range(nw)]
        local.append(pltpu.make_async_copy(sm, smo.at[j], loc.at[nw]))
        for cp in local:
            cp.start()
        sends = []
        for w in range(nw):
            for p in range(3):
                sends.append(_remote(sh[w].at[mine], full[w].at[j, mine], ici_s.at[3 * w + p], ici_r.at[3 * w + p],
                                     (*chips[p], c)))
        for p in range(3):
            sends.append(_remote(sm, smo.at[j], ici_s.at[3 * nw + p], ici_r.at[3 * nw + p], (*chips[p], c)))
        for cp in sends:
            cp.start()
        passed = []
        for w in range(nw):
            for p in range(3):
                slot = full[w].at[jps[p], mine]
                _remote(slot, slot, ici_s.at[3 * w + p], ici_r.at[3 * w + p], me).wait_recv()
                cp = _remote(slot, slot, d2d_s.at[3 * w + p], d2d_r.at[3 * w + p], sib)
                cp.start()
                passed.append(cp)
        for p in range(3):
            slot = smo.at[jps[p]]
            _remote(slot, slot, ici_s.at[3 * nw + p], ici_r.at[3 * nw + p], me).wait_recv()
        for w in range(nw):
            for p in range(3):
                slot = full[w].at[jps[p], theirs]
                _remote(slot, slot, d2d_s.at[3 * w + p], d2d_r.at[3 * w + p], me).wait_recv()
        for cp in sends + passed:
            cp.wait_send()
        for cp in local:
            cp.wait()

    outs = [SDS((4,) + s.shape, s.dtype) for s in shards] + [SDS((4,) + small.shape, small.dtype)]
    res = _comm_call(body, "all_gather_weights", list(shards) + [small], outs,
                     [3 * nw + 3, 3 * nw + 3, 3 * nw, 3 * nw, nw + 1])
    return res[:nw], res[nw]


def _all_gather_small(part):
    def body(p_ref, o_ref, send_s, recv_s, loc):
        x, y, c, _ = _place()
        me_id = 4 * x + 2 * y + c
        own = pltpu.make_async_copy(p_ref, o_ref.at[me_id], loc.at[0])
        own.start()
        sends = []
        for k in range(1, 8):
            peer = (x ^ (k >> 2), y ^ ((k >> 1) & 1), c ^ (k & 1))
            sends.append(_remote(p_ref, o_ref.at[me_id], send_s.at[k - 1], recv_s.at[k - 1], peer))
        for cp in sends:
            cp.start()
        for k in range(1, 8):
            peer_id = 4 * (x ^ (k >> 2)) + 2 * (y ^ ((k >> 1) & 1)) + (c ^ (k & 1))
            slot = o_ref.at[peer_id]
            _remote(slot, slot, send_s.at[k - 1], recv_s.at[k - 1], (x, y, c)).wait_recv()
        for cp in sends:
            cp.wait_send()
        own.wait()

    return _comm_call(body, "all_gather_small", [part], [SDS((8,) + part.shape, part.dtype)], [7, 7, 1])[0]


def _rs_to_sibling(dws):
    nw = len(dws)

    def body(*refs):
        src, land = refs[:nw], refs[nw:2 * nw]
        send_s, recv_s = refs[2 * nw:]
        x, y, c, _ = _place()
        cps = [_remote(src[w].at[:, :, 1 - c], land[w], send_s.at[w], recv_s.at[w], (x, y, 1 - c))
               for w in range(nw)]
        for cp in cps:
            cp.start()
        for cp in cps:
            cp.wait_recv()
        for cp in cps:
            cp.wait_send()

    outs = [SDS((a.shape[0], 4, a.shape[3], a.shape[4]), a.dtype) for a in dws]
    return _comm_call(body, "rs_to_sibling", list(dws), outs, [nw, nw])


def _rs_to_chips(s1s):
    nw = len(s1s)

    def body(*refs):
        src, land = refs[:nw], refs[nw:2 * nw]
        send_s, recv_s = refs[2 * nw:]
        x, y, c, chips = _place()
        cps = []
        for w in range(nw):
            for p in range(3):
                jp = 2 * chips[p][0] + chips[p][1]
                cps.append(_remote(src[w].at[:, jp], land[w].at[p], send_s.at[3 * w + p], recv_s.at[3 * w + p],
                                   (*chips[p], c)))
        for cp in cps:
            cp.start()
        for cp in cps:
            cp.wait_recv()
        for cp in cps:
            cp.wait_send()

    outs = [SDS((3, a.shape[0], a.shape[2], a.shape[3]), a.dtype) for a in s1s]
    return _comm_call(body, "rs_to_chips", list(s1s), outs, [3 * nw, 3 * nw])


def _rs_join_halves(fulls):
    nw = len(fulls)

    def body(*refs):
        full = refs[nw:2 * nw]
        send_s, recv_s = refs[2 * nw:]
        x, y, c, _ = _place()
        cps = [_remote(full[w].at[:, c], full[w].at[:, c], send_s.at[w], recv_s.at[w], (x, y, 1 - c))
               for w in range(nw)]
        for cp in cps:
            cp.start()
        for w in range(nw):
            slot = full[w].at[:, 1 - c]
            _remote(slot, slot, send_s.at[w], recv_s.at[w], (x, y, c)).wait_recv()
        for cp in cps:
            cp.wait_send()

    return pl.pallas_call(
        body, name="rs_join_halves", in_specs=[HBM_SPEC] * nw, out_specs=[HBM_SPEC] * nw,
        out_shape=[SDS(a.shape, a.dtype) for a in fulls],
        scratch_shapes=[pltpu.SemaphoreType.DMA((nw,)), pltpu.SemaphoreType.DMA((nw,))],
        input_output_aliases={w: w for w in range(nw)},
    )(*fulls)


BIG = ("w_in", "w_attn_proj", "w_conv_out", "w_o", "w_ffn_in", "w_ffn_down")
COL_SHARDED = {"w_in": True, "w_attn_proj": True, "w_conv_out": False, "w_o": False,
               "w_ffn_in": True, "w_ffn_down": False}
SMALL = ("b_ada", "g_norm1", "g_q", "g_k", "w_conv_dw", "b_conv_dw", "g_conv_ln", "b_conv_ln",
         "g_norm2", "w_ffn_dw", "b_ffn_dw")
WEIGHTS = ("w_ada", "b_ada", "g_norm1", "w_in", "g_q", "g_k", "w_attn_proj", "w_conv_dw", "b_conv_dw",
           "g_conv_ln", "b_conv_ln", "w_conv_out", "w_o", "g_norm2", "w_ffn_in", "w_ffn_dw", "b_ffn_dw",
           "w_ffn_down")


def _pack_rows(arrs):
    flat = jnp.concatenate([a.reshape(-1) for a in arrs])
    pad = (-flat.shape[0]) % (8 * 128)
    return jnp.pad(flat, (0, pad)).reshape(-1, 128)


def _unpack_rows(packed, shapes):
    flat = packed.reshape(-1)
    out, off = [], 0
    for shp in shapes:
        n = 1
        for s in shp:
            n *= s
        out.append(flat[off:off + n].reshape(shp))
        off += n
    return out


def _rope_tables(positions):
    half = ROT_DIM // 2
    inv_freq = ROPE_THETA ** (-jnp.arange(0, ROT_DIM, 2, dtype=F32) / ROT_DIM)
    ang = positions.astype(F32)[..., None] * inv_freq
    cos, sin = jnp.cos(ang), jnp.sin(ang)
    S = cos.shape[0]
    cos_t = jnp.concatenate([cos, cos, jnp.ones((S, HEAD_DIM - ROT_DIM), F32)], axis=1)
    sa_t = jnp.concatenate([-sin, jnp.zeros((S, HEAD_DIM - half), F32)], axis=1)
    sb_t = jnp.concatenate([jnp.zeros((S, half), F32), sin, jnp.zeros((S, HEAD_DIM - ROT_DIM), F32)], axis=1)
    return cos_t, sa_t, sb_t


def _local_step(x, c, positions, loss_target, wf, small_w):
    S, D = x.shape
    L = wf["w_in"].shape[0]
    tabs = _rope_tables(positions)
    mod, cact = _mod_fwd(c, wf["w_ada"], small_w["b_ada"])
    vec = lambda a: a.reshape(1, -1)
    mods = [[vec(mod[l, k * D:(k + 1) * D]) for k in range(6)] for l in range(L)]

    saved = []
    xs = x
    for l in range(L):
        sh1, sc1, gt1, sh2, sc2, gt2 = mods[l]
        g1, g2 = vec(small_w["g_norm1"][l]), vec(small_w["g_norm2"][l])
        gq, gk = vec(small_w["g_q"][l]), vec(small_w["g_k"][l])
        z, h = _normmod_mm(xs, g1, sc1, sh1, wf["w_in"], l, "in_proj")
        qs, ks, vs = _qk_prep(z, gq, gk, tabs)
        outs, lses = [], []
        for gi in range(len(DILATIONS)):
            o_g, lse_g = _attn_fwd(qs[gi], ks[gi], vs[gi], gi)
            outs.append(o_g)
            lses.append(lse_g)
        attn = _attn_combine(outs, lses)
        y_a = _mm_nn(attn, wf["w_attn_proj"], l, "attn_proj")
        u1 = _glu_conv(z, small_w["w_conv_dw"], small_w["b_conv_dw"], l, D)
        u3 = _ln_silu(u1, vec(small_w["g_conv_ln"][l]), vec(small_w["b_conv_ln"][l]))
        y_b = _mm_nn(u3, wf["w_conv_out"], l, "conv_out")
        merged = _merge(z, y_a, y_b, D)
        t_o, x2 = _mm_nn(merged, wf["w_o"], l, "out_proj", res=xs, gate=gt1)
        gu, h2 = _normmod_mm(x2, g2, sc2, sh2, wf["w_ffn_in"], l, "ffn_in")
        a = _ffn_act(gu, small_w["w_ffn_dw"], small_w["b_ffn_dw"], l)
        t_f, x3 = _mm_nn(a, wf["w_ffn_down"], l, "ffn_down", res=x2, gate=gt2)
        saved.append(dict(x=xs, z=z, h=h, qs=qs, ks=ks, vs=vs, outs=outs, lses=lses, attn=attn, y_a=y_a, u1=u1, u3=u3,
                          y_b=y_b, merged=merged, t_o=t_o, x2=x2, gu=gu, h2=h2, a=a, t_f=t_f))
        xs = x3

    dx, loss = _loss_grad(xs, loss_target)

    bufs = {n: None for n in BIG}
    small_g = {n: [None] * L for n in SMALL}
    dmods = [None] * L

    def wgrad(name, a_mat, b_mat, l):
        bufs[name] = _mm_tn(a_mat, b_mat, bufs[name], l, L, COL_SHARDED[name], "grad_" + name)

    for l in reversed(range(L)):
        sv = saved[l]
        sh1, sc1, gt1, sh2, sc2, gt2 = mods[l]
        g1, g2 = vec(small_w["g_norm1"][l]), vec(small_w["g_norm2"][l])
        gq, gk = vec(small_w["g_q"][l]), vec(small_w["g_k"][l])
        e2, dgt2 = _gate_bwd(dx, sv["t_f"], gt2, "gate_bwd_ffn")
        wgrad("w_ffn_down", sv["a"], e2, l)
        da = _mm_nt(e2, wf["w_ffn_down"], l, "ffn_down_bwd")
        dg, du, ffn_sums = _ffn_act_bwd(da, sv["gu"], small_w["w_ffn_dw"], small_w["b_ffn_dw"], l)
        dgu = (dg, du)
        wgrad("w_ffn_in", sv["h2"], dgu, l)
        dh2 = _mm_nt(dgu, wf["w_ffn_in"], l, "ffn_in_bwd")
        dx2, n2_sums = _norm_bwd(dh2, sv["x2"], dx, g2, sc2, "norm2_bwd")
        e1, dgt1 = _gate_bwd(dx2, sv["t_o"], gt1, "gate_bwd_mix")
        wgrad("w_o", sv["merged"], e1, l)
        dmerged = _mm_nt(e1, wf["w_o"], l, "out_proj_bwd")
        dya, dyb, dga, dgb = _merge_bwd(dmerged, sv["z"], sv["y_a"], sv["y_b"], D)
        wgrad("w_attn_proj", sv["attn"], dya, l)
        wgrad("w_conv_out", sv["u3"], dyb, l)
        dattn = _mm_nt(dya, wf["w_attn_proj"], l, "attn_proj_bwd")
        du3 = _mm_nt(dyb, wf["w_conv_out"], l, "conv_out_bwd")
        du1, ln_sums = _ln_silu_bwd(du3, sv["u1"], vec(small_w["g_conv_ln"][l]), vec(small_w["b_conv_ln"][l]))
        dcv, dcg, conv_sums = _glu_conv_bwd(du1, sv["z"], small_w["w_conv_dw"], l, D)
        dos, dterms = _attn_bwd_prep(dattn, sv["outs"], sv["lses"])
        dqs, dks, dvs = [], [], []
        for gi in range(len(DILATIONS)):
            dq_g, dk_g, dv_g = _attn_bwd(sv["qs"][gi], sv["ks"][gi], sv["vs"][gi], dos[gi], sv["lses"][gi],
                                         dterms[gi], gi)
            dqs.append(dq_g)
            dks.append(dk_g)
            dvs.append(dv_g)
        dq, dk, dv, qk_sums = _qk_prep_bwd(dqs, dks, dvs, sv["z"], gq, gk, tabs)
        dz = jnp.concatenate([dq, dk, dv, dcv, dcg, dga, dgb], axis=1)
        wgrad("w_in", sv["h"], dz, l)
        dh = _mm_nt(dz, wf["w_in"], l, "in_proj_bwd")
        dx, n1_sums = _norm_bwd(dh, sv["x"], dx2, g1, sc1, "norm1_bwd")

        dmods[l] = jnp.concatenate([n1_sums[0], n1_sums[1], dgt1[0], n2_sums[0], n2_sums[1], dgt2[0]])
        small_g["b_ada"][l] = dmods[l]
        small_g["g_norm1"][l] = n1_sums[2]
        small_g["g_q"][l] = qk_sums[0]
        small_g["g_k"][l] = qk_sums[1]
        small_g["w_conv_dw"][l] = conv_sums[:CONV_K]
        small_g["b_conv_dw"][l] = conv_sums[CONV_K]
        small_g["g_conv_ln"][l] = ln_sums[0]
        small_g["b_conv_ln"][l] = ln_sums[1]
        small_g["g_norm2"][l] = n2_sums[2]
        small_g["w_ffn_dw"][l] = ffn_sums[:FFN_K]
        small_g["b_ffn_dw"][l] = ffn_sums[FFN_K]

    small_g = {n: jnp.stack(v) for n, v in small_g.items()}
    return loss, dx, bufs, small_g, cact


def kernel(x, c, positions, w_ada, b_ada, g_norm1, w_in, g_q, g_k, w_attn_proj, w_conv_dw, b_conv_dw, g_conv_ln, b_conv_ln, w_conv_out, w_o, g_norm2, w_ffn_in, w_ffn_dw, b_ffn_dw, w_ffn_down, loss_target, m_w_ada, m_b_ada, m_g_norm1, m_w_in, m_g_q, m_g_k, m_w_attn_proj, m_w_conv_dw, m_b_conv_dw, m_g_conv_ln, m_b_conv_ln, m_w_conv_out, m_w_o, m_g_norm2, m_w_ffn_in, m_w_ffn_dw, m_b_ffn_dw, m_w_ffn_down, v_w_ada, v_b_ada, v_g_norm1, v_w_in, v_g_q, v_g_k, v_w_attn_proj, v_w_conv_dw, v_b_conv_dw, v_g_conv_ln, v_b_conv_ln, v_w_conv_out, v_w_o, v_g_norm2, v_w_ffn_in, v_w_ffn_dw, v_b_ffn_dw, v_w_ffn_down):
    w = dict(w_ada=w_ada, b_ada=b_ada, g_norm1=g_norm1, w_in=w_in, g_q=g_q, g_k=g_k, w_attn_proj=w_attn_proj,
             w_conv_dw=w_conv_dw, b_conv_dw=b_conv_dw, g_conv_ln=g_conv_ln, b_conv_ln=b_conv_ln,
             w_conv_out=w_conv_out, w_o=w_o, g_norm2=g_norm2, w_ffn_in=w_ffn_in, w_ffn_dw=w_ffn_dw,
             b_ffn_dw=b_ffn_dw, w_ffn_down=w_ffn_down)
    m = dict(w_ada=m_w_ada, b_ada=m_b_ada, g_norm1=m_g_norm1, w_in=m_w_in, g_q=m_g_q, g_k=m_g_k,
             w_attn_proj=m_w_attn_proj, w_conv_dw=m_w_conv_dw, b_conv_dw=m_b_conv_dw, g_conv_ln=m_g_conv_ln,
             b_conv_ln=m_b_conv_ln, w_conv_out=m_w_conv_out, w_o=m_w_o, g_norm2=m_g_norm2, w_ffn_in=m_w_ffn_in,
             w_ffn_dw=m_w_ffn_dw, b_ffn_dw=m_b_ffn_dw, w_ffn_down=m_w_ffn_down)
    v = dict(w_ada=v_w_ada, b_ada=v_b_ada, g_norm1=v_g_norm1, w_in=v_w_in, g_q=v_g_q, g_k=v_g_k,
             w_attn_proj=v_w_attn_proj, w_conv_dw=v_w_conv_dw, b_conv_dw=v_b_conv_dw, g_conv_ln=v_g_conv_ln,
             b_conv_ln=v_b_conv_ln, w_conv_out=v_w_conv_out, w_o=v_w_o, g_norm2=v_g_norm2, w_ffn_in=v_w_ffn_in,
             w_ffn_dw=v_w_ffn_dw, b_ffn_dw=v_b_ffn_dw, w_ffn_down=v_w_ffn_down)
    xi, yi, ci = lax.axis_index("x"), lax.axis_index("y"), lax.axis_index("c")
    j = 2 * xi + yi
    L = w_ada.shape[0]
    D = x.shape[-1]

    mats = ("w_ada",) + BIG
    col = dict(COL_SHARDED, w_ada=True)
    shards = [w[n].astype(BF16) for n in mats]
    conv_shapes = [w["w_conv_dw"].shape, w["w_ffn_dw"].shape]
    gathered, small_g4 = _all_gather_weights(shards, _pack_rows([w["w_conv_dw"], w["w_ffn_dw"]]))
    wf = {}
    for n, g4 in zip(mats, gathered):
        _, _, r, cols = g4.shape
        if col[n]:
            wf[n] = g4.transpose(1, 2, 0, 3).reshape(L, r, 4 * cols)
        else:
            wf[n] = g4.transpose(1, 0, 2, 3).reshape(L, 4 * r, cols)
    conv_parts = [_unpack_rows(small_g4[k], conv_shapes) for k in range(4)]
    small_w = {n: w[n] for n in SMALL}
    small_w["w_conv_dw"] = jnp.concatenate([p[0] for p in conv_parts], axis=2)
    small_w["w_ffn_dw"] = jnp.concatenate([p[1] for p in conv_parts], axis=2)

    loss, dx, bufs, small_g, cact = _local_step(x[0], c, positions[0], loss_target[0], wf, small_w)
    loss = lax.psum(loss, ("x", "y", "c"))

    dws = [bufs[n] for n in BIG]
    land1 = _rs_to_sibling(dws)
    c_idx = jnp.reshape(ci, (1,)).astype(jnp.int32)
    s1s = [_add_halves(a, b, c_idx, "rs_add_halves") for a, b in zip(dws, land1)]
    land2 = _rs_to_chips(s1s)
    cj_idx = jnp.stack([ci, j]).astype(jnp.int32)
    halves = [_add_chips(a, b, e, cj_idx, "rs_add_chips") for a, b, e in zip(dws, land1, land2)]
    joined = _rs_join_halves(halves)
    grads = {n: g.reshape(w[n].shape) for n, g in zip(BIG, joined)}

    part = _pack_rows([small_g[n] for n in SMALL] + [cact])
    parts = _all_gather_small(part)
    full_shapes = [small_g[n].shape for n in SMALL]
    summed = _unpack_rows(_sum_parts(parts), full_shapes)
    for n, g in zip(SMALL, summed):
        if n in ("w_conv_dw", "w_ffn_dw"):
            cols = w[n].shape[2]
            g = lax.dynamic_slice_in_dim(g, j * cols, cols, axis=2)
        grads[n] = g
    flat = parts.reshape(8, -1)
    off, dmod_all = 0, None
    for n, shp in zip(SMALL, full_shapes):
        size = 1
        for s in shp:
            size *= s
        if n == "b_ada":
            dmod_all = flat[:, off:off + size].reshape(8, L, 6 * D)
        off += size
    cact_all = flat[:, off:off + D]
    cols = w_ada.shape[2]
    dmod_cols = lax.dynamic_slice_in_dim(dmod_all, j * cols, cols, axis=2).transpose(1, 0, 2)
    grads["w_ada"] = _wada_grad(cact_all.T, dmod_cols)

    delta, new_m, new_v = {}, {}, {}
    for n in ("w_ada",) + BIG:
        shp = w[n].shape
        two_d = lambda a: a.reshape(shp[0] * shp[1], shp[2])
        dl, mn, vn = _adamw(two_d(w[n]), two_d(grads[n]), two_d(m[n]), two_d(v[n]), "adamw_" + n)
        delta[n], new_m[n], new_v[n] = dl.reshape(shp), mn.reshape(shp), vn.reshape(shp)
    small_shapes = [w[n].shape for n in SMALL]
    packs = [_pack_rows([d[n] for n in SMALL]) for d in (w, grads, m, v)]
    outs = _adamw(*packs, "adamw_small")
    for d, packed in zip((delta, new_m, new_v), outs):
        for n, a in zip(SMALL, _unpack_rows(packed, small_shapes)):
            d[n] = a

    return (loss, dx[None], *[grads[n] for n in WEIGHTS], *[delta[n] for n in WEIGHTS],
            *[new_m[n] for n in WEIGHTS], *[new_v[n] for n in WEIGHTS])
```

```python
import functools

import jax
import jax.numpy as jnp
from jax import lax
from jax.experimental import pallas as pl
from jax.experimental.pallas import tpu as pltpu

F32 = jnp.float32
BF16 = jnp.bfloat16
SDS = jax.ShapeDtypeStruct
MESH = pl.DeviceIdType.MESH

HEAD_DIM = 128
BLOCK = 128
HEADS_PER_GROUP = 4
GROUP_W = HEADS_PER_GROUP * HEAD_DIM
DILATIONS = (1, 4, 16)
ATTN_W = len(DILATIONS) * GROUP_W
ROT_DIM = HEAD_DIM // 4
ROPE_THETA = 500000.0
CONV_K = 31
CONV_HALO = 32
FFN_K = 3
FFN_HALO = 8
EPS = 1e-6
NEG = -1e30
CHUNK = 512

ADAM_LR = 0.001
ADAM_B1 = 0.9
ADAM_B2 = 0.999
ADAM_EPS = 1e-08
ADAM_WD = 0.01
ADAM_STEP = 10

VMEM_LIMIT_BYTES = 48 * 1024 * 1024
TILE_BYTES = 2 * 1024 * 1024

NT_DIMS = (((1,), (1,)), ((), ()))
TN_DIMS = (((0,), (0,)), ((), ()))


def _cp(*sem):
    return pltpu.CompilerParams(dimension_semantics=sem if sem else None,
                                vmem_limit_bytes=VMEM_LIMIT_BYTES)


def _tm(s, cap=512):
    return min(cap, s)


def _row_tile(rows, cols, itemsize=4, step=8):
    best = None
    for t in range(step, rows + 1, step):
        if rows % t == 0 and t * cols * itemsize <= TILE_BYTES:
            best = t
    return best if best is not None else rows


def _quarter(n):
    return n // 4 if n % (4 * HEAD_DIM) == 0 else CHUNK


def _sigmoid(v):
    return jax.nn.sigmoid(v)


def _mod_fwd(c, w_ada, b_ada):
    L, D, N6 = w_ada.shape
    tn = N6 // 4

    def body(c_ref, w_ref, b_ref, mod_ref, cact_ref):
        cv = c_ref[...]
        ca = cv * _sigmoid(cv)
        cact_ref[...] = ca
        a8 = jnp.broadcast_to(ca, (8, D)).astype(BF16)
        acc = jnp.dot(a8, w_ref[0], preferred_element_type=F32)
        mod_ref[0] = acc[0:1, :] + b_ref[0]

    mod, cact = pl.pallas_call(
        body, name="mod_fwd", grid=(L, 4),
        in_specs=[pl.BlockSpec((1, D), lambda l, j: (0, 0)),
                  pl.BlockSpec((1, D, tn), lambda l, j: (l, 0, j)),
                  pl.BlockSpec((1, 1, tn), lambda l, j: (l, 0, j))],
        out_specs=[pl.BlockSpec((1, 1, tn), lambda l, j: (l, 0, j)),
                   pl.BlockSpec((1, D), lambda l, j: (0, 0))],
        out_shape=[SDS((L, 1, N6), F32), SDS((1, D), F32)],
        compiler_params=_cp("arbitrary", "arbitrary"),
    )(c, w_ada, b_ada.reshape(L, 1, N6))
    return mod.reshape(L, N6), cact


def _normmod_mm(x, g, sc, sh, w, l, name):
    S, D = x.shape
    N = w.shape[2]
    tm, tn = _tm(S, 1024), _quarter(N)

    def body(x_ref, g_ref, sc_ref, sh_ref, w_ref, z_ref, h_ref, hs):
        @pl.when(pl.program_id(1) == 0)
        def _():
            xv = x_ref[...]
            r = lax.rsqrt(jnp.mean(xv * xv, axis=-1, keepdims=True) + EPS)
            hv = (xv * r) * g_ref[...] * (1.0 + sc_ref[...]) + sh_ref[...]
            hs[...] = hv.astype(BF16)
            h_ref[...] = hs[...]
        z_ref[...] = jnp.dot(hs[...], w_ref[0], preferred_element_type=F32)

    vec = pl.BlockSpec((1, D), lambda i, j: (0, 0))
    return pl.pallas_call(
        body, name=name, grid=(S // tm, N // tn),
        in_specs=[pl.BlockSpec((tm, D), lambda i, j: (i, 0)), vec, vec, vec,
                  pl.BlockSpec((1, D, tn), lambda i, j: (l, 0, j))],
        out_specs=[pl.BlockSpec((tm, tn), lambda i, j: (i, j)),
                   pl.BlockSpec((tm, D), lambda i, j: (i, 0))],
        out_shape=[SDS((S, N), F32), SDS((S, D), BF16)],
        scratch_shapes=[pltpu.VMEM((tm, D), BF16)],
        compiler_params=_cp("parallel", "arbitrary"),
    )(x, g, sc, sh, w)


def _rope(t, cos_t, sa_t, sb_t):
    return t * cos_t + pltpu.roll(t, HEAD_DIM - ROT_DIM // 2, 1) * sa_t + pltpu.roll(t, ROT_DIM // 2, 1) * sb_t


def _unrope(d, cos_t, sa_t, sb_t):
    return d * cos_t + pltpu.roll(d * sa_t, ROT_DIM // 2, 1) + pltpu.roll(d * sb_t, HEAD_DIM - ROT_DIM // 2, 1)


def _dil_spec(d, tm):
    return pl.BlockSpec((d, tm // d, GROUP_W), lambda i: (0, i, 0))


def _to_dilated(scr, val, dst_ref, d, cols):
    if d == 1:
        dst_ref[0, :, cols] = val.astype(dst_ref.dtype)
        return
    n = val.shape[0] // d
    scr[...] = val
    for r in range(d):
        dst_ref[r, :, cols] = scr[pl.ds(r, n, stride=d), :].astype(dst_ref.dtype)


def _from_dilated(scr, src_ref, d, cols):
    if d == 1:
        return src_ref[0, :, cols].astype(F32)
    n = src_ref.shape[1]
    for r in range(d):
        scr[pl.ds(r, n, stride=d), :] = src_ref[r, :, cols].astype(F32)
    return scr[...]


def _qk_prep(z, g_q, g_k, tabs):
    S = z.shape[0]
    tm = _tm(S)
    n_heads = ATTN_W // HEAD_DIM
    ng = len(DILATIONS)

    def body(q_ref, k_ref, v_ref, gq_ref, gk_ref, c_ref, sa_ref, sb_ref, *rest):
        outs, scr = rest[:3 * ng], rest[3 * ng]
        cos_t, sa_t, sb_t = c_ref[...], sa_ref[...], sb_ref[...]
        for wi, (src, gref) in enumerate(((q_ref, gq_ref), (k_ref, gk_ref), (v_ref, None))):
            for h in range(n_heads):
                gi = h // HEADS_PER_GROUP
                cols = slice((h % HEADS_PER_GROUP) * HEAD_DIM, (h % HEADS_PER_GROUP + 1) * HEAD_DIM)
                t = src[:, h * HEAD_DIM:(h + 1) * HEAD_DIM]
                if gref is not None:
                    r = lax.rsqrt(jnp.mean(t * t, axis=-1, keepdims=True) + EPS)
                    t = _rope((t * r) * gref[...], cos_t, sa_t, sb_t)
                _to_dilated(scr, t, outs[wi * ng + gi], DILATIONS[gi], cols)

    tab = pl.BlockSpec((tm, HEAD_DIM), lambda i: (i, 0))
    gsp = pl.BlockSpec((1, HEAD_DIM), lambda i: (0, 0))
    wide = lambda cb: pl.BlockSpec((tm, ATTN_W), lambda i: (i, cb))
    out = pl.pallas_call(
        body, name="qk_prep", grid=(S // tm,),
        in_specs=[wide(0), wide(1), wide(2), gsp, gsp, tab, tab, tab],
        out_specs=[_dil_spec(d, tm) for _ in range(3) for d in DILATIONS],
        out_shape=[SDS((d, S // d, GROUP_W), BF16) for _ in range(3) for d in DILATIONS],
        scratch_shapes=[pltpu.VMEM((tm, HEAD_DIM), F32)],
        compiler_params=_cp("parallel"),
    )(z, z, z, g_q, g_k, *tabs)
    return out[:ng], out[ng:2 * ng], out[2 * ng:]


def _attn_blk(nb, shift):
    if shift < 0:
        return pl.BlockSpec((None, BLOCK, GROUP_W), lambda r, n: (r, jnp.maximum(n - 1, 0), 0))
    if shift > 0:
        return pl.BlockSpec((None, BLOCK, GROUP_W), lambda r, n: (r, jnp.minimum(n + 1, nb - 1), 0))
    return pl.BlockSpec((None, BLOCK, GROUP_W), lambda r, n: (r, n, 0))


def _band_mask(has_prev):
    qi = lax.broadcasted_iota(jnp.int32, (BLOCK, 2 * BLOCK), 0)
    kj = lax.broadcasted_iota(jnp.int32, (BLOCK, 2 * BLOCK), 1)
    band = jnp.logical_and(kj >= qi, kj <= qi + BLOCK)
    return jnp.logical_and(band, jnp.logical_or(has_prev, kj >= BLOCK))


def _attn_fwd(q, k, v, gi):
    d, Sd, _ = q.shape
    nb = Sd // BLOCK
    scale = HEAD_DIM ** -0.5

    def body(q_ref, kc_ref, kp_ref, vc_ref, vp_ref, o_ref, lse_ref, kk, vv):
        n = pl.program_id(1)
        kk[0:BLOCK, :], kk[BLOCK:, :] = kp_ref[...], kc_ref[...]
        vv[0:BLOCK, :], vv[BLOCK:, :] = vp_ref[...], vc_ref[...]
        mask = _band_mask(n > 0)
        for h in range(HEADS_PER_GROUP):
            sl = slice(h * HEAD_DIM, (h + 1) * HEAD_DIM)
            s = lax.dot_general(q_ref[:, sl], kk[:, sl], NT_DIMS, preferred_element_type=F32) * scale
            s = jnp.where(mask, s, NEG)
            m = jnp.max(s, axis=-1, keepdims=True)
            p = jnp.exp(s - m)
            den = jnp.sum(p, axis=-1, keepdims=True)
            acc = jnp.dot(p.astype(BF16), vv[:, sl], preferred_element_type=F32)
            o_ref[:, sl] = acc / den
            lse_ref[:, sl] = jnp.broadcast_to(m + jnp.log(den), (BLOCK, HEAD_DIM))

    cur, prev = _attn_blk(nb, 0), _attn_blk(nb, -1)
    return pl.pallas_call(
        body, name=f"attn_fwd_g{gi}", grid=(d, nb),
        in_specs=[cur, cur, prev, cur, prev], out_specs=[cur] * 2,
        out_shape=[SDS((d, Sd, GROUP_W), F32)] * 2,
        scratch_shapes=[pltpu.VMEM((2 * BLOCK, GROUP_W), BF16)] * 2,
        compiler_params=_cp("parallel", "parallel"),
    )(q, k, k, v, v)


def _group_weights(lses):
    m = jnp.maximum(jnp.maximum(lses[0], lses[1]), lses[2])
    es = [jnp.exp(v - m) for v in lses]
    inv = 1.0 / (es[0] + es[1] + es[2])
    return [e * inv for e in es]


def _attn_combine(outs, lses):
    S = outs[0].shape[0] * outs[0].shape[1]
    tm = _tm(S)
    ng = len(DILATIONS)

    def body(*refs):
        o_refs, l_refs, a_ref, scr = refs[:ng], refs[ng:2 * ng], refs[2 * ng], refs[2 * ng + 1:]
        for h in range(HEADS_PER_GROUP):
            cols = slice(h * HEAD_DIM, (h + 1) * HEAD_DIM)
            ov = [_from_dilated(scr[g], o_refs[g], DILATIONS[g], cols) for g in range(ng)]
            lv = [_from_dilated(scr[ng + g], l_refs[g], DILATIONS[g], cols) for g in range(ng)]
            w = _group_weights(lv)
            a_ref[:, cols] = (w[0] * ov[0] + w[1] * ov[1] + w[2] * ov[2]).astype(BF16)

    dil = [_dil_spec(d, tm) for d in DILATIONS]
    return pl.pallas_call(
        body, name="attn_combine", grid=(S // tm,), in_specs=dil + dil,
        out_specs=pl.BlockSpec((tm, GROUP_W), lambda i: (i, 0)),
        out_shape=SDS((S, GROUP_W), BF16),
        scratch_shapes=[pltpu.VMEM((tm, HEAD_DIM), F32)] * (2 * ng),
        compiler_params=_cp("parallel"),
    )(*outs, *lses)


def _mm_nn(a, w, l, name, res=None, gate=None):
    S, K = a.shape
    N = w.shape[2]
    tm = _tm(S)
    gated = res is not None

    def body(*refs):
        if gated:
            a_ref, w_ref, r_ref, g_ref, t_ref, o_ref = refs
        else:
            a_ref, w_ref, t_ref = refs
        t = jnp.dot(a_ref[...], w_ref[0], preferred_element_type=F32)
        t_ref[...] = t
        if gated:
            o_ref[...] = r_ref[...] + g_ref[...] * t

    row = pl.BlockSpec((tm, N), lambda i: (i, 0))
    in_specs = [pl.BlockSpec((tm, K), lambda i: (i, 0)), pl.BlockSpec((1, K, N), lambda i: (l, 0, 0))]
    args = [a, w]
    if gated:
        in_specs += [row, pl.BlockSpec((1, N), lambda i: (0, 0))]
        args += [res, gate]
    n_out = 2 if gated else 1
    out = pl.pallas_call(
        body, name=name, grid=(S // tm,), in_specs=in_specs, out_specs=[row] * n_out,
        out_shape=[SDS((S, N), F32)] * n_out, compiler_params=_cp("parallel"),
    )(*args)
    return out if gated else out[0]


def _glu(cv, cg):
    return cv * _sigmoid(cg)


STRIP = 16
SHIFTS = 8


def _fill_shifted(sh, ext, rows):
    sh[0] = ext[...]
    for b in range(1, SHIFTS):
        sh[b, 0:rows, :] = ext[pl.ds(b, rows), :]


def _tap(sh, base, off, rows=STRIP):
    return sh[off % SHIFTS, pl.ds(base + SHIFTS * (off // SHIFTS), rows), :]


def _glu_conv(z, w_dw, b_dw, l, D):
    S = z.shape[0]
    tm = _tm(S)
    cc0 = 3 * ATTN_W // CHUNK
    ncc = D // CHUNK
    hb = tm // CONV_HALO
    lead = CONV_HALO - (CONV_K - 1)

    def body(cv_ref, cg_ref, hv_ref, hg_ref, w_ref, b_ref, o_ref, ext, sh):
        i = pl.program_id(1)
        ext[0:CONV_HALO, :] = jnp.where(i > 0, _glu(hv_ref[...], hg_ref[...]), 0.0)
        ext[CONV_HALO:, :] = _glu(cv_ref[...], cg_ref[...])
        _fill_shifted(sh, ext, tm + CONV_HALO - SHIFTS)

        def strip(s, carry):
            base = s * STRIP
            acc = jnp.broadcast_to(b_ref[...], (STRIP, CHUNK))
            for k in range(CONV_K):
                acc = acc + w_ref[0, k:k + 1, :] * _tap(sh, base, lead + k)
            o_ref[pl.ds(base, STRIP), :] = acc
            return carry

        _unrolled(tm // STRIP, strip, 0)

    cur = lambda off: pl.BlockSpec((tm, CHUNK), lambda cc, i: (i, cc0 + off + cc))
    halo = lambda off: pl.BlockSpec((CONV_HALO, CHUNK), lambda cc, i: (jnp.maximum(i * hb - 1, 0), cc0 + off + cc))
    return pl.pallas_call(
        body, name="glu_conv", grid=(ncc, S // tm),
        in_specs=[cur(0), cur(ncc), halo(0), halo(ncc),
                  pl.BlockSpec((1, CONV_K, CHUNK), lambda cc, i: (l, 0, cc)),
                  pl.BlockSpec((None, 1, CHUNK), lambda cc, i: (l, 0, cc))],
        out_specs=pl.BlockSpec((tm, CHUNK), lambda cc, i: (i, cc)),
        out_shape=SDS((S, D), F32),
        scratch_shapes=[pltpu.VMEM((tm + CONV_HALO, CHUNK), F32),
                        pltpu.VMEM((SHIFTS, tm + CONV_HALO, CHUNK), F32)],
        compiler_params=_cp("parallel", "parallel"),
    )(z, z, z, z, w_dw, b_dw[:, None, :])


def _ln_stats(u):
    mu = jnp.mean(u, axis=-1, keepdims=True)
    xc = u - mu
    rstd = lax.rsqrt(jnp.mean(xc * xc, axis=-1, keepdims=True) + EPS)
    return xc * rstd, rstd


def _ln_silu(u1, g, b):
    S, D = u1.shape
    tm = _tm(S)

    def body(u_ref, g_ref, b_ref, o_ref):
        xh, _ = _ln_stats(u_ref[...])
        yv = xh * g_ref[...] + b_ref[...]
        o_ref[...] = (yv * _sigmoid(yv)).astype(BF16)

    vec = pl.BlockSpec((1, D), lambda i: (0, 0))
    return pl.pallas_call(
        body, name="ln_silu", grid=(S // tm,),
        in_specs=[pl.BlockSpec((tm, D), lambda i: (i, 0)), vec, vec],
        out_specs=pl.BlockSpec((tm, D), lambda i: (i, 0)),
        out_shape=SDS((S, D), BF16), compiler_params=_cp("parallel"),
    )(u1, g, b)


def _merge(z, y_a, y_b, D):
    S = z.shape[0]
    tm = _tm(S)
    ncc = D // CHUNK
    ga0 = (3 * ATTN_W + 2 * D) // CHUNK

    def body(ga_ref, gb_ref, ya_ref, yb_ref, o_ref):
        o_ref[...] = (_sigmoid(ga_ref[...]) * ya_ref[...] + _sigmoid(gb_ref[...]) * yb_ref[...]).astype(BF16)

    ysp = pl.BlockSpec((tm, CHUNK), lambda i, cc: (i, cc))
    return pl.pallas_call(
        body, name="merge", grid=(S // tm, ncc),
        in_specs=[pl.BlockSpec((tm, CHUNK), lambda i, cc: (i, ga0 + cc)),
                  pl.BlockSpec((tm, CHUNK), lambda i, cc: (i, ga0 + ncc + cc)), ysp, ysp],
        out_specs=ysp, out_shape=SDS((S, D), BF16), compiler_params=_cp("parallel", "parallel"),
    )(z, z, y_a, y_b)


def _ffn_chunk(F):
    best = 128
    for t in range(128, min(F, 1536) + 1, 128):
        if F % t == 0:
            best = t
    return best


FFN_ROWS = 32


def _unrolled(n, step, carry):
    for s in range(n):
        carry = step(s, carry)
    return carry


def _ffn_shift(sh, ext, rows):
    for k in range(FFN_K - 1):
        sh[k, 0:rows, :] = ext[pl.ds(FFN_HALO - (FFN_K - 1) + k, rows), :]


def _ffn_taps(sh, ext, base, ls):
    rows = pl.ds(base, FFN_ROWS)
    return [sh[k, rows, ls] for k in range(FFN_K - 1)] + [ext[pl.ds(base + FFN_HALO, FFN_ROWS), ls]]


def _ffn_act(gu, w_dw, b_dw, l):
    S, F2 = gu.shape
    F = F2 // 2
    cw = _ffn_chunk(F)
    ncc = F // cw
    tm = _tm(S)
    hb = tm // FFN_HALO

    def body(g_ref, u_ref, h_ref, w_ref, b_ref, o_ref, ext, sh):
        i = pl.program_id(1)
        ext[0:FFN_HALO, :] = jnp.where(i > 0, h_ref[...], 0.0)
        ext[FFN_HALO:, :] = g_ref[...]
        _ffn_shift(sh, ext, tm)
        for cg in range(cw // HEAD_DIM):
            ls = slice(cg * HEAD_DIM, (cg + 1) * HEAD_DIM)
            wv = [w_ref[0, k:k + 1, ls] for k in range(FFN_K)]
            bv = b_ref[:, ls]

            def strip(s, carry, ls=ls, wv=wv, bv=bv):
                base = s * FFN_ROWS
                rows = pl.ds(base, FFN_ROWS)
                taps = _ffn_taps(sh, ext, base, ls)
                gp = bv + wv[0] * taps[0] + wv[1] * taps[1] + wv[2] * taps[2]
                o_ref[rows, ls] = (gp * _sigmoid(gp) * u_ref[rows, ls]).astype(BF16)
                return carry

            _unrolled(tm // FFN_ROWS, strip, 0)

    return pl.pallas_call(
        body, name="ffn_act", grid=(ncc, S // tm),
        in_specs=[pl.BlockSpec((tm, cw), lambda cc, i: (i, cc)),
                  pl.BlockSpec((tm, cw), lambda cc, i: (i, ncc + cc)),
                  pl.BlockSpec((FFN_HALO, cw), lambda cc, i: (jnp.maximum(i * hb - 1, 0), cc)),
                  pl.BlockSpec((1, FFN_K, cw), lambda cc, i: (l, 0, cc)),
                  pl.BlockSpec((None, 1, cw), lambda cc, i: (l, 0, cc))],
        out_specs=pl.BlockSpec((tm, cw), lambda cc, i: (i, cc)),
        out_shape=SDS((S, F), BF16),
        scratch_shapes=[pltpu.VMEM((tm + FFN_HALO, cw), F32), pltpu.VMEM((FFN_K - 1, tm, cw), F32)],
        compiler_params=_cp("parallel", "parallel"),
    )(gu, gu, gu, w_dw, b_dw[:, None, :])


def _silu_grad(gp, sg):
    return sg * (1.0 + gp * (1.0 - sg))


def _ffn_act_bwd(da, gu, w_dw, b_dw, l):
    S, F2 = gu.shape
    F = F2 // 2
    cw = _ffn_chunk(F)
    ncc = F // cw
    tm = _tm(S, 256)
    ni = S // tm
    hb = tm // FFN_HALO
    last_hb = S // FFN_HALO - 1

    def body(da_ref, dan_ref, g_ref, gh_ref, gn_ref, u_ref, un_ref, w_ref, b_ref, dg_ref, du_ref, sums_ref,
             ext, sh, dgp, dsh):
        i = pl.program_id(1)

        @pl.when(i == 0)
        def _():
            sums_ref[...] = jnp.zeros_like(sums_ref)
        ext[0:FFN_HALO, :] = jnp.where(i > 0, gh_ref[...], 0.0)
        ext[FFN_HALO:FFN_HALO + tm, :] = g_ref[...]
        ext[FFN_HALO + tm:, :] = gn_ref[...]
        _ffn_shift(sh, ext, tm + FFN_HALO)
        tail = pl.ds(tm, FFN_HALO)
        gp_t = b_ref[...]
        for k, tap in enumerate((sh[0, tail, :], sh[1, tail, :], ext[pl.ds(tm + FFN_HALO, FFN_HALO), :])):
            gp_t = gp_t + w_ref[0, k:k + 1, :] * tap
        d_t = dan_ref[...] * un_ref[...] * _silu_grad(gp_t, _sigmoid(gp_t))
        dgp[tail, :] = jnp.where(i < ni - 1, d_t, 0.0)
        for cg in range(cw // HEAD_DIM):
            ls = slice(cg * HEAD_DIM, (cg + 1) * HEAD_DIM)
            wv = [w_ref[0, k:k + 1, ls] for k in range(FFN_K)]
            bv = b_ref[:, ls]

            def strip(s, accs, ls=ls, wv=wv, bv=bv):
                base = s * FFN_ROWS
                rows = pl.ds(base, FFN_ROWS)
                taps = _ffn_taps(sh, ext, base, ls)
                gp = bv + wv[0] * taps[0] + wv[1] * taps[1] + wv[2] * taps[2]
                sg = _sigmoid(gp)
                dav = da_ref[rows, ls]
                du_ref[rows, ls] = (dav * gp * sg).astype(BF16)
                dv = dav * u_ref[rows, ls] * _silu_grad(gp, sg)
                dgp[rows, ls] = dv
                return tuple(a + dv * t for a, t in zip(accs[:FFN_K], taps)) + (accs[FFN_K] + dv,)

            accs = _unrolled(tm // FFN_ROWS, strip,
                             tuple(jnp.zeros((FFN_ROWS, HEAD_DIM), F32) for _ in range(FFN_K + 1)))
            for k, a in enumerate(accs):
                sums_ref[k:k + 1, ls] += jnp.sum(a, axis=0, keepdims=True)
        for b in range(1, FFN_K):
            dsh[b - 1] = dgp[pl.ds(b, tm), :]
        for cg in range(cw // HEAD_DIM):
            ls = slice(cg * HEAD_DIM, (cg + 1) * HEAD_DIM)
            wv = [w_ref[0, k:k + 1, ls] for k in range(FFN_K)]

            def strip2(s, carry, ls=ls, wv=wv):
                rows = pl.ds(s * FFN_ROWS, FFN_ROWS)
                dg_ref[rows, ls] = (wv[0] * dsh[1, rows, ls] + wv[1] * dsh[0, rows, ls]
                                    + wv[2] * dgp[rows, ls]).astype(BF16)
                return carry

            _unrolled(tm // FFN_ROWS, strip2, 0)

    cur = lambda off: pl.BlockSpec((tm, cw), lambda cc, i: (i, off + cc))
    nxt = lambda off: pl.BlockSpec((FFN_HALO, cw), lambda cc, i: (jnp.minimum((i + 1) * hb, last_hb), off + cc))
    return pl.pallas_call(
        body, name="ffn_act_bwd", grid=(ncc, ni),
        in_specs=[cur(0), nxt(0), cur(0),
                  pl.BlockSpec((FFN_HALO, cw), lambda cc, i: (jnp.maximum(i * hb - 1, 0), cc)), nxt(0),
                  cur(ncc), nxt(ncc),
                  pl.BlockSpec((1, FFN_K, cw), lambda cc, i: (l, 0, cc)),
                  pl.BlockSpec((None, 1, cw), lambda cc, i: (l, 0, cc))],
        out_specs=[cur(0), cur(0), pl.BlockSpec((8, cw), lambda cc, i: (0, cc))],
        out_shape=[SDS((S, F), BF16), SDS((S, F), BF16), SDS((8, F), F32)],
        scratch_shapes=[pltpu.VMEM((tm + 2 * FFN_HALO, cw), F32), pltpu.VMEM((FFN_K - 1, tm + FFN_HALO, cw), F32),
                        pltpu.VMEM((tm + FFN_HALO, cw), F32), pltpu.VMEM((FFN_K - 1, tm, cw), F32)],
        compiler_params=_cp("parallel", "arbitrary"),
    )(da, da, gu, gu, gu, gu, gu, w_dw, b_dw[:, None, :])


def _loss_grad(y, target):
    S, D = y.shape
    tm = _tm(S)

    def body(y_ref, t_ref, dy_ref, l_ref):
        @pl.when(pl.program_id(0) == 0)
        def _():
            l_ref[...] = jnp.zeros_like(l_ref)
        e = y_ref[...] - t_ref[...]
        dy_ref[...] = e * (1.0 / D)
        l_ref[...] += jnp.broadcast_to(0.5 * jnp.sum(jnp.mean(e * e, axis=-1, keepdims=True)), (1, HEAD_DIM))

    row = pl.BlockSpec((tm, D), lambda i: (i, 0))
    dy, lsum = pl.pallas_call(
        body, name="loss_grad", grid=(S // tm,), in_specs=[row, row],
        out_specs=[row, pl.BlockSpec((1, HEAD_DIM), lambda i: (0, 0))],
        out_shape=[SDS((S, D), F32), SDS((1, HEAD_DIM), F32)],
        compiler_params=_cp("arbitrary"),
    )(y, target)
    return dy, lsum[0, 0]


def _gate_bwd(dx, t, gate, name):
    S, D = dx.shape
    tm = _tm(S)

    def body(dx_ref, t_ref, g_ref, e_ref, dg_ref):
        @pl.when(pl.program_id(0) == 0)
        def _():
            dg_ref[...] = jnp.zeros_like(dg_ref)
        dxv = dx_ref[...]
        e_ref[...] = (dxv * g_ref[...]).astype(BF16)
        dg_ref[...] += jnp.sum(dxv * t_ref[...], axis=0, keepdims=True)

    row = pl.BlockSpec((tm, D), lambda i: (i, 0))
    vec = pl.BlockSpec((1, D), lambda i: (0, 0))
    return pl.pallas_call(
        body, name=name, grid=(S // tm,), in_specs=[row, row, vec], out_specs=[row, vec],
        out_shape=[SDS((S, D), BF16), SDS((1, D), F32)], compiler_params=_cp("arbitrary"),
    )(dx, t, gate)


def _mm_nt(dy, w, l, name):
    parts = dy if isinstance(dy, tuple) else (dy,)
    S, cp = parts[0].shape
    C = cp * len(parts)
    K = w.shape[1]
    tm = _tm(S, 1024 if K <= 1024 else 512)
    tc = _quarter(C) if C > 1024 else C
    nc = C // tc
    per = cp // tc

    def body(*refs):
        p_refs, (w_ref, o_ref, acc) = refs[:len(parts)], refs[len(parts):]
        kk = pl.program_id(1)

        @pl.when(kk == 0)
        def _():
            acc[...] = jnp.zeros_like(acc)
        for p, p_ref in enumerate(p_refs):
            @pl.when(jnp.logical_and(kk >= p * per, kk < (p + 1) * per))
            def _(p_ref=p_ref):
                acc[...] += lax.dot_general(p_ref[...], w_ref[0], NT_DIMS, preferred_element_type=F32)

        @pl.when(kk == nc - 1)
        def _():
            o_ref[...] = acc[...]

    part_spec = lambda p: pl.BlockSpec((tm, tc), lambda i, kk: (i, jnp.clip(kk - p * per, 0, per - 1)))
    return pl.pallas_call(
        body, name=name, grid=(S // tm, nc),
        in_specs=[part_spec(p) for p in range(len(parts))]
        + [pl.BlockSpec((1, K, tc), lambda i, kk: (l, 0, kk))],
        out_specs=pl.BlockSpec((tm, K), lambda i, kk: (i, 0)),
        out_shape=SDS((S, K), F32),
        scratch_shapes=[pltpu.VMEM((tm, K), F32)],
        compiler_params=_cp("parallel", "arbitrary"),
    )(*parts, w)


def _mm_tn(a, b, buf, l, n_layers, col_sharded, name):
    parts = b if isinstance(b, tuple) else (b,)
    S, K = a.shape
    N = parts[0].shape[1] * len(parts)
    ts = _tm(S, 1024)
    ns = S // ts
    if col_sharded:
        cols, r2 = N // 4, K // 2
        grid = (4, ns)
        spp = 4 // len(parts)

        def body(*refs):
            a_ref, b_refs, o_ref = refs[0], refs[1:1 + len(parts)], refs[-1]
            jj, s = pl.program_id(0), pl.program_id(1)

            @pl.when(s == 0)
            def _():
                o_ref[...] = jnp.zeros_like(o_ref)
            for p, b_ref in enumerate(b_refs):
                @pl.when(jj // spp == p)
                def _(b_ref=b_ref):
                    for hh in range(2):
                        o_ref[hh] += lax.dot_general(a_ref[:, hh * r2:(hh + 1) * r2], b_ref[...], TN_DIMS,
                                                     preferred_element_type=F32)

        part_spec = lambda p: pl.BlockSpec(
            (ts, cols), lambda jj, s: (jnp.where(jj // spp == p, s, 0), jnp.clip(jj - p * spp, 0, spp - 1)))
        in_specs = [pl.BlockSpec((ts, K), lambda jj, s: (s, 0))] + [part_spec(p) for p in range(len(parts))]
        out_spec = pl.BlockSpec((None, None, 2, r2, cols), lambda jj, s: (l, jj, 0, 0, 0))
        scratch = []
    else:
        cols, r2 = N, K // 8
        tn = CHUNK if N % CHUNK == 0 else N
        grid = (N // tn, ns)

        def body(*refs):
            a_ref, b_ref, o_ref, acc = refs[0], refs[1], refs[-2], refs[-1]
            s = pl.program_id(1)

            @pl.when(s == 0)
            def _():
                acc[...] = jnp.zeros_like(acc)
            acc[...] += lax.dot_general(a_ref[...], b_ref[...], TN_DIMS, preferred_element_type=F32)

            @pl.when(s == ns - 1)
            def _():
                for jj in range(4):
                    for hh in range(2):
                        o_ref[jj, hh] = acc[(2 * jj + hh) * r2:(2 * jj + hh + 1) * r2, :]

        in_specs = [pl.BlockSpec((ts, K), lambda nb, s: (s, 0)), pl.BlockSpec((ts, tn), lambda nb, s: (s, nb))]
        out_spec = pl.BlockSpec((None, 4, 2, r2, tn), lambda nb, s: (l, 0, 0, 0, nb))
        scratch = [pltpu.VMEM((K, tn), F32)]
    args = [a, *parts]
    aliases = {}
    if buf is not None:
        in_specs = in_specs + [pl.BlockSpec(memory_space=pl.ANY)]
        args.append(buf)
        aliases = {len(args) - 1: 0}
    return pl.pallas_call(
        body, name=name, grid=grid, in_specs=in_specs, out_specs=out_spec,
        out_shape=SDS((n_layers, 4, 2, r2, cols), F32), scratch_shapes=scratch,
        input_output_aliases=aliases, compiler_params=_cp("parallel", "arbitrary"),
    )(*args)


def _norm_bwd(dh, x, dres, g, sc, name):
    S, D = x.shape
    tm = _tm(S)

    def body(dh_ref, x_ref, dr_ref, g_ref, sc_ref, dx_ref, sums_ref):
        @pl.when(pl.program_id(0) == 0)
        def _():
            sums_ref[...] = jnp.zeros_like(sums_ref)
        xv, dhv = x_ref[...], dh_ref[...]
        r = lax.rsqrt(jnp.mean(xv * xv, axis=-1, keepdims=True) + EPS)
        xn = xv * r
        gv, sv = g_ref[...], 1.0 + sc_ref[...]
        dxn = dhv * gv * sv
        dx_ref[...] = dr_ref[...] + r * (dxn - xn * jnp.mean(dxn * xn, axis=-1, keepdims=True))
        dhxn = dhv * xn
        sums_ref[0:1, :] += jnp.sum(dhv, axis=0, keepdims=True)
        sums_ref[1:2, :] += jnp.sum(dhxn * gv, axis=0, keepdims=True)
        sums_ref[2:3, :] += jnp.sum(dhxn * sv, axis=0, keepdims=True)

    row = pl.BlockSpec((tm, D), lambda i: (i, 0))
    vec = pl.BlockSpec((1, D), lambda i: (0, 0))
    return pl.pallas_call(
        body, name=name, grid=(S // tm,), in_specs=[row, row, row, vec, vec],
        out_specs=[row, pl.BlockSpec((8, D), lambda i: (0, 0))],
        out_shape=[SDS((S, D), F32), SDS((8, D), F32)], compiler_params=_cp("arbitrary"),
    )(dh, x, dres, g, sc)


def _merge_bwd(dm, z, y_a, y_b, D):
    S = z.shape[0]
    tm = _tm(S)
    ncc = D // CHUNK
    ga0 = (3 * ATTN_W + 2 * D) // CHUNK

    def body(dm_ref, ga_ref, gb_ref, ya_ref, yb_ref, dya_ref, dyb_ref, dga_ref, dgb_ref):
        dmv = dm_ref[...]
        sa, sb = _sigmoid(ga_ref[...]), _sigmoid(gb_ref[...])
        dya_ref[...] = (dmv * sa).astype(BF16)
        dyb_ref[...] = (dmv * sb).astype(BF16)
        dga_ref[...] = (dmv * ya_ref[...] * sa * (1.0 - sa)).astype(BF16)
        dgb_ref[...] = (dmv * yb_ref[...] * sb * (1.0 - sb)).astype(BF16)

    ysp = pl.BlockSpec((tm, CHUNK), lambda i, cc: (i, cc))
    return pl.pallas_call(
        body, name="merge_bwd", grid=(S // tm, ncc),
        in_specs=[ysp, pl.BlockSpec((tm, CHUNK), lambda i, cc: (i, ga0 + cc)),
                  pl.BlockSpec((tm, CHUNK), lambda i, cc: (i, ga0 + ncc + cc)), ysp, ysp],
        out_specs=[ysp] * 4, out_shape=[SDS((S, D), BF16)] * 4,
        compiler_params=_cp("parallel", "parallel"),
    )(dm, z, z, y_a, y_b)


def _ln_silu_bwd(du3, u1, g, b):
    S, D = u1.shape
    tm = _tm(S)

    def body(d_ref, u_ref, g_ref, b_ref, du_ref, sums_ref):
        @pl.when(pl.program_id(0) == 0)
        def _():
            sums_ref[...] = jnp.zeros_like(sums_ref)
        xh, rstd = _ln_stats(u_ref[...])
        gv = g_ref[...]
        yv = xh * gv + b_ref[...]
        sg = _sigmoid(yv)
        dy = d_ref[...] * (sg * (1.0 + yv * (1.0 - sg)))
        dxh = dy * gv
        du_ref[...] = rstd * (dxh - jnp.mean(dxh, axis=-1, keepdims=True)
                              - xh * jnp.mean(dxh * xh, axis=-1, keepdims=True))
        sums_ref[0:1, :] += jnp.sum(dy * xh, axis=0, keepdims=True)
        sums_ref[1:2, :] += jnp.sum(dy, axis=0, keepdims=True)

    row = pl.BlockSpec((tm, D), lambda i: (i, 0))
    vec = pl.BlockSpec((1, D), lambda i: (0, 0))
    return pl.pallas_call(
        body, name="ln_silu_bwd", grid=(S // tm,), in_specs=[row, row, vec, vec],
        out_specs=[row, pl.BlockSpec((8, D), lambda i: (0, 0))],
        out_shape=[SDS((S, D), F32), SDS((8, D), F32)], compiler_params=_cp("arbitrary"),
    )(du3, u1, g, b)


def _glu_conv_bwd(du1, z, w_dw, l, D):
    S = z.shape[0]
    tm = _tm(S)
    ni = S // tm
    cc0 = 3 * ATTN_W // CHUNK
    ncc = D // CHUNK
    hb = tm // CONV_HALO
    last_hb = S // CONV_HALO - 1
    lead = CONV_HALO - (CONV_K - 1)
    group = 8

    def body(d_ref, dn_ref, cv_ref, cg_ref, hv_ref, hg_ref, w_ref, dcv_ref, dcg_ref, sums_ref,
             extd, extu, shd, shu):
        i = pl.program_id(1)

        @pl.when(i == 0)
        def _():
            sums_ref[...] = jnp.zeros_like(sums_ref)
        extd[0:tm, :] = d_ref[...]
        extd[tm:, :] = jnp.where(i < ni - 1, dn_ref[...], 0.0)
        extu[0:CONV_HALO, :] = jnp.where(i > 0, _glu(hv_ref[...], hg_ref[...]), 0.0)
        extu[CONV_HALO:, :] = _glu(cv_ref[...], cg_ref[...])
        _fill_shifted(shd, extd, tm + CONV_HALO - SHIFTS)
        _fill_shifted(shu, extu, tm + CONV_HALO - SHIFTS)

        def strip(s, carry):
            base = s * STRIP
            acc = jnp.zeros((STRIP, CHUNK), F32)
            for k in range(CONV_K):
                acc = acc + w_ref[0, k:k + 1, :] * _tap(shd, base, CONV_K - 1 - k)
            cv, cg = cv_ref[pl.ds(base, STRIP), :], cg_ref[pl.ds(base, STRIP), :]
            sg = _sigmoid(cg)
            dcv_ref[pl.ds(base, STRIP), :] = (acc * sg).astype(BF16)
            dcg_ref[pl.ds(base, STRIP), :] = (acc * cv * sg * (1.0 - sg)).astype(BF16)
            return carry

        _unrolled(tm // STRIP, strip, 0)

        for k0 in range(0, CONV_K + 1, group):
            ks = list(range(k0, min(k0 + group, CONV_K + 1)))

            def rows(s, accs, ks=ks):
                base = s * SHIFTS
                dv = extd[pl.ds(base, SHIFTS), :]
                return tuple(a + (dv if k == CONV_K else dv * _tap(shu, base, lead + k, SHIFTS))
                             for a, k in zip(accs, ks))

            accs = _unrolled(tm // SHIFTS, rows,
                             tuple(jnp.zeros((SHIFTS, CHUNK), F32) for _ in ks))
            for k, a in zip(ks, accs):
                sums_ref[k:k + 1, :] += jnp.sum(a, axis=0, keepdims=True)

    cur = lambda off: pl.BlockSpec((tm, CHUNK), lambda cc, i: (i, cc0 + off + cc))
    halo = lambda off: pl.BlockSpec((CONV_HALO, CHUNK), lambda cc, i: (jnp.maximum(i * hb - 1, 0), cc0 + off + cc))
    osp = pl.BlockSpec((tm, CHUNK), lambda cc, i: (i, cc))
    big = pltpu.VMEM((tm + CONV_HALO, CHUNK), F32)
    shifted = pltpu.VMEM((SHIFTS, tm + CONV_HALO, CHUNK), F32)
    return pl.pallas_call(
        body, name="glu_conv_bwd", grid=(ncc, ni),
        in_specs=[osp,
                  pl.BlockSpec((CONV_HALO, CHUNK), lambda cc, i: (jnp.minimum((i + 1) * hb, last_hb), cc)),
                  cur(0), cur(ncc), halo(0), halo(ncc),
                  pl.BlockSpec((1, CONV_K, CHUNK), lambda cc, i: (l, 0, cc))],
        out_specs=[osp, osp, pl.BlockSpec((CONV_HALO, CHUNK), lambda cc, i: (0, cc))],
        out_shape=[SDS((S, D), BF16), SDS((S, D), BF16), SDS((CONV_HALO, D), F32)],
        scratch_shapes=[big, big, shifted, shifted],
        compiler_params=_cp("parallel", "arbitrary"),
    )(du1, du1, z, z, z, z, w_dw)


def _attn_bwd_prep(dattn, outs, lses):
    S = dattn.shape[0]
    tm = _tm(S)
    ng = len(DILATIONS)

    def body(*refs):
        da_ref, o_refs, l_refs = refs[0], refs[1:1 + ng], refs[1 + ng:1 + 2 * ng]
        d_refs, t_refs = refs[1 + 2 * ng:1 + 3 * ng], refs[1 + 3 * ng:1 + 4 * ng]
        scr = refs[1 + 4 * ng:]
        for h in range(HEADS_PER_GROUP):
            cols = slice(h * HEAD_DIM, (h + 1) * HEAD_DIM)
            ov = [_from_dilated(scr[g], o_refs[g], DILATIONS[g], cols) for g in range(ng)]
            lv = [_from_dilated(scr[ng + g], l_refs[g], DILATIONS[g], cols) for g in range(ng)]
            w = _group_weights(lv)
            dav = da_ref[:, cols]
            rs = jnp.sum(dav * (w[0] * ov[0] + w[1] * ov[1] + w[2] * ov[2]), axis=-1, keepdims=True)
            rs = jnp.broadcast_to(rs, (tm, HEAD_DIM))
            for g in range(ng):
                _to_dilated(scr[g], w[g] * dav, d_refs[g], DILATIONS[g], cols)
                _to_dilated(scr[ng + g], -w[g] * rs, t_refs[g], DILATIONS[g], cols)

    dil = [_dil_spec(d, tm) for d in DILATIONS]
    out = pl.pallas_call(
        body, name="attn_bwd_prep", grid=(S // tm,),
        in_specs=[pl.BlockSpec((tm, GROUP_W), lambda i: (i, 0))] + dil + dil, out_specs=dil + dil,
        out_shape=[SDS((d, S // d, GROUP_W), BF16) for d in DILATIONS]
        + [SDS((d, S // d, GROUP_W), F32) for d in DILATIONS],
        scratch_shapes=[pltpu.VMEM((tm, HEAD_DIM), F32)] * (2 * ng),
        compiler_params=_cp("parallel"),
    )(dattn, *outs, *lses)
    return out[:ng], out[ng:]


def _attn_bwd(q, k, v, do, lse, dterm, gi):
    d, Sd, _ = q.shape
    nb = Sd // BLOCK
    scale = HEAD_DIM ** -0.5

    def body(q_ref, qx_ref, kc_ref, kp_ref, vc_ref, vp_ref, do_ref, dox_ref, l_ref, lx_ref, t_ref, tx_ref,
             dq_ref, dk_ref, dv_ref, kk, vv, qq, dd, ds2, p2):
        n = pl.program_id(1)
        kk[0:BLOCK, :], kk[BLOCK:, :] = kp_ref[...], kc_ref[...]
        vv[0:BLOCK, :], vv[BLOCK:, :] = vp_ref[...], vc_ref[...]
        qq[0:BLOCK, :], qq[BLOCK:, :] = q_ref[...], qx_ref[...]
        dd[0:BLOCK, :], dd[BLOCK:, :] = do_ref[...], dox_ref[...]
        mask_ab = _band_mask(n > 0)
        qi = lax.broadcasted_iota(jnp.int32, (BLOCK, BLOCK), 0)
        kj = lax.broadcasted_iota(jnp.int32, (BLOCK, BLOCK), 1)
        mask_c = jnp.logical_and(kj >= qi, n < nb - 1)

        def pair(qh, kh, vh, dov, lv, tv, mask):
            s = lax.dot_general(qh, kh, NT_DIMS, preferred_element_type=F32) * scale
            p = jnp.exp(jnp.where(mask, s - lv, NEG))
            dp = lax.dot_general(dov, vh, NT_DIMS, preferred_element_type=F32)
            return p, p * (dp + tv) * scale

        for h in range(HEADS_PER_GROUP):
            sl = slice(h * HEAD_DIM, (h + 1) * HEAD_DIM)
            lv = jnp.concatenate([l_ref[:, sl], l_ref[:, sl]], axis=1)
            tv = jnp.concatenate([t_ref[:, sl], t_ref[:, sl]], axis=1)
            p_ab, ds_ab = pair(q_ref[:, sl], kk[:, sl], vv[:, sl], do_ref[:, sl], lv, tv, mask_ab)
            p_c, ds_c = pair(qx_ref[:, sl], kc_ref[:, sl], vc_ref[:, sl], dox_ref[:, sl],
                             lx_ref[:, sl], tx_ref[:, sl], mask_c)
            dq_ref[:, sl] = jnp.dot(ds_ab.astype(BF16), kk[:, sl], preferred_element_type=F32)
            ds2[0:BLOCK, :], ds2[BLOCK:, :] = ds_ab[:, BLOCK:].astype(BF16), ds_c.astype(BF16)
            p2[0:BLOCK, :], p2[BLOCK:, :] = p_ab[:, BLOCK:].astype(BF16), p_c.astype(BF16)
            dk_ref[:, sl] = lax.dot_general(ds2[...], qq[:, sl], TN_DIMS, preferred_element_type=F32)
            dv_ref[:, sl] = lax.dot_general(p2[...], dd[:, sl], TN_DIMS, preferred_element_type=F32).astype(BF16)

    cur, prev, nxt = _attn_blk(nb, 0), _attn_blk(nb, -1), _attn_blk(nb, 1)
    wide = pltpu.VMEM((2 * BLOCK, GROUP_W), BF16)
    tall = pltpu.VMEM((2 * BLOCK, HEAD_DIM), BF16)
    return pl.pallas_call(
        body, name=f"attn_bwd_g{gi}", grid=(d, nb),
        in_specs=[cur, nxt, cur, prev, cur, prev, cur, nxt, cur, nxt, cur, nxt],
        out_specs=[cur] * 3,
        out_shape=[SDS((d, Sd, GROUP_W), F32)] * 2 + [SDS((d, Sd, GROUP_W), BF16)],
        scratch_shapes=[wide, wide, wide, wide, tall, tall],
        compiler_params=_cp("parallel", "parallel"),
    )(q, q, k, k, v, v, do, do, lse, lse, dterm, dterm)


def _qk_prep_bwd(dqs, dks, dvs, z, g_q, g_k, tabs):
    S = z.shape[0]
    tm = _tm(S)
    n_heads = ATTN_W // HEAD_DIM
    ng = len(DILATIONS)

    def body(*refs):
        dq_refs, dk_refs, dv_refs = refs[:ng], refs[ng:2 * ng], refs[2 * ng:3 * ng]
        q_ref, k_ref, gq_ref, gk_ref, c_ref, sa_ref, sb_ref = refs[3 * ng:3 * ng + 7]
        dqo_ref, dko_ref, dvo_ref, sums_ref, scr = refs[3 * ng + 7:]

        @pl.when(pl.program_id(0) == 0)
        def _():
            sums_ref[...] = jnp.zeros_like(sums_ref)
        cos_t, sa_t, sb_t = c_ref[...], sa_ref[...], sb_ref[...]
        for row, (drefs, src, gref, dst) in enumerate(((dq_refs, q_ref, gq_ref, dqo_ref),
                                                       (dk_refs, k_ref, gk_ref, dko_ref))):
            gv = gref[...]
            gsum = jnp.zeros((1, HEAD_DIM), F32)
            for h in range(n_heads):
                gi = h // HEADS_PER_GROUP
                sl = slice(h * HEAD_DIM, (h + 1) * HEAD_DIM)
                cols = slice((h % HEADS_PER_GROUP) * HEAD_DIM, (h % HEADS_PER_GROUP + 1) * HEAD_DIM)
                dyn = _unrope(_from_dilated(scr, drefs[gi], DILATIONS[gi], cols), cos_t, sa_t, sb_t)
                t = src[:, sl]
                r = lax.rsqrt(jnp.mean(t * t, axis=-1, keepdims=True) + EPS)
                xh = t * r
                gsum = gsum + jnp.sum(dyn * xh, axis=0, keepdims=True)
                gy = dyn * gv
                dst[:, sl] = (r * (gy - xh * jnp.mean(gy * xh, axis=-1, keepdims=True))).astype(BF16)
            sums_ref[row:row + 1, :] += gsum
        for h in range(n_heads):
            gi = h // HEADS_PER_GROUP
            cols = slice((h % HEADS_PER_GROUP) * HEAD_DIM, (h % HEADS_PER_GROUP + 1) * HEAD_DIM)
            dvo_ref[:, h * HEAD_DIM:(h + 1) * HEAD_DIM] = _from_dilated(
                scr, dv_refs[gi], DILATIONS[gi], cols).astype(BF16)

    dil = [_dil_spec(d, tm) for d in DILATIONS]
    tab = pl.BlockSpec((tm, HEAD_DIM), lambda i: (i, 0))
    gsp = pl.BlockSpec((1, HEAD_DIM), lambda i: (0, 0))
    wide = pl.BlockSpec((tm, ATTN_W), lambda i: (i, 0))
    return pl.pallas_call(
        body, name="qk_prep_bwd", grid=(S // tm,),
        in_specs=dil * 3 + [wide, pl.BlockSpec((tm, ATTN_W), lambda i: (i, 1)), gsp, gsp, tab, tab, tab],
        out_specs=[wide, wide, wide, pl.BlockSpec((8, HEAD_DIM), lambda i: (0, 0))],
        out_shape=[SDS((S, ATTN_W), BF16)] * 3 + [SDS((8, HEAD_DIM), F32)],
        scratch_shapes=[pltpu.VMEM((tm, HEAD_DIM), F32)],
        compiler_params=_cp("arbitrary"),
    )(*dqs, *dks, *dvs, z, z, g_q, g_k, *tabs)


def _adamw(w, g, m, v, name):
    R, C = w.shape
    tr = _row_tile(R, C * 2)
    c1 = 1.0 - ADAM_B1 ** ADAM_STEP
    c2 = 1.0 - ADAM_B2 ** ADAM_STEP

    def body(w_ref, g_ref, m_ref, v_ref, d_ref, mo_ref, vo_ref):
        gv = g_ref[...]
        mn = ADAM_B1 * m_ref[...] + (1.0 - ADAM_B1) * gv
        vn = ADAM_B2 * v_ref[...] + (1.0 - ADAM_B2) * (gv * gv)
        mo_ref[...] = mn
        vo_ref[...] = vn
        d_ref[...] = -ADAM_LR * ((mn / c1) / (jnp.sqrt(vn / c2) + ADAM_EPS) + ADAM_WD * w_ref[...])

    sp = pl.BlockSpec((tr, C), lambda i: (i, 0))
    return pl.pallas_call(
        body, name=name, grid=(R // tr,), in_specs=[sp] * 4, out_specs=[sp] * 3,
        out_shape=[SDS((R, C), F32)] * 3, compiler_params=_cp("parallel"),
    )(w, g, m, v)


def _sum_parts(parts):
    n, R, C = parts.shape

    def body(p_ref, o_ref):
        acc = p_ref[0]
        for k in range(1, n):
            acc = acc + p_ref[k]
        o_ref[...] = acc

    return pl.pallas_call(
        body, name="sum_parts", out_shape=SDS((R, C), F32),
        compiler_params=pltpu.CompilerParams(vmem_limit_bytes=VMEM_LIMIT_BYTES),
    )(parts)


def _wada_grad(cact_t, dmod):
    D, n = cact_t.shape
    L, _, cols = dmod.shape

    def body(c_ref, d_ref, o_ref):
        acc = c_ref[:, 0:1] * d_ref[0, 0:1, :]
        for k in range(1, n):
            acc = acc + c_ref[:, k:k + 1] * d_ref[0, k:k + 1, :]
        o_ref[0] = acc

    return pl.pallas_call(
        body, name="wada_grad", grid=(L,),
        in_specs=[pl.BlockSpec((D, n), lambda l: (0, 0)), pl.BlockSpec((1, n, cols), lambda l: (l, 0, 0))],
        out_specs=pl.BlockSpec((1, D, cols), lambda l: (l, 0, 0)),
        out_shape=SDS((L, D, cols), F32), compiler_params=_cp("parallel"),
    )(cact_t, dmod)


def _add_halves(dw, land, c_idx, name):
    L, _, _, r2, cols = dw.shape
    tr = _row_tile(r2, cols, step=16)
    dw4 = dw.reshape(L * 4, 2, r2, cols)
    land3 = land.reshape(L * 4, r2, cols)

    def body(pf, a_ref, b_ref, o_ref):
        o_ref[...] = (a_ref[...] + b_ref[...]).astype(BF16)

    out = pl.pallas_call(
        body, name=name,
        grid_spec=pltpu.PrefetchScalarGridSpec(
            num_scalar_prefetch=1, grid=(L * 4, r2 // tr),
            in_specs=[pl.BlockSpec((None, None, tr, cols), lambda a, i, pf: (a, pf[0], i, 0)),
                      pl.BlockSpec((None, tr, cols), lambda a, i, pf: (a, i, 0))],
            out_specs=pl.BlockSpec((None, tr, cols), lambda a, i, pf: (a, i, 0))),
        out_shape=SDS((L * 4, r2, cols), BF16), compiler_params=_cp("parallel", "parallel"),
    )(c_idx, dw4, land3)
    return out.reshape(L, 4, r2, cols)


def _add_chips(dw, land1, land2, cj_idx, name):
    L, _, _, r2, cols = dw.shape
    tr = _row_tile(r2, cols, step=16)

    def body(pf, a_ref, b_ref, c0, c1, c2, o_ref):
        own = a_ref[...] + b_ref[...]
        o_ref[...] = ((own + c0[...].astype(F32)) + c1[...].astype(F32)) + c2[...].astype(F32)

    lsp = lambda p: pl.BlockSpec((None, None, tr, cols), lambda l, i, pf: (p, l, i, 0))
    return pl.pallas_call(
        body, name=name,
        grid_spec=pltpu.PrefetchScalarGridSpec(
            num_scalar_prefetch=1, grid=(L, r2 // tr),
            in_specs=[pl.BlockSpec((None, None, None, tr, cols), lambda l, i, pf: (l, pf[1], pf[0], i, 0)),
                      pl.BlockSpec((None, None, tr, cols), lambda l, i, pf: (l, pf[1], i, 0)),
                      lsp(0), lsp(1), lsp(2)],
            out_specs=pl.BlockSpec((None, None, tr, cols), lambda l, i, pf: (l, pf[0], i, 0))),
        out_shape=SDS((L, 2, r2, cols), F32), compiler_params=_cp("parallel", "parallel"),
    )(cj_idx, dw, land1, land2, land2, land2)


HBM_SPEC = pl.BlockSpec(memory_space=pltpu.HBM)


def _place():
    x, y, c = lax.axis_index("x"), lax.axis_index("y"), lax.axis_index("c")
    chips = [(x, 1 - y), (1 - x, y), (1 - x, 1 - y)]
    return x, y, c, chips


def _remote(src, dst, send_sem, recv_sem, device):
    return pltpu.make_async_remote_copy(src_ref=src, dst_ref=dst, send_sem=send_sem, recv_sem=recv_sem,
                                        device_id=device, device_id_type=MESH)


def _comm_call(body, name, ins, out_shapes, n_sems):
    return pl.pallas_call(
        body, name=name, in_specs=[HBM_SPEC] * len(ins), out_specs=[HBM_SPEC] * len(out_shapes),
        out_shape=out_shapes,
        scratch_shapes=[pltpu.SemaphoreType.DMA((n,)) for n in n_sems],
    )(*ins)


def _all_gather_weights(shards, small):
    nw = len(shards)
    L = shards[0].shape[0]
    Lh = L // 2

    def body(*refs):
        sh, sm = refs[:nw], refs[nw]
        full, smo = refs[nw + 1:2 * nw + 1], refs[2 * nw + 1]
        ici_s, ici_r, d2d_s, d2d_r, loc = refs[2 * nw + 2:]
        x, y, c, chips = _place()
        j = 2 * x + y
        jps = [2 * px + py for px, py in chips]
        me, sib = (x, y, c), (x, y, 1 - c)
        mine, theirs = pl.ds(c * Lh, Lh), pl.ds((1 - c) * Lh, Lh)

        local = [pltpu.make_async_copy(sm, smo.at[j], loc.at[0])]
        for cp in local:
            cp.start()
        sends = []
        for w in range(nw):
            for p in range(3):
                sends.append(_remote(sh[w].at[mine], full[w].at[j, mine], ici_s.at[3 * w + p], ici_r.at[3 * w + p],
                                     (*chips[p], c)))
        for p in range(3):
            sends.append(_remote(sm, smo.at[j], ici_s.at[3 * nw + p], ici_r.at[3 * nw + p], (*chips[p], c)))
        for cp in sends:
            cp.start()
        passed = []
        for w in range(nw):
            for p in range(3):
                slot = full[w].at[jps[p], mine]
                _remote(slot, slot, ici_s.at[3 * w + p], ici_r.at[3 * w + p], me).wait_recv()
                cp = _remote(slot, slot, d2d_s.at[3 * w + p], d2d_r.at[3 * w + p], sib)
                cp.start()
                passed.append(cp)
        for p in range(3):
            slot = smo.at[jps[p]]
            _remote(slot, slot, ici_s.at[3 * nw + p], ici_r.at[3 * nw + p], me).wait_recv()
        for w in range(nw):
            for p in range(3):
                slot = full[w].at[jps[p], theirs]
                _remote(slot, slot, d2d_s.at[3 * w + p], d2d_r.at[3 * w + p], me).wait_recv()
        for cp in sends + passed:
            cp.wait_send()
        for cp in local:
            cp.wait()

    outs = [SDS((4,) + s.shape, s.dtype) for s in shards] + [SDS((4,) + small.shape, small.dtype)]
    res = _comm_call(body, "all_gather_weights", list(shards) + [small], outs,
                     [3 * nw + 3, 3 * nw + 3, 3 * nw, 3 * nw, 1])
    return res[:nw], res[nw]


def _all_gather_small(part):
    def body(p_ref, o_ref, send_s, recv_s, loc):
        x, y, c, _ = _place()
        me_id = 4 * x + 2 * y + c
        own = pltpu.make_async_copy(p_ref, o_ref.at[me_id], loc.at[0])
        own.start()
        sends = []
        for k in range(1, 8):
            peer = (x ^ (k >> 2), y ^ ((k >> 1) & 1), c ^ (k & 1))
            sends.append(_remote(p_ref, o_ref.at[me_id], send_s.at[k - 1], recv_s.at[k - 1], peer))
        for cp in sends:
            cp.start()
        for k in range(1, 8):
            peer_id = 4 * (x ^ (k >> 2)) + 2 * (y ^ ((k >> 1) & 1)) + (c ^ (k & 1))
            slot = o_ref.at[peer_id]
            _remote(slot, slot, send_s.at[k - 1], recv_s.at[k - 1], (x, y, c)).wait_recv()
        for cp in sends:
            cp.wait_send()
        own.wait()

    return _comm_call(body, "all_gather_small", [part], [SDS((8,) + part.shape, part.dtype)], [7, 7, 1])[0]


def _rs_to_sibling(dws):
    nw = len(dws)

    def body(*refs):
        src, land = refs[:nw], refs[nw:2 * nw]
        send_s, recv_s = refs[2 * nw:]
        x, y, c, _ = _place()
        cps = [_remote(src[w].at[:, :, 1 - c], land[w], send_s.at[w], recv_s.at[w], (x, y, 1 - c))
               for w in range(nw)]
        for cp in cps:
            cp.start()
        for cp in cps:
            cp.wait_recv()
        for cp in cps:
            cp.wait_send()

    outs = [SDS((a.shape[0], 4, a.shape[3], a.shape[4]), a.dtype) for a in dws]
    return _comm_call(body, "rs_to_sibling", list(dws), outs, [nw, nw])


def _rs_to_chips(s1s):
    nw = len(s1s)

    def body(*refs):
        src, land = refs[:nw], refs[nw:2 * nw]
        send_s, recv_s = refs[2 * nw:]
        x, y, c, chips = _place()
        cps = []
        for w in range(nw):
            for p in range(3):
                jp = 2 * chips[p][0] + chips[p][1]
                cps.append(_remote(src[w].at[:, jp], land[w].at[p], send_s.at[3 * w + p], recv_s.at[3 * w + p],
                                   (*chips[p], c)))
        for cp in cps:
            cp.start()
        for cp in cps:
            cp.wait_recv()
        for cp in cps:
            cp.wait_send()

    outs = [SDS((3, a.shape[0], a.shape[2], a.shape[3]), a.dtype) for a in s1s]
    return _comm_call(body, "rs_to_chips", list(s1s), outs, [3 * nw, 3 * nw])


def _rs_join_halves(fulls):
    nw = len(fulls)

    def body(*refs):
        full = refs[nw:2 * nw]
        send_s, recv_s = refs[2 * nw:]
        x, y, c, _ = _place()
        cps = [_remote(full[w].at[:, c], full[w].at[:, c], send_s.at[w], recv_s.at[w], (x, y, 1 - c))
               for w in range(nw)]
        for cp in cps:
            cp.start()
        for w in range(nw):
            slot = full[w].at[:, 1 - c]
            _remote(slot, slot, send_s.at[w], recv_s.at[w], (x, y, c)).wait_recv()
        for cp in cps:
            cp.wait_send()

    return pl.pallas_call(
        body, name="rs_join_halves", in_specs=[HBM_SPEC] * nw, out_specs=[HBM_SPEC] * nw,
        out_shape=[SDS(a.shape, a.dtype) for a in fulls],
        scratch_shapes=[pltpu.SemaphoreType.DMA((nw,)), pltpu.SemaphoreType.DMA((nw,))],
        input_output_aliases={w: w for w in range(nw)},
    )(*fulls)


BIG = ("w_in", "w_attn_proj", "w_conv_out", "w_o", "w_ffn_in", "w_ffn_down")
COL_SHARDED = {"w_in": True, "w_attn_proj": True, "w_conv_out": False, "w_o": False,
               "w_ffn_in": True, "w_ffn_down": False}
SMALL = ("b_ada", "g_norm1", "g_q", "g_k", "w_conv_dw", "b_conv_dw", "g_conv_ln", "b_conv_ln",
         "g_norm2", "w_ffn_dw", "b_ffn_dw")
WEIGHTS = ("w_ada", "b_ada", "g_norm1", "w_in", "g_q", "g_k", "w_attn_proj", "w_conv_dw", "b_conv_dw",
           "g_conv_ln", "b_conv_ln", "w_conv_out", "w_o", "g_norm2", "w_ffn_in", "w_ffn_dw", "b_ffn_dw",
           "w_ffn_down")


def _pack_rows(arrs):
    flat = jnp.concatenate([a.reshape(-1) for a in arrs])
    pad = (-flat.shape[0]) % (8 * 128)
    return jnp.pad(flat, (0, pad)).reshape(-1, 128)


def _unpack_rows(packed, shapes):
    flat = packed.reshape(-1)
    out, off = [], 0
    for shp in shapes:
        n = 1
        for s in shp:
            n *= s
        out.append(flat[off:off + n].reshape(shp))
        off += n
    return out


def _rope_tables(positions):
    half = ROT_DIM // 2
    inv_freq = ROPE_THETA ** (-jnp.arange(0, ROT_DIM, 2, dtype=F32) / ROT_DIM)
    ang = positions.astype(F32)[..., None] * inv_freq
    cos, sin = jnp.cos(ang), jnp.sin(ang)
    S = cos.shape[0]
    cos_t = jnp.concatenate([cos, cos, jnp.ones((S, HEAD_DIM - ROT_DIM), F32)], axis=1)
    sa_t = jnp.concatenate([-sin, jnp.zeros((S, HEAD_DIM - half), F32)], axis=1)
    sb_t = jnp.concatenate([jnp.zeros((S, half), F32), sin, jnp.zeros((S, HEAD_DIM - ROT_DIM), F32)], axis=1)
    return cos_t, sa_t, sb_t


def _local_step(x, c, positions, loss_target, wf, small_w):
    S, D = x.shape
    L = wf["w_in"].shape[0]
    tabs = _rope_tables(positions)
    mod, cact = _mod_fwd(c, wf["w_ada"], small_w["b_ada"])
    vec = lambda a: a.reshape(1, -1)
    mods = [[vec(mod[l, k * D:(k + 1) * D]) for k in range(6)] for l in range(L)]

    saved = []
    xs = x
    for l in range(L):
        sh1, sc1, gt1, sh2, sc2, gt2 = mods[l]
        g1, g2 = vec(small_w["g_norm1"][l]), vec(small_w["g_norm2"][l])
        gq, gk = vec(small_w["g_q"][l]), vec(small_w["g_k"][l])
        z, h = _normmod_mm(xs, g1, sc1, sh1, wf["w_in"], l, "in_proj")
        qs, ks, vs = _qk_prep(z, gq, gk, tabs)
        outs, lses = [], []
        for gi in range(len(DILATIONS)):
            o_g, lse_g = _attn_fwd(qs[gi], ks[gi], vs[gi], gi)
            outs.append(o_g)
            lses.append(lse_g)
        attn = _attn_combine(outs, lses)
        y_a = _mm_nn(attn, wf["w_attn_proj"], l, "attn_proj")
        u1 = _glu_conv(z, small_w["w_conv_dw"], small_w["b_conv_dw"], l, D)
        u3 = _ln_silu(u1, vec(small_w["g_conv_ln"][l]), vec(small_w["b_conv_ln"][l]))
        y_b = _mm_nn(u3, wf["w_conv_out"], l, "conv_out")
        merged = _merge(z, y_a, y_b, D)
        t_o, x2 = _mm_nn(merged, wf["w_o"], l, "out_proj", res=xs, gate=gt1)
        gu, h2 = _normmod_mm(x2, g2, sc2, sh2, wf["w_ffn_in"], l, "ffn_in")
        a = _ffn_act(gu, small_w["w_ffn_dw"], small_w["b_ffn_dw"], l)
        t_f, x3 = _mm_nn(a, wf["w_ffn_down"], l, "ffn_down", res=x2, gate=gt2)
        saved.append(dict(x=xs, z=z, h=h, qs=qs, ks=ks, vs=vs, outs=outs, lses=lses, attn=attn, y_a=y_a, u1=u1, u3=u3,
                          y_b=y_b, merged=merged, t_o=t_o, x2=x2, gu=gu, h2=h2, a=a, t_f=t_f))
        xs = x3

    dx, loss = _loss_grad(xs, loss_target)

    bufs = {n: None for n in BIG}
    small_g = {n: [None] * L for n in SMALL}
    dmods = [None] * L

    def wgrad(name, a_mat, b_mat, l):
        bufs[name] = _mm_tn(a_mat, b_mat, bufs[name], l, L, COL_SHARDED[name], "grad_" + name)

    for l in reversed(range(L)):
        sv = saved[l]
        sh1, sc1, gt1, sh2, sc2, gt2 = mods[l]
        g1, g2 = vec(small_w["g_norm1"][l]), vec(small_w["g_norm2"][l])
        gq, gk = vec(small_w["g_q"][l]), vec(small_w["g_k"][l])
        e2, dgt2 = _gate_bwd(dx, sv["t_f"], gt2, "gate_bwd_ffn")
        wgrad("w_ffn_down", sv["a"], e2, l)
        da = _mm_nt(e2, wf["w_ffn_down"], l, "ffn_down_bwd")
        dg, du, ffn_sums = _ffn_act_bwd(da, sv["gu"], small_w["w_ffn_dw"], small_w["b_ffn_dw"], l)
        dgu = (dg, du)
        wgrad("w_ffn_in", sv["h2"], dgu, l)
        dh2 = _mm_nt(dgu, wf["w_ffn_in"], l, "ffn_in_bwd")
        dx2, n2_sums = _norm_bwd(dh2, sv["x2"], dx, g2, sc2, "norm2_bwd")
        e1, dgt1 = _gate_bwd(dx2, sv["t_o"], gt1, "gate_bwd_mix")
        wgrad("w_o", sv["merged"], e1, l)
        dmerged = _mm_nt(e1, wf["w_o"], l, "out_proj_bwd")
        dya, dyb, dga, dgb = _merge_bwd(dmerged, sv["z"], sv["y_a"], sv["y_b"], D)
        wgrad("w_attn_proj", sv["attn"], dya, l)
        wgrad("w_conv_out", sv["u3"], dyb, l)
        dattn = _mm_nt(dya, wf["w_attn_proj"], l, "attn_proj_bwd")
        du3 = _mm_nt(dyb, wf["w_conv_out"], l, "conv_out_bwd")
        du1, ln_sums = _ln_silu_bwd(du3, sv["u1"], vec(small_w["g_conv_ln"][l]), vec(small_w["b_conv_ln"][l]))
        dcv, dcg, conv_sums = _glu_conv_bwd(du1, sv["z"], small_w["w_conv_dw"], l, D)
        dos, dterms = _attn_bwd_prep(dattn, sv["outs"], sv["lses"])
        dqs, dks, dvs = [], [], []
        for gi in range(len(DILATIONS)):
            dq_g, dk_g, dv_g = _attn_bwd(sv["qs"][gi], sv["ks"][gi], sv["vs"][gi], dos[gi], sv["lses"][gi],
                                         dterms[gi], gi)
            dqs.append(dq_g)
            dks.append(dk_g)
            dvs.append(dv_g)
        dq, dk, dv, qk_sums = _qk_prep_bwd(dqs, dks, dvs, sv["z"], gq, gk, tabs)
        dz = jnp.concatenate([dq, dk, dv, dcv, dcg, dga, dgb], axis=1)
        wgrad("w_in", sv["h"], dz, l)
        dh = _mm_nt(dz, wf["w_in"], l, "in_proj_bwd")
        dx, n1_sums = _norm_bwd(dh, sv["x"], dx2, g1, sc1, "norm1_bwd")

        dmods[l] = jnp.concatenate([n1_sums[0], n1_sums[1], dgt1[0], n2_sums[0], n2_sums[1], dgt2[0]])
        small_g["b_ada"][l] = dmods[l]
        small_g["g_norm1"][l] = n1_sums[2]
        small_g["g_q"][l] = qk_sums[0]
        small_g["g_k"][l] = qk_sums[1]
        small_g["w_conv_dw"][l] = conv_sums[:CONV_K]
        small_g["b_conv_dw"][l] = conv_sums[CONV_K]
        small_g["g_conv_ln"][l] = ln_sums[0]
        small_g["b_conv_ln"][l] = ln_sums[1]
        small_g["g_norm2"][l] = n2_sums[2]
        small_g["w_ffn_dw"][l] = ffn_sums[:FFN_K]
        small_g["b_ffn_dw"][l] = ffn_sums[FFN_K]

    small_g = {n: jnp.stack(v) for n, v in small_g.items()}
    return loss, dx, bufs, small_g, cact


def kernel(x, c, positions, w_ada, b_ada, g_norm1, w_in, g_q, g_k, w_attn_proj, w_conv_dw, b_conv_dw, g_conv_ln, b_conv_ln, w_conv_out, w_o, g_norm2, w_ffn_in, w_ffn_dw, b_ffn_dw, w_ffn_down, loss_target, m_w_ada, m_b_ada, m_g_norm1, m_w_in, m_g_q, m_g_k, m_w_attn_proj, m_w_conv_dw, m_b_conv_dw, m_g_conv_ln, m_b_conv_ln, m_w_conv_out, m_w_o, m_g_norm2, m_w_ffn_in, m_w_ffn_dw, m_b_ffn_dw, m_w_ffn_down, v_w_ada, v_b_ada, v_g_norm1, v_w_in, v_g_q, v_g_k, v_w_attn_proj, v_w_conv_dw, v_b_conv_dw, v_g_conv_ln, v_b_conv_ln, v_w_conv_out, v_w_o, v_g_norm2, v_w_ffn_in, v_w_ffn_dw, v_b_ffn_dw, v_w_ffn_down):
    w = dict(w_ada=w_ada, b_ada=b_ada, g_norm1=g_norm1, w_in=w_in, g_q=g_q, g_k=g_k, w_attn_proj=w_attn_proj,
             w_conv_dw=w_conv_dw, b_conv_dw=b_conv_dw, g_conv_ln=g_conv_ln, b_conv_ln=b_conv_ln,
             w_conv_out=w_conv_out, w_o=w_o, g_norm2=g_norm2, w_ffn_in=w_ffn_in, w_ffn_dw=w_ffn_dw,
             b_ffn_dw=b_ffn_dw, w_ffn_down=w_ffn_down)
    m = dict(w_ada=m_w_ada, b_ada=m_b_ada, g_norm1=m_g_norm1, w_in=m_w_in, g_q=m_g_q, g_k=m_g_k,
             w_attn_proj=m_w_attn_proj, w_conv_dw=m_w_conv_dw, b_conv_dw=m_b_conv_dw, g_conv_ln=m_g_conv_ln,
             b_conv_ln=m_b_conv_ln, w_conv_out=m_w_conv_out, w_o=m_w_o, g_norm2=m_g_norm2, w_ffn_in=m_w_ffn_in,
             w_ffn_dw=m_w_ffn_dw, b_ffn_dw=m_b_ffn_dw, w_ffn_down=m_w_ffn_down)
    v = dict(w_ada=v_w_ada, b_ada=v_b_ada, g_norm1=v_g_norm1, w_in=v_w_in, g_q=v_g_q, g_k=v_g_k,
             w_attn_proj=v_w_attn_proj, w_conv_dw=v_w_conv_dw, b_conv_dw=v_b_conv_dw, g_conv_ln=v_g_conv_ln,
             b_conv_ln=v_b_conv_ln, w_conv_out=v_w_conv_out, w_o=v_w_o, g_norm2=v_g_norm2, w_ffn_in=v_w_ffn_in,
             w_ffn_dw=v_w_ffn_dw, b_ffn_dw=v_b_ffn_dw, w_ffn_down=v_w_ffn_down)
    xi, yi, ci = lax.axis_index("x"), lax.axis_index("y"), lax.axis_index("c")
    j = 2 * xi + yi
    L = w_ada.shape[0]
    D = x.shape[-1]

    mats = ("w_ada",) + BIG
    col = dict(COL_SHARDED, w_ada=True)
    shards = [w[n].astype(BF16) for n in mats]
    conv_shapes = [w["w_conv_dw"].shape, w["w_ffn_dw"].shape]
    gathered, small_g4 = _all_gather_weights(shards, _pack_rows([w["w_conv_dw"], w["w_ffn_dw"]]))
    wf = {}
    for n, g4, own in zip(mats, gathered, shards):
        _, _, r, cols = g4.shape
        g4 = lax.dynamic_update_index_in_dim(g4, own, j, 0)
        if col[n]:
            wf[n] = g4.transpose(1, 2, 0, 3).reshape(L, r, 4 * cols)
        else:
            wf[n] = g4.transpose(1, 0, 2, 3).reshape(L, 4 * r, cols)
    conv_parts = [_unpack_rows(small_g4[k], conv_shapes) for k in range(4)]
    small_w = {n: w[n] for n in SMALL}
    small_w["w_conv_dw"] = jnp.concatenate([p[0] for p in conv_parts], axis=2)
    small_w["w_ffn_dw"] = jnp.concatenate([p[1] for p in conv_parts], axis=2)

    loss, dx, bufs, small_g, cact = _local_step(x[0], c, positions[0], loss_target[0], wf, small_w)
    loss = lax.psum(loss, ("x", "y", "c"))

    dws = [bufs[n] for n in BIG]
    land1 = _rs_to_sibling(dws)
    c_idx = jnp.reshape(ci, (1,)).astype(jnp.int32)
    s1s = [_add_halves(a, b, c_idx, "rs_add_halves") for a, b in zip(dws, land1)]
    land2 = _rs_to_chips(s1s)
    cj_idx = jnp.stack([ci, j]).astype(jnp.int32)
    halves = [_add_chips(a, b, e, cj_idx, "rs_add_chips") for a, b, e in zip(dws, land1, land2)]
    joined = _rs_join_halves(halves)
    grads = {n: g.reshape(w[n].shape) for n, g in zip(BIG, joined)}

    part = _pack_rows([small_g[n] for n in SMALL] + [cact])
    parts = _all_gather_small(part)
    full_shapes = [small_g[n].shape for n in SMALL]
    summed = _unpack_rows(_sum_parts(parts), full_shapes)
    for n, g in zip(SMALL, summed):
        if n in ("w_conv_dw", "w_ffn_dw"):
            cols = w[n].shape[2]
            g = lax.dynamic_slice_in_dim(g, j * cols, cols, axis=2)
        grads[n] = g
    flat = parts.reshape(8, -1)
    off, dmod_all = 0, None
    for n, shp in zip(SMALL, full_shapes):
        size = 1
        for s in shp:
            size *= s
        if n == "b_ada":
            dmod_all = flat[:, off:off + size].reshape(8, L, 6 * D)
        off += size
    cact_all = flat[:, off:off + D]
    cols = w_ada.shape[2]
    dmod_cols = lax.dynamic_slice_in_dim(dmod_all, j * cols, cols, axis=2).transpose(1, 0, 2)
    grads["w_ada"] = _wada_grad(cact_all.T, dmod_cols)

    delta, new_m, new_v = {}, {}, {}
    for n in ("w_ada",) + BIG:
        shp = w[n].shape
        two_d = lambda a: a.reshape(shp[0] * shp[1], shp[2])
        dl, mn, vn = _adamw(two_d(w[n]), two_d(grads[n]), two_d(m[n]), two_d(v[n]), "adamw_" + n)
        delta[n], new_m[n], new_v[n] = dl.reshape(shp), mn.reshape(shp), vn.reshape(shp)
    small_shapes = [w[n].shape for n in SMALL]
    packs = [_pack_rows([d[n] for n in SMALL]) for d in (w, grads, m, v)]
    outs = _adamw(*packs, "adamw_small")
    for d, packed in zip((delta, new_m, new_v), outs):
        for n, a in zip(SMALL, _unpack_rows(packed, small_shapes)):
            d[n] = a

    return (loss, dx[None], *[grads[n] for n in WEIGHTS], *[delta[n] for n in WEIGHTS],
            *[new_m[n] for n in WEIGHTS], *[new_v[n] for n in WEIGHTS])
```
